```python
import math
import jax, jax.numpy as jnp
from jax import lax
import numpy as np

D_MODEL = 1024
BATCH = 8
SEQ = 4096
DEPTH = 1

D_MIX = D_MODEL
ATTN_HEADS = 8
ATTN_HEAD_DIM = 64
ATTN_WIDTH = ATTN_HEADS * ATTN_HEAD_DIM
MLSTM_HEADS = 4
MLSTM_HEAD_DIM = 128
MLSTM_WIDTH = MLSTM_HEADS * MLSTM_HEAD_DIM
DILATED_PATTERNS = ((128, 1), (512, 4), (2048, 16))
ATTN_BLOCK = 128
ROPE_THETA = 500000.0
ROPE_DIM = ATTN_HEAD_DIM // 4
CONV_WIDTH = 4
MLSTM_CHUNK = 64
D_FF = 4 * D_MODEL
PLE_DIM = 256
RMS_EPS = 1e-6
IN_PROJ_WIDTH = 3 * ATTN_WIDTH + 4 * MLSTM_WIDTH + 2 * MLSTM_HEADS

kernel_name = "hymba_dilated_attn_mlstm_block"


def rms_norm(x, g):
    xf = x.astype(jnp.float32)
    y = xf * lax.rsqrt(jnp.mean(xf * xf, axis=-1, keepdims=True) + RMS_EPS)
    return (y * g.astype(jnp.float32)).astype(x.dtype)


def partial_rope(x, pos):
    half = ROPE_DIM // 2
    inv_freq = jnp.power(ROPE_THETA, -jnp.arange(half, dtype=jnp.float32) / half)
    ang = pos.astype(jnp.float32)[:, None] * inv_freq[None, :]
    cos, sin = jnp.cos(ang), jnp.sin(ang)
    xr = x[..., :ROPE_DIM].astype(jnp.float32)
    x1, x2 = xr[..., :half], xr[..., half:]
    rot = jnp.concatenate([x1 * cos - x2 * sin, x2 * cos + x1 * sin], axis=-1)
    return jnp.concatenate([rot.astype(x.dtype), x[..., ROPE_DIM:]], axis=-1)


def dilated_window_partial(q, k, v, window, dilation):
    B, H, S, hd = q.shape
    n_back = window // dilation
    L = S // dilation
    nb = -(-L // ATTN_BLOCK)
    Lp = nb * ATTN_BLOCK
    blk = ATTN_BLOCK

    def to_sub(t):
        t = t.reshape(B, H, L, dilation, hd).transpose(0, 1, 3, 2, 4)
        t = jnp.pad(t, ((0, 0), (0, 0), (0, 0), (0, Lp - L), (0, 0)))
        return t.reshape(B, H, dilation, nb, blk, hd)

    def with_prev(t):
        prev = jnp.pad(t, ((0, 0), (0, 0), (0, 0), (1, 0), (0, 0), (0, 0)))[:, :, :, :-1]
        return jnp.concatenate([prev, t], axis=4)

    qs = to_sub(q)
    kb = with_prev(to_sub(k))
    vb = with_prev(to_sub(v))
    s = jnp.einsum('bhrnqd,bhrnkd->bhrnqk', qs, kb) * (1.0 / math.sqrt(hd))
    qi = jnp.arange(blk)[:, None]
    ki = jnp.arange(2 * blk)[None, :]
    dist = qi + blk - ki
    key_pos = jnp.arange(nb)[:, None, None] * blk + ki[None] - blk
    valid = (dist >= 0) & (dist <= n_back) & (key_pos >= 0)
    s = jnp.where(valid, s, -jnp.inf)
    m = jnp.max(s, axis=-1)
    pexp = jnp.exp(s - m[..., None])
    den = jnp.sum(pexp, axis=-1)
    num = jnp.einsum('bhrnqk,bhrnkd->bhrnqd', pexp, vb)

    def back_vec(t):
        t = t.reshape(B, H, dilation, Lp, hd)[:, :, :, :L]
        return t.transpose(0, 1, 3, 2, 4).reshape(B, H, S, hd)

    def back_scalar(t):
        t = t.reshape(B, H, dilation, Lp)[:, :, :, :L]
        return t.transpose(0, 1, 3, 2).reshape(B, H, S)

    return back_vec(num), back_scalar(m), back_scalar(den)


def dilated_attention(q, k, v):
    parts = [dilated_window_partial(q, k, v, w, d) for (w, d) in DILATED_PATTERNS]
    m_all = jnp.max(jnp.stack([pm for (_, pm, _) in parts]), axis=0)
    num = sum(pn * jnp.exp(pm - m_all)[..., None] for (pn, pm, _) in parts)
    den = sum(pd * jnp.exp(pm - m_all) for (_, pm, pd) in parts)
    return num / den[..., None]


def mlstm_chunkwise(q, k, v, i_pre, f_pre):
    B, H, S, dh = q.shape
    Lc = MLSTM_CHUNK
    nc = S // Lc
    k = k * (1.0 / math.sqrt(dh))
    logf = jax.nn.log_sigmoid(f_pre)

    def chunks(t):
        t = t.reshape(B, H, nc, Lc, *t.shape[3:])
        return jnp.moveaxis(t, 2, 0)

    xs = (chunks(q), chunks(k), chunks(v), chunks(i_pre), chunks(logf))
    causal = jnp.tril(jnp.ones((Lc, Lc), dtype=bool))

    def step(carry, inp):
        C, n, m = carry
        qc, kc, vc, ic, fc = inp
        b = jnp.cumsum(fc, axis=-1)
        log_d = b[..., :, None] - b[..., None, :] + ic[..., None, :]
        log_d = jnp.where(causal, log_d, -jnp.inf)
        log_inter = b + m[..., None]
        m_t = jnp.maximum(log_inter, jnp.max(log_d, axis=-1))
        w_intra = jnp.exp(log_d - m_t[..., None])
        w_inter = jnp.exp(log_inter - m_t)
        qk = jnp.einsum('bhtd,bhsd->bhts', qc, kc) * w_intra
        num = (w_inter[..., None] * jnp.einsum('bhtd,bhde->bhte', qc, C)
               + jnp.einsum('bhts,bhse->bhte', qk, vc))
        den = w_inter * jnp.einsum('bhtd,bhd->bht', qc, n) + jnp.sum(qk, axis=-1)
        h = num / jnp.maximum(jnp.abs(den), jnp.exp(-m_t))[..., None]
        b_last = b[..., -1]
        log_s = b_last[..., None] - b + ic
        m_new = jnp.maximum(b_last + m, jnp.max(log_s, axis=-1))
        decay = jnp.exp(b_last + m - m_new)
        ws = jnp.exp(log_s - m_new[..., None])
        C_new = decay[..., None, None] * C + jnp.einsum('bhs,bhsd,bhse->bhde', ws, kc, vc)
        n_new = decay[..., None] * n + jnp.einsum('bhs,bhsd->bhd', ws, kc)
        return (C_new, n_new, m_new), h

    init = (jnp.zeros((B, H, dh, dh), jnp.float32),
            jnp.zeros((B, H, dh), jnp.float32),
            jnp.zeros((B, H), jnp.float32))
    _, hs = lax.scan(step, init, xs)
    return jnp.moveaxis(hs, 0, 2).reshape(B, H, S, dh)


def causal_short_conv(x, w, b):
    S = x.shape[1]
    xp = jnp.pad(x, ((0, 0), (CONV_WIDTH - 1, 0), (0, 0)))
    return sum(w[j] * xp[:, j:j + S] for j in range(CONV_WIDTH)) + b


def split_heads(t, n_heads, hd):
    B, S, _ = t.shape
    return t.reshape(B, S, n_heads, hd).transpose(0, 2, 1, 3)


def merge_heads(t):
    B, H, S, hd = t.shape
    return t.transpose(0, 2, 1, 3).reshape(B, S, H * hd)


def setup_inputs(seed: int = 0) -> dict:
    key = jax.random.key(seed)
    ks = jax.random.split(key, 20)
    f32 = jnp.float32

    def nrm(k, shape, scale):
        return jax.random.normal(k, shape, f32) * scale

    x = jax.random.normal(ks[0], (BATCH, SEQ, D_MODEL), f32)
    p = jax.random.normal(ks[1], (DEPTH, BATCH, SEQ, PLE_DIM), f32)
    norm_mix_g = 1.0 + nrm(ks[2], (DEPTH, D_MODEL), 0.02)
    w_in = nrm(ks[3], (DEPTH, D_MODEL, IN_PROJ_WIDTH), D_MODEL ** -0.5)
    conv_w = nrm(ks[4], (DEPTH, CONV_WIDTH, 2 * MLSTM_WIDTH), CONV_WIDTH ** -0.5)
    conv_b = nrm(ks[5], (DEPTH, 2 * MLSTM_WIDTH), 0.01)
    i_bias = nrm(ks[6], (DEPTH, MLSTM_HEADS), 0.1)
    f_bias = (jnp.linspace(3.0, 6.0, MLSTM_HEADS, dtype=f32)[None, :]
              + nrm(ks[7], (DEPTH, MLSTM_HEADS), 0.1))
    gate_b = jnp.concatenate([i_bias, f_bias], axis=-1)
    mlstm_norm_g = 1.0 + nrm(ks[8], (DEPTH, MLSTM_WIDTH), 0.02)
    w_out = nrm(ks[9], (DEPTH, D_MIX, D_MODEL), D_MIX ** -0.5)
    norm_mlp_g = 1.0 + nrm(ks[10], (DEPTH, D_MODEL), 0.02)
    w_up = nrm(ks[11], (DEPTH, D_MODEL, D_FF), D_MODEL ** -0.5)
    w_down = nrm(ks[12], (DEPTH, D_FF, D_MODEL), D_FF ** -0.5)
    norm_ple_g = 1.0 + nrm(ks[13], (DEPTH, D_MODEL), 0.02)
    w_ple_gate = nrm(ks[14], (DEPTH, D_MODEL, D_MODEL), D_MODEL ** -0.5)
    w_ple = nrm(ks[15], (DEPTH, PLE_DIM, D_MODEL), PLE_DIM ** -0.5)
    final_norm_g = 1.0 + nrm(ks[16], (D_MODEL,), 0.02)
    return {"x": x, "p": p, "norm_mix_g": norm_mix_g, "w_in": w_in,
            "conv_w": conv_w, "conv_b": conv_b, "gate_b": gate_b,
            "mlstm_norm_g": mlstm_norm_g, "w_out": w_out, "norm_mlp_g": norm_mlp_g,
            "w_up": w_up, "w_down": w_down, "norm_ple_g": norm_ple_g,
            "w_ple_gate": w_ple_gate, "w_ple": w_ple, "final_norm_g": final_norm_g}


def reference(x, p, norm_mix_g, w_in, conv_w, conv_b, gate_b, mlstm_norm_g, w_out,
              norm_mlp_g, w_up, w_down, norm_ple_g, w_ple_gate, w_ple, final_norm_g):
    B, S, _ = x.shape
    pos = jnp.arange(S, dtype=jnp.int32)
    A, M = ATTN_WIDTH, MLSTM_WIDTH
    h = x
    for layer in range(DEPTH):
        u = rms_norm(h, norm_mix_g[layer])
        proj = u @ w_in[layer]
        aq = proj[..., 0:A]
        ak = proj[..., A:2 * A]
        av = proj[..., 2 * A:3 * A]
        o0 = 3 * A
        mqk = proj[..., o0:o0 + 2 * M]
        mv = proj[..., o0 + 2 * M:o0 + 3 * M]
        mo = proj[..., o0 + 3 * M:o0 + 4 * M]
        gates = (proj[..., o0 + 4 * M:] + gate_b[layer]).astype(jnp.float32)

        qa = partial_rope(split_heads(aq, ATTN_HEADS, ATTN_HEAD_DIM), pos).astype(jnp.float32)
        ka = partial_rope(split_heads(ak, ATTN_HEADS, ATTN_HEAD_DIM), pos).astype(jnp.float32)
        va = split_heads(av, ATTN_HEADS, ATTN_HEAD_DIM).astype(jnp.float32)
        attn_out = merge_heads(dilated_attention(qa, ka, va)).astype(h.dtype)

        qk_c = jax.nn.silu(causal_short_conv(mqk, conv_w[layer], conv_b[layer]))
        qm = split_heads(qk_c[..., :M], MLSTM_HEADS, MLSTM_HEAD_DIM).astype(jnp.float32)
        km = split_heads(qk_c[..., M:], MLSTM_HEADS, MLSTM_HEAD_DIM).astype(jnp.float32)
        vm = split_heads(mv, MLSTM_HEADS, MLSTM_HEAD_DIM).astype(jnp.float32)
        i_pre = gates[..., :MLSTM_HEADS].transpose(0, 2, 1)
        f_pre = gates[..., MLSTM_HEADS:].transpose(0, 2, 1)
        hm = mlstm_chunkwise(qm, km, vm, i_pre, f_pre)
        hm = hm * lax.rsqrt(jnp.mean(hm * hm, axis=-1, keepdims=True) + RMS_EPS)
        hm = hm * mlstm_norm_g[layer].astype(jnp.float32).reshape(MLSTM_HEADS, 1, MLSTM_HEAD_DIM)
        mlstm_out = (jax.nn.sigmoid(mo.astype(jnp.float32)) * merge_heads(hm)).astype(h.dtype)

        mix = jnp.concatenate([attn_out, mlstm_out], axis=-1)
        h = h + mix @ w_out[layer]

        u = rms_norm(h, norm_mlp_g[layer])
        h = h + jnp.square(jax.nn.relu(u @ w_up[layer])) @ w_down[layer]

        gate = jax.nn.sigmoid(rms_norm(h, norm_ple_g[layer]) @ w_ple_gate[layer])
        h = h + gate * (p[layer] @ w_ple[layer])
    return rms_norm(h, final_norm_g)
```

```python
import functools
import math

import jax
import jax.numpy as jnp
from jax import lax
from jax.experimental import pallas as pl
from jax.experimental.pallas import tpu as pltpu

D_MODEL = 1024
ATTN_HEADS = 8
ATTN_HEAD_DIM = 64
ATTN_WIDTH = ATTN_HEADS * ATTN_HEAD_DIM
MLSTM_HEADS = 4
MLSTM_HEAD_DIM = 128
MLSTM_WIDTH = MLSTM_HEADS * MLSTM_HEAD_DIM
DILATIONS = (1, 4, 16)
ATTN_BLOCK = 128
ROPE_THETA = 500000.0
ROPE_DIM = ATTN_HEAD_DIM // 4
CONV_WIDTH = 4
D_FF = 4 * D_MODEL
PLE_DIM = 256
RMS_EPS = 1e-6

LANES = 128
HEAD_PAIRS = ATTN_WIDTH // LANES
MAIN_WIDTH = 3 * ATTN_WIDTH + 4 * MLSTM_WIDTH
MLSTM_CHUNK = 256
CONV_HALO = 16
VMEM_LIMIT = 56 * 1024 * 1024

F32 = jnp.float32
BF16 = jnp.bfloat16
NEG_INF = float("-inf")


def _rms(x, g):
    return x * lax.rsqrt(jnp.mean(x * x, axis=-1, keepdims=True) + RMS_EPS) * g


def _resident(shape):
    return pl.BlockSpec(shape, lambda *_: (0,) * len(shape), pipeline_mode=pl.Buffered(1))


def _in_proj_kernel(x_ref, g_ref, w_ref, wg_ref, gb_ref, cos_ref, sin_ref,
                    q_ref, k_ref, v_ref, mqk_ref, mv_ref, mo_ref, gt_ref, *, tm):
    u = _rms(x_ref[...], g_ref[...]).astype(BF16)
    pos0 = pl.multiple_of(pl.program_id(1) * tm, tm)
    cos = cos_ref[pl.ds(pos0, tm), :]
    sin = sin_ref[pl.ds(pos0, tm), :]
    lane = lax.broadcasted_iota(jnp.int32, (tm, LANES), 1)
    first_half = (lane % ATTN_HEAD_DIM) < (ROPE_DIM // 2)

    def proj(c0, width):
        return jnp.dot(u, w_ref[:, c0:c0 + width], preferred_element_type=F32)

    def rope_store(y, o_ref):
        for j in range(HEAD_PAIRS):
            yj = y[:, j * LANES:(j + 1) * LANES]
            partner = jnp.where(first_half,
                                pltpu.roll(yj, LANES - ROPE_DIM // 2, axis=1),
                                pltpu.roll(yj, ROPE_DIM // 2, axis=1))
            o_ref[j] = (yj * cos + partner * sin).astype(BF16)

    A, M = ATTN_WIDTH, MLSTM_WIDTH
    rope_store(proj(0, A), q_ref)
    rope_store(proj(A, A), k_ref)
    yv = proj(2 * A, A)
    for j in range(HEAD_PAIRS):
        v_ref[j] = yv[:, j * LANES:(j + 1) * LANES].astype(BF16)
    mqk_ref[...] = proj(3 * A, 2 * M).astype(BF16)
    mv_ref[...] = proj(3 * A + 2 * M, M).astype(BF16)
    mo_ref[...] = proj(3 * A + 3 * M, M).astype(BF16)
    gates = jnp.dot(u, wg_ref[...], preferred_element_type=F32) + gb_ref[...]
    gt_ref[...] = gates.T[:2 * MLSTM_HEADS, :]


def _in_proj(x, g, w_main, w_gate, gate_b, cos_t, sin_t, tm=512):
    B, S, D = x.shape
    grid = (B, S // tm)
    tok = lambda width: pl.BlockSpec((None, tm, width), lambda b, i: (b, i, 0))
    pair = pl.BlockSpec((None, HEAD_PAIRS, tm, LANES), lambda b, i: (b, 0, i, 0))
    pair_shape = jax.ShapeDtypeStruct((B, HEAD_PAIRS, S, LANES), BF16)
    return pl.pallas_call(
        functools.partial(_in_proj_kernel, tm=tm),
        name="in_proj",
        grid=grid,
        in_specs=[tok(D), _resident((1, D)), _resident(w_main.shape), _resident(w_gate.shape),
                  _resident((1, LANES)), _resident(cos_t.shape), _resident(sin_t.shape)],
        out_specs=[pair, pair, pair, tok(2 * MLSTM_WIDTH), tok(MLSTM_WIDTH), tok(MLSTM_WIDTH),
                   pl.BlockSpec((None, 2 * MLSTM_HEADS, tm), lambda b, i: (b, 0, i))],
        out_shape=[pair_shape, pair_shape, pair_shape,
                   jax.ShapeDtypeStruct((B, S, 2 * MLSTM_WIDTH), BF16),
                   jax.ShapeDtypeStruct((B, S, MLSTM_WIDTH), BF16),
                   jax.ShapeDtypeStruct((B, S, MLSTM_WIDTH), BF16),
                   jax.ShapeDtypeStruct((B, 2 * MLSTM_HEADS, S), F32)],
        compiler_params=pltpu.CompilerParams(
            dimension_semantics=("parallel", "parallel"), vmem_limit_bytes=VMEM_LIMIT),
    )(x, g, w_main, w_gate, gate_b, cos_t, sin_t)


def _attn_kernel(q1, k1, v1, q4, k4, v4, q16, k16, v16, o1, o4, o16,
                 acc1, acc4, acc16, st1, st4, st16, bias_ref, *, seq):
    blk = ATTN_BLOCK
    qi = lax.broadcasted_iota(jnp.int32, (blk, 2 * blk), 0)
    ki = lax.broadcasted_iota(jnp.int32, (blk, 2 * blk), 1)
    dist = qi + blk - ki
    valid = (dist >= 0) & (dist <= blk)
    bias_ref[1] = jnp.where(valid, 0.0, NEG_INF)
    bias_ref[0] = jnp.where(valid & (ki >= blk), 0.0, NEG_INF)

    lane_q = lax.broadcasted_iota(jnp.int32, (blk, LANES), 1)
    lane_kv = lax.broadcasted_iota(jnp.int32, (2 * blk, LANES), 1)
    head0_q = lane_q < ATTN_HEAD_DIM
    head0_kv = lane_kv < ATTN_HEAD_DIM

    patterns = ((q1, k1, v1, o1, acc1, st1, DILATIONS[0]),
                (q4, k4, v4, o4, acc4, st4, DILATIONS[1]),
                (q16, k16, v16, o16, acc16, st16, DILATIONS[2]))

    def token_rows(st_ref, row0, r, d):
        if d == 1:
            return st_ref.at[pl.ds(row0, blk), 0:2]
        return st_ref.at[pl.ds(row0 * d + r, blk, stride=d), 0:2]

    for (q_ref, k_ref, v_ref, _, acc_ref, st_ref, d) in patterns:
        nb = seq // d // blk

        def block(i, carry, q_ref=q_ref, k_ref=k_ref, v_ref=v_ref, acc_ref=acc_ref, st_ref=st_ref, d=d, nb=nb):
            r = i // nb
            n = i % nb
            row0 = pl.multiple_of(n * blk, blk)
            prow0 = pl.multiple_of(jnp.maximum(n - 1, 0) * blk, blk)
            lanes = pl.ds(pl.multiple_of(r * LANES, LANES), LANES)
            q = q_ref[pl.ds(row0, blk), lanes]
            kk = jnp.concatenate([k_ref[pl.ds(prow0, blk), lanes], k_ref[pl.ds(row0, blk), lanes]], axis=0)
            vv = jnp.concatenate([v_ref[pl.ds(prow0, blk), lanes], v_ref[pl.ds(row0, blk), lanes]], axis=0)
            bias = bias_ref[jnp.minimum(n, 1)]
            outs, lses = [], []
            for hh in range(2):
                mine_q = head0_q if hh == 0 else ~head0_q
                mine_kv = head0_kv if hh == 0 else ~head0_kv
                qh = jnp.where(mine_q, q, jnp.zeros_like(q))
                s = lax.dot_general(qh, kk, (((1,), (1,)), ((), ())), preferred_element_type=F32) + bias
                m = jnp.max(s, axis=1, keepdims=True)
                p = jnp.exp(s - m).astype(BF16)
                vh = jnp.where(mine_kv, vv, jnp.ones_like(vv))
                o = jnp.dot(p, vh, preferred_element_type=F32)
                den = o[:, ATTN_HEAD_DIM:ATTN_HEAD_DIM + 1] if hh == 0 else o[:, 0:1]
                outs.append(o * (1.0 / den))
                lses.append(m + jnp.log(den))
            acc_ref[pl.ds(row0, blk), lanes] = jnp.where(head0_q, outs[0], outs[1])
            token_rows(st_ref, row0, r, d)[...] = jnp.concatenate(lses, axis=1)
            return carry

        lax.fori_loop(0, d * nb, block, 0)

    rows_b = 512

    def weights(c, carry):
        rows = pl.ds(pl.multiple_of(c * rows_b, rows_b), rows_b)
        a, b, e = st1[rows, 0:2], st4[rows, 0:2], st16[rows, 0:2]
        top = jnp.maximum(jnp.maximum(a, b), e)
        ea, eb, ee = jnp.exp(a - top), jnp.exp(b - top), jnp.exp(e - top)
        inv = 1.0 / (ea + eb + ee)
        st1[rows, 0:2] = ea * inv
        st4[rows, 0:2] = eb * inv
        st16[rows, 0:2] = ee * inv
        return carry

    lax.fori_loop(0, seq // rows_b, weights, 0)

    for (_, _, _, o_ref, acc_ref, st_ref, d) in patterns:
        nb = seq // d // blk

        def scale(i, carry, o_ref=o_ref, acc_ref=acc_ref, st_ref=st_ref, d=d, nb=nb):
            r = i // nb
            n = i % nb
            row0 = pl.multiple_of(n * blk, blk)
            lanes = pl.ds(pl.multiple_of(r * LANES, LANES), LANES)
            w = token_rows(st_ref, row0, r, d)[...]
            wfull = jnp.where(head0_q, w[:, 0:1], w[:, 1:2])
            o_ref[pl.ds(row0, blk), lanes] = (acc_ref[pl.ds(row0, blk), lanes] * wfull).astype(BF16)
            return carry

        lax.fori_loop(0, d * nb, scale, 0)


def _dilated_attention(q, k, v):
    B, HP, S, _ = q.shape
    views, specs = [], []
    for d in DILATIONS:
        shape = (B, HP, S // d, d * LANES)
        for t in (q, k, v):
            views.append(t.reshape(shape))
        specs += [pl.BlockSpec((None, None, S // d, d * LANES), lambda b, h: (b, h, 0, 0))] * 3
    out_specs = [pl.BlockSpec((None, None, S // d, d * LANES), lambda b, h: (b, h, 0, 0)) for d in DILATIONS]
    out_shape = [jax.ShapeDtypeStruct((B, HP, S // d, d * LANES), BF16) for d in DILATIONS]
    scratch = ([pltpu.VMEM((S // d, d * LANES), F32) for d in DILATIONS]
               + [pltpu.VMEM((S, LANES), F32) for _ in DILATIONS]
               + [pltpu.VMEM((2, ATTN_BLOCK, 2 * ATTN_BLOCK), F32)])
    outs = pl.pallas_call(
        functools.partial(_attn_kernel, seq=S),
        name="dilated_attention",
        grid=(B, HP),
        in_specs=specs,
        out_specs=out_specs,
        out_shape=out_shape,
        scratch_shapes=scratch,
        compiler_params=pltpu.CompilerParams(
            dimension_semantics=("parallel", "parallel"), vmem_limit_bytes=VMEM_LIMIT),
    )(*views)
    return [o.reshape(B, HP, S, LANES) for o in outs]


def _mlstm_kernel(q_ref, k_ref, v_ref, mo_ref, gt_ref, cwq_ref, cwk_ref, cbq_ref, cbk_ref, ng_ref,
                  o_ref, c_ref, m_ref, brow_ref, arow_ref, lf_ref, *, seq):
    L = MLSTM_CHUNK
    nc = seq // L
    dh = MLSTM_HEAD_DIM
    h_idx = pl.program_id(1)

    i_mat = gt_ref[h_idx]
    f_mat = gt_ref[MLSTM_HEADS + h_idx]
    logf = jnp.minimum(f_mat, 0.0) - jnp.log(1.0 + jnp.exp(-jnp.abs(f_mat)))
    upper = (lax.broadcasted_iota(jnp.int32, (L, L), 0) <= lax.broadcasted_iota(jnp.int32, (L, L), 1)).astype(F32)
    b_mat = jnp.dot(logf, upper, preferred_element_type=F32, precision=lax.Precision.HIGHEST)
    brow_ref[...] = b_mat
    arow_ref[...] = i_mat - b_mat
    lf_ref[...] = logf

    c_ref[...] = jnp.zeros_like(c_ref)
    m_ref[...] = jnp.zeros_like(m_ref)

    t_idx = lax.broadcasted_iota(jnp.int32, (L, L), 0)
    s_idx = lax.broadcasted_iota(jnp.int32, (L, L), 1)
    causal = s_idx <= t_idx
    ones_col = (lax.broadcasted_iota(jnp.int32, (L, dh), 1) == 0).astype(BF16)
    k_scale = 1.0 / math.sqrt(dh)

    def conv_silu(x_ref, w_ref, b_ref, c, r0):
        cur = x_ref[pl.ds(r0, L), :].astype(F32)
        h0 = pl.multiple_of(jnp.maximum(r0 - CONV_HALO, 0), CONV_HALO)
        halo = x_ref[pl.ds(h0, CONV_HALO), :].astype(F32) * (c > 0).astype(F32)
        ext = jnp.concatenate([halo, cur], axis=0)
        acc = b_ref[...] + w_ref[CONV_WIDTH - 1:CONV_WIDTH, :] * cur
        for j in range(1, CONV_WIDTH):
            acc = acc + w_ref[CONV_WIDTH - 1 - j:CONV_WIDTH - j, :] * pltpu.roll(ext, j, axis=0)[CONV_HALO:, :]
        return acc * jax.nn.sigmoid(acc)

    def chunk(c, carry):
        r0 = pl.multiple_of(c * L, L)
        q = conv_silu(q_ref, cwq_ref, cbq_ref, c, r0)
        k = conv_silu(k_ref, cwk_ref, cbk_ref, c, r0) * k_scale
        kT = k.T
        q_b = q.astype(BF16)
        v_aug = jnp.concatenate([v_ref[pl.ds(r0, L), :], ones_col], axis=1)

        b_row = brow_ref[pl.ds(c, 1), :]
        a_row = arow_ref[pl.ds(c, 1), :]
        lf_row = lf_ref[pl.ds(c, 1), :]
        m_prev = m_ref[0:1, 0:1]

        b_col = jnp.sum(jnp.where(causal, lf_row, 0.0), axis=1, keepdims=True)
        log_d = jnp.where(causal, b_col + a_row, NEG_INF)
        m_t = jnp.maximum(b_col + m_prev, jnp.max(log_d, axis=1, keepdims=True))
        w_intra = jnp.exp(log_d - m_t)
        w_inter = jnp.exp(b_col + m_prev - m_t)
        qk = jnp.dot(q_b, kT.astype(BF16), preferred_element_type=F32) * w_intra
        res = (w_inter * jnp.dot(q_b, c_ref[...].astype(BF16), preferred_element_type=F32)
               + jnp.dot(qk.astype(BF16), v_aug, preferred_element_type=F32))
        num = res[:, :dh]
        den = res[:, dh:dh + 1]
        h = num / jnp.maximum(jnp.abs(den), jnp.exp(-m_t))

        b_last = b_row[:, L - 1:L]
        log_s = b_last + a_row
        m_new = jnp.maximum(b_last + m_prev, jnp.max(log_s, axis=1, keepdims=True))
        decay = jnp.exp(b_last + m_prev - m_new)
        ws = jnp.exp(log_s - m_new)
        c_ref[...] = decay * c_ref[...] + jnp.dot((kT * ws).astype(BF16), v_aug, preferred_element_type=F32)
        m_ref[...] = jnp.broadcast_to(m_new, m_ref.shape)

        hn = _rms(h, ng_ref[...])
        o_ref[pl.ds(r0, L), :] = (jax.nn.sigmoid(mo_ref[pl.ds(r0, L), :].astype(F32)) * hn).astype(BF16)
        return carry

    lax.fori_loop(0, nc, chunk, 0)


def _mlstm(mqk, mv, mo, gates_t, conv_w, conv_b, norm_g):
    B, S, _ = mv.shape
    H, dh, L = MLSTM_HEADS, MLSTM_HEAD_DIM, MLSTM_CHUNK
    nc = S // L
    gt = gates_t.reshape(B, 2 * H, nc, L)
    col = lambda off: pl.BlockSpec((None, S, dh), lambda b, h: (b, 0, off + h))
    vec = lambda rows, off: pl.BlockSpec((rows, dh), lambda b, h: (0, off + h))
    return pl.pallas_call(
        functools.partial(_mlstm_kernel, seq=S),
        name="mlstm",
        grid=(B, H),
        in_specs=[col(0), col(H), col(0), col(0),
                  pl.BlockSpec((None, 2 * H, nc, L), lambda b, h: (b, 0, 0, 0)),
                  vec(CONV_WIDTH, 0), vec(CONV_WIDTH, H), vec(1, 0), vec(1, H), vec(1, 0)],
        out_specs=col(0),
        out_shape=jax.ShapeDtypeStruct((B, S, H * dh), BF16),
        scratch_shapes=[pltpu.VMEM((dh, 2 * dh), F32), pltpu.VMEM((8, LANES), F32),
                        pltpu.VMEM((nc, L), F32), pltpu.VMEM((nc, L), F32), pltpu.VMEM((nc, L), F32)],
        compiler_params=pltpu.CompilerParams(
            dimension_semantics=("parallel", "parallel"), vmem_limit_bytes=VMEM_LIMIT),
    )(mqk, mqk, mv, mo, gt, conv_w, conv_w, conv_b, conv_b, norm_g)


def _tail_kernel(x_ref, a1_ref, a4_ref, a16_ref, ml_ref, p_ref, wo_ref, g_mlp_ref, wu_ref, wd_ref,
                 g_ple_ref, wpg_ref, wple_ref, g_fin_ref, o_ref, *, ff_chunk, final_norm):
    attn = [(a1_ref[j].astype(F32) + a4_ref[j].astype(F32) + a16_ref[j].astype(F32)).astype(BF16)
            for j in range(HEAD_PAIRS)]
    mix = jnp.concatenate(attn + [ml_ref[...]], axis=1)
    h = x_ref[...] + jnp.dot(mix, wo_ref[...], preferred_element_type=F32)

    u = _rms(h, g_mlp_ref[...]).astype(BF16)
    for c0 in range(0, D_FF, ff_chunk):
        a = jnp.dot(u, wu_ref[:, c0:c0 + ff_chunk], preferred_element_type=F32)
        a = jnp.square(jnp.maximum(a, 0.0)).astype(BF16)
        h = h + jnp.dot(a, wd_ref[c0:c0 + ff_chunk, :], preferred_element_type=F32)

    gate = jax.nn.sigmoid(jnp.dot(_rms(h, g_ple_ref[...]).astype(BF16), wpg_ref[...], preferred_element_type=F32))
    h = h + gate * jnp.dot(p_ref[...].astype(BF16), wple_ref[...], preferred_element_type=F32)
    o_ref[...] = _rms(h, g_fin_ref[...]) if final_norm else h


def _tail(x, a1, a4, a16, ml, p, w_out, g_mlp, w_up, w_down, g_ple, w_pg, w_ple, g_fin, final_norm,
          tm=512, ff_chunk=1024):
    B, S, D = x.shape
    tok = lambda width: pl.BlockSpec((None, tm, width), lambda b, i: (b, i, 0))
    pair = pl.BlockSpec((None, HEAD_PAIRS, tm, LANES), lambda b, i: (b, 0, i, 0))
    return pl.pallas_call(
        functools.partial(_tail_kernel, ff_chunk=ff_chunk, final_norm=final_norm),
        name="tail",
        grid=(B, S // tm),
        in_specs=[tok(D), pair, pair, pair, tok(MLSTM_WIDTH), tok(PLE_DIM),
                  _resident(w_out.shape), _resident((1, D)), _resident(w_up.shape), _resident(w_down.shape),
                  _resident((1, D)), _resident(w_pg.shape), _resident(w_ple.shape), _resident((1, D))],
        out_specs=tok(D),
        out_shape=jax.ShapeDtypeStruct((B, S, D), F32),
        compiler_params=pltpu.CompilerParams(
            dimension_semantics=("parallel", "parallel"), vmem_limit_bytes=VMEM_LIMIT),
    )(x, a1, a4, a16, ml, p, w_out, g_mlp, w_up, w_down, g_ple, w_pg, w_ple, g_fin)


def _rope_tables(seq):
    half = ROPE_DIM // 2
    inv_freq = jnp.power(ROPE_THETA, -jnp.arange(half, dtype=F32) / half)
    ang = jnp.arange(seq, dtype=jnp.int32).astype(F32)[:, None] * inv_freq[None, :]
    cos, sin = jnp.cos(ang), jnp.sin(ang)
    rest = ATTN_HEAD_DIM - ROPE_DIM
    cos_head = jnp.concatenate([cos, cos, jnp.ones((seq, rest), F32)], axis=1)
    sin_head = jnp.concatenate([-sin, sin, jnp.zeros((seq, rest), F32)], axis=1)
    reps = LANES // ATTN_HEAD_DIM
    return jnp.tile(cos_head, (1, reps)), jnp.tile(sin_head, (1, reps))


def kernel(x, p, norm_mix_g, w_in, conv_w, conv_b, gate_b, mlstm_norm_g, w_out, norm_mlp_g, w_up, w_down,
           norm_ple_g, w_ple_gate, w_ple, final_norm_g):
    B, S, D = x.shape
    depth = w_in.shape[0]
    cos_t, sin_t = _rope_tables(S)
    row = lambda v: v.reshape(1, -1).astype(F32)
    n_gate = 2 * MLSTM_HEADS
    h = x
    for layer in range(depth):
        w = w_in[layer]
        w_main = jnp.concatenate([w[:, :ATTN_WIDTH] * (1.0 / math.sqrt(ATTN_HEAD_DIM)),
                                  w[:, ATTN_WIDTH:MAIN_WIDTH]], axis=1).astype(BF16)
        w_gate = jnp.pad(w[:, MAIN_WIDTH:], ((0, 0), (0, LANES - n_gate))).astype(BF16)
        gb = jnp.pad(gate_b[layer].astype(F32), (0, LANES - n_gate)).reshape(1, LANES)
        q, k, v, mqk, mv, mo, gates_t = _in_proj(h, row(norm_mix_g[layer]), w_main, w_gate, gb, cos_t, sin_t)
        a1, a4, a16 = _dilated_attention(q, k, v)
        ml = _mlstm(mqk, mv, mo, gates_t, conv_w[layer].astype(F32), row(conv_b[layer]), row(mlstm_norm_g[layer]))
        h = _tail(h, a1, a4, a16, ml, p[layer], w_out[layer].astype(BF16), row(norm_mlp_g[layer]),
                  w_up[layer].astype(BF16), w_down[layer].astype(BF16), row(norm_ple_g[layer]),
                  w_ple_gate[layer].astype(BF16), w_ple[layer].astype(BF16), row(final_norm_g),
                  final_norm=(layer == depth - 1))
    return h
```

```python
import functools
import math

import jax
import jax.numpy as jnp
from jax import lax
from jax.experimental import pallas as pl
from jax.experimental.pallas import tpu as pltpu

D_MODEL = 1024
ATTN_HEADS = 8
ATTN_HEAD_DIM = 64
ATTN_WIDTH = ATTN_HEADS * ATTN_HEAD_DIM
MLSTM_HEADS = 4
MLSTM_HEAD_DIM = 128
MLSTM_WIDTH = MLSTM_HEADS * MLSTM_HEAD_DIM
DILATIONS = (1, 4, 16)
ATTN_BLOCK = 128
ROPE_THETA = 500000.0
ROPE_DIM = ATTN_HEAD_DIM // 4
CONV_WIDTH = 4
D_FF = 4 * D_MODEL
PLE_DIM = 256
RMS_EPS = 1e-6

LANES = 128
HEAD_PAIRS = ATTN_WIDTH // LANES
MAIN_WIDTH = 3 * ATTN_WIDTH + 4 * MLSTM_WIDTH
MLSTM_CHUNK = 256
CONV_HALO = 16
VMEM_LIMIT = 56 * 1024 * 1024

F32 = jnp.float32
BF16 = jnp.bfloat16
NEG_INF = float("-inf")


def _rms(x, g):
    return x * lax.rsqrt(jnp.mean(x * x, axis=-1, keepdims=True) + RMS_EPS) * g


def _resident(shape):
    return pl.BlockSpec(shape, lambda *_: (0,) * len(shape), pipeline_mode=pl.Buffered(1))


def _in_proj_kernel(x_ref, g_ref, w_ref, wg_ref, gb_ref, cos_ref, sin_ref,
                    q_ref, k_ref, v_ref, mqk_ref, mv_ref, mo_ref, gt_ref, *, tm):
    u = _rms(x_ref[...], g_ref[...]).astype(BF16)
    pos0 = pl.multiple_of(pl.program_id(1) * tm, tm)
    cos = cos_ref[pl.ds(pos0, tm), :]
    sin = sin_ref[pl.ds(pos0, tm), :]
    lane = lax.broadcasted_iota(jnp.int32, (tm, LANES), 1)
    first_half = (lane % ATTN_HEAD_DIM) < (ROPE_DIM // 2)

    def proj(c0, width):
        return jnp.dot(u, w_ref[:, c0:c0 + width], preferred_element_type=F32)

    def rope_store(y, o_ref):
        for j in range(HEAD_PAIRS):
            yj = y[:, j * LANES:(j + 1) * LANES]
            partner = jnp.where(first_half,
                                pltpu.roll(yj, LANES - ROPE_DIM // 2, axis=1),
                                pltpu.roll(yj, ROPE_DIM // 2, axis=1))
            o_ref[:, j * LANES:(j + 1) * LANES] = (yj * cos + partner * sin).astype(BF16)

    A, M = ATTN_WIDTH, MLSTM_WIDTH
    rope_store(proj(0, A), q_ref)
    rope_store(proj(A, A), k_ref)
    v_ref[...] = proj(2 * A, A).astype(BF16)
    mqk_ref[...] = proj(3 * A, 2 * M).astype(BF16)
    mv_ref[...] = proj(3 * A + 2 * M, M).astype(BF16)
    mo_ref[...] = proj(3 * A + 3 * M, M).astype(BF16)
    gates = jnp.dot(u, wg_ref[...], preferred_element_type=F32) + gb_ref[...]
    gt_ref[...] = gates.T[:2 * MLSTM_HEADS, :]


def _in_proj(x, g, w_main, w_gate, gate_b, cos_t, sin_t, tm=512):
    B, S, D = x.shape
    grid = (B, S // tm)
    tok = lambda width: pl.BlockSpec((None, tm, width), lambda b, i: (b, i, 0))
    attn_shape = jax.ShapeDtypeStruct((B, S, ATTN_WIDTH), BF16)
    return pl.pallas_call(
        functools.partial(_in_proj_kernel, tm=tm),
        name="in_proj",
        grid=grid,
        in_specs=[tok(D), _resident((1, D)), _resident(w_main.shape), _resident(w_gate.shape),
                  _resident((1, LANES)), _resident(cos_t.shape), _resident(sin_t.shape)],
        out_specs=[tok(ATTN_WIDTH), tok(ATTN_WIDTH), tok(ATTN_WIDTH),
                   tok(2 * MLSTM_WIDTH), tok(MLSTM_WIDTH), tok(MLSTM_WIDTH),
                   pl.BlockSpec((None, 2 * MLSTM_HEADS, tm), lambda b, i: (b, 0, i))],
        out_shape=[attn_shape, attn_shape, attn_shape,
                   jax.ShapeDtypeStruct((B, S, 2 * MLSTM_WIDTH), BF16),
                   jax.ShapeDtypeStruct((B, S, MLSTM_WIDTH), BF16),
                   jax.ShapeDtypeStruct((B, S, MLSTM_WIDTH), BF16),
                   jax.ShapeDtypeStruct((B, 2 * MLSTM_HEADS, S), F32)],
        compiler_params=pltpu.CompilerParams(
            dimension_semantics=("parallel", "parallel"), vmem_limit_bytes=VMEM_LIMIT),
    )(x, g, w_main, w_gate, gate_b, cos_t, sin_t)


def _attn_kernel(q_ref, k_ref, v_ref, o_ref, stage, q4, k4, v4, q16, k16, v16,
                 acc1, acc4, acc16, st1, st4, st16, bias_ref, *, seq):
    blk = ATTN_BLOCK
    qi = lax.broadcasted_iota(jnp.int32, (2 * blk, 2 * blk), 0) % blk
    ki = lax.broadcasted_iota(jnp.int32, (2 * blk, 2 * blk), 1)
    dist = qi + blk - ki
    valid = (dist >= 0) & (dist <= blk)
    bias_ref[1] = jnp.where(valid, 0.0, NEG_INF)
    bias_ref[0] = jnp.where(valid & (ki >= blk), 0.0, NEG_INF)

    head0 = lax.broadcasted_iota(jnp.int32, (blk, LANES), 1) < ATTN_HEAD_DIM
    ones_kv = jnp.ones((2 * blk, LANES), BF16)

    conv_rows = 512
    for x_ref, views in ((q_ref, (q4, q16)), (k_ref, (k4, k16)), (v_ref, (v4, v16))):
        def to_f32(c, carry, x_ref=x_ref):
            rows = pl.ds(pl.multiple_of(c * conv_rows, conv_rows), conv_rows)
            stage[rows, :] = x_ref[rows, :].astype(F32)
            return carry
        lax.fori_loop(0, seq // conv_rows, to_f32, 0)
        for view, d in zip(views, DILATIONS[1:]):
            rows_v = 64
            def gather(c, carry, view=view, d=d):
                l0 = pl.multiple_of(c * rows_v, rows_v)
                for r in range(d):
                    view[pl.ds(l0, rows_v), r * LANES:(r + 1) * LANES] = (
                        stage[pl.ds(l0 * d + r, rows_v, stride=d), :].astype(BF16))
                return carry
            lax.fori_loop(0, seq // d // rows_v, gather, 0)

    patterns = ((q_ref, k_ref, v_ref, acc1, st1, DILATIONS[0]),
                (q4, k4, v4, acc4, st4, DILATIONS[1]),
                (q16, k16, v16, acc16, st16, DILATIONS[2]))

    def token_rows(ref, row0, r, d):
        if d == 1:
            return ref.at[pl.ds(row0, blk), :]
        return ref.at[pl.ds(row0 * d + r, blk, stride=d), :]

    def view_lanes(r, d):
        return slice(0, LANES) if d == 1 else pl.ds(pl.multiple_of(r * LANES, LANES), LANES)

    for (qv, kv, vv_ref, acc_ref, st_ref, d) in patterns:
        nb = seq // d // blk

        def block(i, carry, qv=qv, kv=kv, vv_ref=vv_ref, acc_ref=acc_ref, st_ref=st_ref, d=d, nb=nb):
            r = i // nb
            n = i % nb
            row0 = pl.multiple_of(n * blk, blk)
            prow0 = pl.multiple_of(jnp.maximum(n - 1, 0) * blk, blk)
            lanes = view_lanes(r, d)
            q = qv[pl.ds(row0, blk), lanes]
            zero = jnp.zeros_like(q)
            q2 = jnp.concatenate([jnp.where(head0, q, zero), jnp.where(head0, zero, q)], axis=0)
            kk = jnp.concatenate([kv[pl.ds(prow0, blk), lanes], kv[pl.ds(row0, blk), lanes]], axis=0)
            vv = jnp.concatenate([vv_ref[pl.ds(prow0, blk), lanes], vv_ref[pl.ds(row0, blk), lanes]], axis=0)
            s = lax.dot_general(q2, kk, (((1,), (1,)), ((), ())), preferred_element_type=F32)
            s = s + bias_ref[jnp.minimum(n, 1)]
            m = jnp.max(s, axis=1, keepdims=True)
            p = jnp.exp(s - m).astype(BF16)
            res = jnp.dot(p, jnp.concatenate([vv, ones_kv], axis=1), preferred_element_type=F32)
            den = res[:, LANES:]
            o = res[:, :LANES] * (1.0 / den)
            acc_ref[pl.ds(row0, blk), lanes] = jnp.where(head0, o[:blk], o[blk:])
            lse = m + jnp.log(den)
            token_rows(st_ref, row0, r, d)[...] = jnp.where(head0, lse[:blk], lse[blk:])
            return carry

        lax.fori_loop(0, d * nb, block, 0, unroll=4)

    rows_b = 512

    def weights(c, carry):
        rows = pl.ds(pl.multiple_of(c * rows_b, rows_b), rows_b)
        a, b, e = st1[rows, :], st4[rows, :], st16[rows, :]
        top = jnp.maximum(jnp.maximum(a, b), e)
        ea, eb, ee = jnp.exp(a - top), jnp.exp(b - top), jnp.exp(e - top)
        inv = 1.0 / (ea + eb + ee)
        st1[rows, :] = ea * inv
        st4[rows, :] = eb * inv
        st16[rows, :] = ee * inv
        return carry

    lax.fori_loop(0, seq // rows_b, weights, 0)

    for (_, _, _, acc_ref, st_ref, d) in patterns:
        nb = seq // d // blk

        def scale(i, carry, acc_ref=acc_ref, st_ref=st_ref, d=d, nb=nb):
            r = i // nb
            n = i % nb
            row0 = pl.multiple_of(n * blk, blk)
            scaled = acc_ref[pl.ds(row0, blk), view_lanes(r, d)] * token_rows(st_ref, row0, r, d)[...]
            dst = token_rows(acc1, row0, r, d)
            dst[...] = scaled if d == 1 else dst[...] + scaled
            return carry

        lax.fori_loop(0, d * nb, scale, 0)

    def emit(c, carry):
        rows = pl.ds(pl.multiple_of(c * conv_rows, conv_rows), conv_rows)
        o_ref[rows, :] = acc1[rows, :].astype(BF16)
        return carry

    lax.fori_loop(0, seq // conv_rows, emit, 0)


def _dilated_attention(q, k, v):
    B, S, W = q.shape
    spec = pl.BlockSpec((None, S, LANES), lambda b, h: (b, 0, h))
    scratch = ([pltpu.VMEM((S, LANES), F32)]
               + [pltpu.VMEM((S // d, d * LANES), BF16) for d in DILATIONS[1:] for _ in range(3)]
               + [pltpu.VMEM((S // d, d * LANES), F32) for d in DILATIONS]
               + [pltpu.VMEM((S, LANES), F32) for _ in DILATIONS]
               + [pltpu.VMEM((2, 2 * ATTN_BLOCK, 2 * ATTN_BLOCK), F32)])
    return pl.pallas_call(
        functools.partial(_attn_kernel, seq=S),
        name="dilated_attention",
        grid=(B, W // LANES),
        in_specs=[spec, spec, spec],
        out_specs=spec,
        out_shape=jax.ShapeDtypeStruct((B, S, W), BF16),
        scratch_shapes=scratch,
        compiler_params=pltpu.CompilerParams(
            dimension_semantics=("parallel", "parallel"), vmem_limit_bytes=VMEM_LIMIT),
    )(q, k, v)


def _mlstm_kernel(q_ref, k_ref, v_ref, mo_ref, gt_ref, cwq_ref, cwk_ref, cbq_ref, cbk_ref, ng_ref,
                  o_ref, c_ref, m_ref, brow_ref, arow_ref, lf_ref, *, seq):
    L = MLSTM_CHUNK
    nc = seq // L
    dh = MLSTM_HEAD_DIM
    h_idx = pl.program_id(1)

    i_mat = gt_ref[h_idx]
    f_mat = gt_ref[MLSTM_HEADS + h_idx]
    logf = jnp.minimum(f_mat, 0.0) - jnp.log(1.0 + jnp.exp(-jnp.abs(f_mat)))
    upper = (lax.broadcasted_iota(jnp.int32, (L, L), 0) <= lax.broadcasted_iota(jnp.int32, (L, L), 1)).astype(F32)
    b_mat = jnp.dot(logf, upper, preferred_element_type=F32, precision=lax.Precision.HIGHEST)
    brow_ref[...] = b_mat
    arow_ref[...] = i_mat - b_mat
    lf_ref[...] = logf

    c_ref[...] = jnp.zeros_like(c_ref)
    m_ref[...] = jnp.zeros_like(m_ref)

    t_idx = lax.broadcasted_iota(jnp.int32, (L, L), 0)
    s_idx = lax.broadcasted_iota(jnp.int32, (L, L), 1)
    causal = s_idx <= t_idx
    ones_col = (lax.broadcasted_iota(jnp.int32, (L, dh), 1) == 0).astype(BF16)
    k_scale = 1.0 / math.sqrt(dh)

    def conv_silu(x_ref, w_ref, b_ref, c, r0):
        cur = x_ref[pl.ds(r0, L), :].astype(F32)
        h0 = pl.multiple_of(jnp.maximum(r0 - CONV_HALO, 0), CONV_HALO)
        halo = x_ref[pl.ds(h0, CONV_HALO), :].astype(F32) * (c > 0).astype(F32)
        ext = jnp.concatenate([halo, cur], axis=0)
        acc = b_ref[...] + w_ref[CONV_WIDTH - 1:CONV_WIDTH, :] * cur
        for j in range(1, CONV_WIDTH):
            acc = acc + w_ref[CONV_WIDTH - 1 - j:CONV_WIDTH - j, :] * pltpu.roll(ext, j, axis=0)[CONV_HALO:, :]
        return acc * jax.nn.sigmoid(acc)

    def chunk(c, carry):
        r0 = pl.multiple_of(c * L, L)
        q = conv_silu(q_ref, cwq_ref, cbq_ref, c, r0)
        k = conv_silu(k_ref, cwk_ref, cbk_ref, c, r0) * k_scale
        kT = k.T
        q_b = q.astype(BF16)
        v_aug = jnp.concatenate([v_ref[pl.ds(r0, L), :], ones_col], axis=1)

        b_row = brow_ref[pl.ds(c, 1), :]
        a_row = arow_ref[pl.ds(c, 1), :]
        lf_row = lf_ref[pl.ds(c, 1), :]
        m_prev = m_ref[0:1, 0:1]

        b_col = jnp.sum(jnp.where(causal, lf_row, 0.0), axis=1, keepdims=True)
        log_d = jnp.where(causal, b_col + a_row, NEG_INF)
        m_t = jnp.maximum(b_col + m_prev, jnp.max(log_d, axis=1, keepdims=True))
        w_intra = jnp.exp(log_d - m_t)
        w_inter = jnp.exp(b_col + m_prev - m_t)
        qk = jnp.dot(q_b, kT.astype(BF16), preferred_element_type=F32) * w_intra
        res = (w_inter * jnp.dot(q_b, c_ref[...].astype(BF16), preferred_element_type=F32)
               + jnp.dot(qk.astype(BF16), v_aug, preferred_element_type=F32))
        num = res[:, :dh]
        den = res[:, dh:dh + 1]
        h = num / jnp.maximum(jnp.abs(den), jnp.exp(-m_t))

        b_last = b_row[:, L - 1:L]
        log_s = b_last + a_row
        m_new = jnp.maximum(b_last + m_prev, jnp.max(log_s, axis=1, keepdims=True))
        decay = jnp.exp(b_last + m_prev - m_new)
        ws = jnp.exp(log_s - m_new)
        c_ref[...] = decay * c_ref[...] + jnp.dot((kT * ws).astype(BF16), v_aug, preferred_element_type=F32)
        m_ref[...] = jnp.broadcast_to(m_new, m_ref.shape)

        hn = _rms(h, ng_ref[...])
        o_ref[pl.ds(r0, L), :] = (jax.nn.sigmoid(mo_ref[pl.ds(r0, L), :].astype(F32)) * hn).astype(BF16)
        return carry

    lax.fori_loop(0, nc, chunk, 0)


def _mlstm(mqk, mv, mo, gates_t, conv_w, conv_b, norm_g):
    B, S, _ = mv.shape
    H, dh, L = MLSTM_HEADS, MLSTM_HEAD_DIM, MLSTM_CHUNK
    nc = S // L
    gt = gates_t.reshape(B, 2 * H, nc, L)
    col = lambda off: pl.BlockSpec((None, S, dh), lambda b, h: (b, 0, off + h))
    vec = lambda rows, off: pl.BlockSpec((rows, dh), lambda b, h: (0, off + h))
    return pl.pallas_call(
        functools.partial(_mlstm_kernel, seq=S),
        name="mlstm",
        grid=(B, H),
        in_specs=[col(0), col(H), col(0), col(0),
                  pl.BlockSpec((None, 2 * H, nc, L), lambda b, h: (b, 0, 0, 0)),
                  vec(CONV_WIDTH, 0), vec(CONV_WIDTH, H), vec(1, 0), vec(1, H), vec(1, 0)],
        out_specs=col(0),
        out_shape=jax.ShapeDtypeStruct((B, S, H * dh), BF16),
        scratch_shapes=[pltpu.VMEM((dh, 2 * dh), F32), pltpu.VMEM((8, LANES), F32),
                        pltpu.VMEM((nc, L), F32), pltpu.VMEM((nc, L), F32), pltpu.VMEM((nc, L), F32)],
        compiler_params=pltpu.CompilerParams(
            dimension_semantics=("parallel", "parallel"), vmem_limit_bytes=VMEM_LIMIT),
    )(mqk, mqk, mv, mo, gt, conv_w, conv_w, conv_b, conv_b, norm_g)


def _tail_kernel(x_ref, at_ref, ml_ref, p_ref, wo_ref, g_mlp_ref, wu_ref, wd_ref,
                 g_ple_ref, wpg_ref, wple_ref, g_fin_ref, o_ref, *, ff_chunk, final_norm):
    mix = jnp.concatenate([at_ref[...], ml_ref[...]], axis=1)
    h = x_ref[...] + jnp.dot(mix, wo_ref[...], preferred_element_type=F32)

    u = _rms(h, g_mlp_ref[...]).astype(BF16)
    for c0 in range(0, D_FF, ff_chunk):
        a = jnp.dot(u, wu_ref[:, c0:c0 + ff_chunk], preferred_element_type=F32)
        a = jnp.square(jnp.maximum(a, 0.0)).astype(BF16)
        h = h + jnp.dot(a, wd_ref[c0:c0 + ff_chunk, :], preferred_element_type=F32)

    gate = jax.nn.sigmoid(jnp.dot(_rms(h, g_ple_ref[...]).astype(BF16), wpg_ref[...], preferred_element_type=F32))
    h = h + gate * jnp.dot(p_ref[...].astype(BF16), wple_ref[...], preferred_element_type=F32)
    o_ref[...] = _rms(h, g_fin_ref[...]) if final_norm else h


def _tail(x, attn, ml, p, w_out, g_mlp, w_up, w_down, g_ple, w_pg, w_ple, g_fin, final_norm,
          tm=512, ff_chunk=1024):
    B, S, D = x.shape
    tok = lambda width: pl.BlockSpec((None, tm, width), lambda b, i: (b, i, 0))
    return pl.pallas_call(
        functools.partial(_tail_kernel, ff_chunk=ff_chunk, final_norm=final_norm),
        name="tail",
        grid=(B, S // tm),
        in_specs=[tok(D), tok(ATTN_WIDTH), tok(MLSTM_WIDTH), tok(PLE_DIM),
                  _resident(w_out.shape), _resident((1, D)), _resident(w_up.shape), _resident(w_down.shape),
                  _resident((1, D)), _resident(w_pg.shape), _resident(w_ple.shape), _resident((1, D))],
        out_specs=tok(D),
        out_shape=jax.ShapeDtypeStruct((B, S, D), F32),
        compiler_params=pltpu.CompilerParams(
            dimension_semantics=("parallel", "parallel"), vmem_limit_bytes=VMEM_LIMIT),
    )(x, attn, ml, p, w_out, g_mlp, w_up, w_down, g_ple, w_pg, w_ple, g_fin)


def _rope_tables(seq):
    half = ROPE_DIM // 2
    inv_freq = jnp.power(ROPE_THETA, -jnp.arange(half, dtype=F32) / half)
    ang = jnp.arange(seq, dtype=jnp.int32).astype(F32)[:, None] * inv_freq[None, :]
    cos, sin = jnp.cos(ang), jnp.sin(ang)
    rest = ATTN_HEAD_DIM - ROPE_DIM
    cos_head = jnp.concatenate([cos, cos, jnp.ones((seq, rest), F32)], axis=1)
    sin_head = jnp.concatenate([-sin, sin, jnp.zeros((seq, rest), F32)], axis=1)
    reps = LANES // ATTN_HEAD_DIM
    return jnp.tile(cos_head, (1, reps)), jnp.tile(sin_head, (1, reps))


def kernel(x, p, norm_mix_g, w_in, conv_w, conv_b, gate_b, mlstm_norm_g, w_out, norm_mlp_g, w_up, w_down,
           norm_ple_g, w_ple_gate, w_ple, final_norm_g):
    B, S, D = x.shape
    depth = w_in.shape[0]
    cos_t, sin_t = _rope_tables(S)
    row = lambda v: v.reshape(1, -1).astype(F32)
    n_gate = 2 * MLSTM_HEADS
    h = x
    for layer in range(depth):
        w = w_in[layer]
        w_main = jnp.concatenate([w[:, :ATTN_WIDTH] * (1.0 / math.sqrt(ATTN_HEAD_DIM)),
                                  w[:, ATTN_WIDTH:MAIN_WIDTH]], axis=1).astype(BF16)
        w_gate = jnp.pad(w[:, MAIN_WIDTH:], ((0, 0), (0, LANES - n_gate))).astype(BF16)
        gb = jnp.pad(gate_b[layer].astype(F32), (0, LANES - n_gate)).reshape(1, LANES)
        q, k, v, mqk, mv, mo, gates_t = _in_proj(h, row(norm_mix_g[layer]), w_main, w_gate, gb, cos_t, sin_t)
        attn = _dilated_attention(q, k, v)
        ml = _mlstm(mqk, mv, mo, gates_t, conv_w[layer].astype(F32), row(conv_b[layer]), row(mlstm_norm_g[layer]))
        h = _tail(h, attn, ml, p[layer], w_out[layer].astype(BF16), row(norm_mlp_g[layer]),
                  w_up[layer].astype(BF16), w_down[layer].astype(BF16), row(norm_ple_g[layer]),
                  w_ple_gate[layer].astype(BF16), w_ple[layer].astype(BF16), row(final_norm_g),
                  final_norm=(layer == depth - 1))
    return h
```

```python
import functools
import math

import jax
import jax.numpy as jnp
from jax import lax
from jax.experimental import pallas as pl
from jax.experimental.pallas import tpu as pltpu

D_MODEL = 1024
ATTN_HEADS = 8
ATTN_HEAD_DIM = 64
ATTN_WIDTH = ATTN_HEADS * ATTN_HEAD_DIM
MLSTM_HEADS = 4
MLSTM_HEAD_DIM = 128
MLSTM_WIDTH = MLSTM_HEADS * MLSTM_HEAD_DIM
DILATIONS = (1, 4, 16)
ATTN_BLOCK = 128
ROPE_THETA = 500000.0
ROPE_DIM = ATTN_HEAD_DIM // 4
CONV_WIDTH = 4
D_FF = 4 * D_MODEL
PLE_DIM = 256
RMS_EPS = 1e-6

LANES = 128
HEAD_PAIRS = ATTN_WIDTH // LANES
MAIN_WIDTH = 3 * ATTN_WIDTH + 4 * MLSTM_WIDTH
MLSTM_CHUNK = 256
CONV_HALO = 8
VMEM_LIMIT = 56 * 1024 * 1024

F32 = jnp.float32
BF16 = jnp.bfloat16
NEG_INF = float("-inf")
LOG2E = 1.4426950408889634


def _rms(x, g):
    return x * lax.rsqrt(jnp.mean(x * x, axis=-1, keepdims=True) + RMS_EPS) * g


def _resident(shape):
    return pl.BlockSpec(shape, lambda *_: (0,) * len(shape), pipeline_mode=pl.Buffered(1))


def _in_proj_kernel(x_ref, g_ref, w_ref, wg_ref, gb_ref, cos_ref, sin_ref, cw_ref, cb_ref,
                    q_ref, k_ref, v_ref, mqk_ref, mv_ref, mo_ref, gt_ref, tail_ref, *, tm):
    @pl.when(pl.program_id(1) == 0)
    def _():
        tail_ref[:CONV_HALO, :] = jnp.zeros((CONV_HALO, tail_ref.shape[1]), F32)

    u = _rms(x_ref[...], g_ref[...]).astype(BF16)
    pos0 = pl.multiple_of(pl.program_id(1) * tm, tm)
    cos = cos_ref[pl.ds(pos0, tm), :]
    sin = sin_ref[pl.ds(pos0, tm), :]
    lane = lax.broadcasted_iota(jnp.int32, (tm, LANES), 1)
    first_half = (lane % ATTN_HEAD_DIM) < (ROPE_DIM // 2)

    def proj(c0, width):
        return jnp.dot(u, w_ref[:, c0:c0 + width], preferred_element_type=F32)

    def rope_store(y, o_ref):
        for j in range(HEAD_PAIRS):
            yj = y[:, j * LANES:(j + 1) * LANES]
            partner = jnp.where(first_half,
                                pltpu.roll(yj, LANES - ROPE_DIM // 2, axis=1),
                                pltpu.roll(yj, ROPE_DIM // 2, axis=1))
            o_ref[:, j * LANES:(j + 1) * LANES] = (yj * cos + partner * sin).astype(BF16)

    A, M = ATTN_WIDTH, MLSTM_WIDTH
    tail_ref[CONV_HALO:, :] = proj(3 * A, 2 * M)
    k_scale = 1.0 / math.sqrt(MLSTM_HEAD_DIM)
    rb = 128
    for c0 in range(0, 2 * M, LANES):
        cols = slice(c0, c0 + LANES)
        for r0 in range(0, tm, rb):
            acc = cb_ref[:, cols]
            for j in range(CONV_WIDTH):
                start = CONV_HALO + r0 - j
                acc = acc + cw_ref[CONV_WIDTH - 1 - j:CONV_WIDTH - j, cols] * tail_ref[start:start + rb, cols]
            act = acc * jax.nn.sigmoid(acc)
            mqk_ref[r0:r0 + rb, cols] = (act if c0 < M else act * k_scale).astype(BF16)
    tail_ref[:CONV_HALO, :] = tail_ref[tm:tm + CONV_HALO, :]
    rope_store(proj(0, A), q_ref)
    rope_store(proj(A, A), k_ref)
    v_ref[...] = proj(2 * A, A).astype(BF16)
    mv_ref[...] = proj(3 * A + 2 * M, M).astype(BF16)
    mo_ref[...] = proj(3 * A + 3 * M, M).astype(BF16)
    gates = jnp.dot(u, wg_ref[...], preferred_element_type=F32) + gb_ref[...]
    gt_ref[...] = gates.T[:2 * MLSTM_HEADS, :]


def _in_proj(x, g, w_main, w_gate, gate_b, cos_t, sin_t, conv_w, conv_b, tm=512):
    B, S, D = x.shape
    grid = (B, S // tm)
    tok = lambda width: pl.BlockSpec((None, tm, width), lambda b, i: (b, i, 0))
    attn_shape = jax.ShapeDtypeStruct((B, S, ATTN_WIDTH), BF16)
    return pl.pallas_call(
        functools.partial(_in_proj_kernel, tm=tm),
        name="in_proj",
        grid=grid,
        in_specs=[tok(D), _resident((1, D)), _resident(w_main.shape), _resident(w_gate.shape),
                  _resident((1, LANES)), _resident(cos_t.shape), _resident(sin_t.shape),
                  _resident(conv_w.shape), _resident(conv_b.shape)],
        out_specs=[tok(ATTN_WIDTH), tok(ATTN_WIDTH), tok(ATTN_WIDTH),
                   tok(2 * MLSTM_WIDTH), tok(MLSTM_WIDTH), tok(MLSTM_WIDTH),
                   pl.BlockSpec((None, 2 * MLSTM_HEADS, tm), lambda b, i: (b, 0, i))],
        out_shape=[attn_shape, attn_shape, attn_shape,
                   jax.ShapeDtypeStruct((B, S, 2 * MLSTM_WIDTH), BF16),
                   jax.ShapeDtypeStruct((B, S, MLSTM_WIDTH), BF16),
                   jax.ShapeDtypeStruct((B, S, MLSTM_WIDTH), BF16),
                   jax.ShapeDtypeStruct((B, 2 * MLSTM_HEADS, S), F32)],
        scratch_shapes=[pltpu.VMEM((CONV_HALO + tm, 2 * MLSTM_WIDTH), F32)],
        compiler_params=pltpu.CompilerParams(
            dimension_semantics=("parallel", "arbitrary"), vmem_limit_bytes=VMEM_LIMIT),
    )(x, g, w_main, w_gate, gate_b, cos_t, sin_t, conv_w, conv_b)


def _attn_kernel(q_ref, k_ref, v_ref, o_ref, stage, q4, k4, v4, q16, k16, v16,
                 acc1, acc4, acc16, st1, st4, st16, bias_ref, *, seq):
    blk = ATTN_BLOCK
    qi = lax.broadcasted_iota(jnp.int32, (2 * blk, 2 * blk), 0) % blk
    ki = lax.broadcasted_iota(jnp.int32, (2 * blk, 2 * blk), 1)
    dist = qi + blk - ki
    valid = (dist >= 0) & (dist <= blk)
    bias_ref[1] = jnp.where(valid, 0.0, NEG_INF)
    bias_ref[0] = jnp.where(valid & (ki >= blk), 0.0, NEG_INF)

    head0 = lax.broadcasted_iota(jnp.int32, (blk, LANES), 1) < ATTN_HEAD_DIM
    ones_kv = jnp.ones((2 * blk, LANES), BF16)

    conv_rows = 512
    for x_ref, views in ((q_ref, (q4, q16)), (k_ref, (k4, k16)), (v_ref, (v4, v16))):
        def to_f32(c, carry, x_ref=x_ref):
            rows = pl.ds(pl.multiple_of(c * conv_rows, conv_rows), conv_rows)
            stage[rows, :] = x_ref[rows, :].astype(F32)
            return carry
        lax.fori_loop(0, seq // conv_rows, to_f32, 0)
        for view, d in zip(views, DILATIONS[1:]):
            rows_v = 64
            def gather(c, carry, view=view, d=d):
                l0 = pl.multiple_of(c * rows_v, rows_v)
                for r in range(d):
                    view[pl.ds(l0, rows_v), r * LANES:(r + 1) * LANES] = (
                        stage[pl.ds(l0 * d + r, rows_v, stride=d), :].astype(BF16))
                return carry
            lax.fori_loop(0, seq // d // rows_v, gather, 0)

    patterns = ((q_ref, k_ref, v_ref, acc1, st1, DILATIONS[0]),
                (q4, k4, v4, acc4, st4, DILATIONS[1]),
                (q16, k16, v16, acc16, st16, DILATIONS[2]))

    def token_rows(ref, row0, r, d):
        if d == 1:
            return ref.at[pl.ds(row0, blk), :]
        return ref.at[pl.ds(row0 * d + r, blk, stride=d), :]

    def view_lanes(r, d):
        return slice(0, LANES) if d == 1 else pl.ds(pl.multiple_of(r * LANES, LANES), LANES)

    for (qv, kv, vv_ref, acc_ref, st_ref, d) in patterns:
        nb = seq // d // blk

        def block(i, carry, qv=qv, kv=kv, vv_ref=vv_ref, acc_ref=acc_ref, st_ref=st_ref, d=d, nb=nb):
            r = i // nb
            n = i % nb
            row0 = pl.multiple_of(n * blk, blk)
            prow0 = pl.multiple_of(jnp.maximum(n - 1, 0) * blk, blk)
            lanes = view_lanes(r, d)
            q = qv[pl.ds(row0, blk), lanes]
            zero = jnp.zeros_like(q)
            q2 = jnp.concatenate([jnp.where(head0, q, zero), jnp.where(head0, zero, q)], axis=0)
            kk = jnp.concatenate([kv[pl.ds(prow0, blk), lanes], kv[pl.ds(row0, blk), lanes]], axis=0)
            vv = jnp.concatenate([vv_ref[pl.ds(prow0, blk), lanes], vv_ref[pl.ds(row0, blk), lanes]], axis=0)
            s = lax.dot_general(q2, kk, (((1,), (1,)), ((), ())), preferred_element_type=F32)
            s = s + bias_ref[jnp.minimum(n, 1)]
            m = jnp.max(s, axis=1, keepdims=True)
            p = jnp.exp(s - m).astype(BF16)
            res = jnp.dot(p, jnp.concatenate([vv, ones_kv], axis=1), preferred_element_type=F32)
            den = res[:, LANES:]
            o = res[:, :LANES] * (1.0 / den)
            acc_ref[pl.ds(row0, blk), lanes] = jnp.where(head0, o[:blk], o[blk:])
            lse = m + jnp.log(den)
            token_rows(st_ref, row0, r, d)[...] = jnp.where(head0, lse[:blk], lse[blk:])
            return carry

        lax.fori_loop(0, d * nb, block, 0, unroll=4)

    rows_b = 512

    def weights(c, carry):
        rows = pl.ds(pl.multiple_of(c * rows_b, rows_b), rows_b)
        a, b, e = st1[rows, :], st4[rows, :], st16[rows, :]
        top = jnp.maximum(jnp.maximum(a, b), e)
        ea, eb, ee = jnp.exp(a - top), jnp.exp(b - top), jnp.exp(e - top)
        inv = 1.0 / (ea + eb + ee)
        st1[rows, :] = ea * inv
        st4[rows, :] = eb * inv
        st16[rows, :] = ee * inv
        return carry

    lax.fori_loop(0, seq // rows_b, weights, 0)

    for (_, _, _, acc_ref, st_ref, d) in patterns:
        nb = seq // d // blk

        def scale(i, carry, acc_ref=acc_ref, st_ref=st_ref, d=d, nb=nb):
            r = i // nb
            n = i % nb
            row0 = pl.multiple_of(n * blk, blk)
            scaled = acc_ref[pl.ds(row0, blk), view_lanes(r, d)] * token_rows(st_ref, row0, r, d)[...]
            dst = token_rows(acc1, row0, r, d)
            dst[...] = scaled if d == 1 else dst[...] + scaled
            return carry

        lax.fori_loop(0, d * nb, scale, 0)

    def emit(c, carry):
        rows = pl.ds(pl.multiple_of(c * conv_rows, conv_rows), conv_rows)
        o_ref[rows, :] = acc1[rows, :].astype(BF16)
        return carry

    lax.fori_loop(0, seq // conv_rows, emit, 0)


def _dilated_attention(q, k, v):
    B, S, W = q.shape
    spec = pl.BlockSpec((None, S, LANES), lambda b, h: (b, 0, h))
    scratch = ([pltpu.VMEM((S, LANES), F32)]
               + [pltpu.VMEM((S // d, d * LANES), BF16) for d in DILATIONS[1:] for _ in range(3)]
               + [pltpu.VMEM((S // d, d * LANES), F32) for d in DILATIONS]
               + [pltpu.VMEM((S, LANES), F32) for _ in DILATIONS]
               + [pltpu.VMEM((2, 2 * ATTN_BLOCK, 2 * ATTN_BLOCK), F32)])
    return pl.pallas_call(
        functools.partial(_attn_kernel, seq=S),
        name="dilated_attention",
        grid=(B, W // LANES),
        in_specs=[spec, spec, spec],
        out_specs=spec,
        out_shape=jax.ShapeDtypeStruct((B, S, W), BF16),
        scratch_shapes=scratch,
        compiler_params=pltpu.CompilerParams(
            dimension_semantics=("parallel", "parallel"), vmem_limit_bytes=VMEM_LIMIT),
    )(q, k, v)


def _mlstm_kernel(qk_ref, v_ref, mo_ref, gt_ref, ng_ref, o_ref, c_ref, m_ref, a_ref, b_ref, col_ref, *, rows):
    L, H, dh = MLSTM_CHUNK, MLSTM_HEADS, MLSTM_HEAD_DIM
    nc = rows // L

    @pl.when(pl.program_id(1) == 0)
    def _():
        c_ref[...] = jnp.zeros_like(c_ref)
        m_ref[...] = jnp.zeros_like(m_ref)

    gates = gt_ref[...]
    f_pre = gates[H:]
    b = (jnp.minimum(f_pre, 0.0) - jnp.log(1.0 + jnp.exp(-jnp.abs(f_pre)))) * LOG2E
    pos = lax.broadcasted_iota(jnp.int32, (H, rows), 1) % L
    for step in [1 << e for e in range(L.bit_length() - 1)]:
        b = b + jnp.where(pos >= step, pltpu.roll(b, step, axis=1), 0.0)
    a = gates[:H] * LOG2E - b
    amax = a
    for step in [1 << e for e in range(L.bit_length() - 1)]:
        amax = jnp.maximum(amax, jnp.where(pos >= step, pltpu.roll(amax, step, axis=1), NEG_INF))
    a_ref[...] = a
    b_ref[...] = b
    col_ref[...] = jnp.concatenate([b, amax, a, jnp.zeros((LANES - 3 * H, rows), F32)], axis=0).T

    causal = lax.broadcasted_iota(jnp.int32, (L, L), 1) <= lax.broadcasted_iota(jnp.int32, (L, L), 0)
    ones_blk = jnp.ones((L, dh), BF16)
    wide = lambda x: jnp.concatenate([x, x], axis=1)

    def head_chunk(h, c, r0):
        cols = slice(h * dh, (h + 1) * dh)
        q_b = qk_ref[pl.ds(r0, L), cols]
        k_b = qk_ref[pl.ds(r0, L), MLSTM_WIDTH + h * dh:MLSTM_WIDTH + (h + 1) * dh]
        v_b = v_ref[pl.ds(r0, L), cols]

        column = lambda j: jnp.broadcast_to(col_ref[pl.ds(r0, L), j:j + 1], (L, dh))
        b_rep, amax_rep, a_rep = column(h), column(H + h), column(2 * H + h)
        a_row = a_ref[h:h + 1, pl.ds(r0, L)]
        b_last = b_ref[h:h + 1, pl.ds(r0, L)][:, L - 1:L]
        m_prev = m_ref[h, 0:1, :]
        mx = jnp.maximum(m_prev, amax_rep)

        w_intra = jnp.exp2(jnp.where(causal, a_row - wide(mx), NEG_INF))
        qk = lax.dot_general(q_b, k_b, (((1,), (1,)), ((), ())), preferred_element_type=F32) * w_intra
        res = (wide(jnp.exp2(m_prev - mx)) * jnp.dot(q_b, c_ref[h].astype(BF16), preferred_element_type=F32)
               + jnp.dot(qk.astype(BF16), jnp.concatenate([v_b, ones_blk], axis=1), preferred_element_type=F32))
        hid = res[:, :dh] / jnp.maximum(jnp.abs(res[:, dh:]), jnp.exp2(-(b_rep + mx)))

        m_new = b_last + jnp.maximum(m_prev, jnp.max(a_row, axis=1, keepdims=True))
        decay = jnp.exp2(b_last + m_prev - m_new)
        ws = jnp.exp2(b_last + a_rep - m_new)
        wv = jnp.concatenate([(v_b.astype(F32) * ws).astype(BF16), ws.astype(BF16)], axis=1)
        c_ref[h] = wide(decay) * c_ref[h] + lax.dot_general(k_b, wv, (((0,), (0,)), ((), ())),
                                                            preferred_element_type=F32)
        m_ref[h] = jnp.broadcast_to(m_new, m_ref.shape[1:])

        hn = _rms(hid, ng_ref[:, cols])
        o_ref[pl.ds(r0, L), cols] = (jax.nn.sigmoid(mo_ref[pl.ds(r0, L), cols].astype(F32)) * hn).astype(BF16)

    def chunk(c, carry):
        r0 = pl.multiple_of(c * L, L)
        for h in range(H):
            head_chunk(h, c, r0)
        return carry

    lax.fori_loop(0, nc, chunk, 0)


def _mlstm(mqk, mv, mo, gates_t, norm_g, parts=2):
    B, S, W = mv.shape
    H, dh = MLSTM_HEADS, MLSTM_HEAD_DIM
    rows = S // parts
    tok = lambda width: pl.BlockSpec((None, rows, width), lambda b, i: (b, i, 0))
    return pl.pallas_call(
        functools.partial(_mlstm_kernel, rows=rows),
        name="mlstm",
        grid=(B, parts),
        in_specs=[tok(2 * W), tok(W), tok(W),
                  pl.BlockSpec((None, 2 * H, rows), lambda b, i: (b, 0, i)),
                  pl.BlockSpec(norm_g.shape, lambda b, i: (0, 0))],
        out_specs=tok(W),
        out_shape=jax.ShapeDtypeStruct((B, S, W), BF16),
        scratch_shapes=[pltpu.VMEM((H, dh, 2 * dh), F32), pltpu.VMEM((H, 8, LANES), F32),
                        pltpu.VMEM((H, rows), F32), pltpu.VMEM((H, rows), F32), pltpu.VMEM((rows, LANES), F32)],
        compiler_params=pltpu.CompilerParams(
            dimension_semantics=("parallel", "arbitrary"), vmem_limit_bytes=VMEM_LIMIT),
    )(mqk, mv, mo, gates_t, norm_g)


def _tail_kernel(x_ref, at_ref, ml_ref, p_ref, wo_ref, g_mlp_ref, wu_ref, wd_ref,
                 g_ple_ref, wpg_ref, wple_ref, g_fin_ref, o_ref, *, ff_chunk, final_norm):
    mix = jnp.concatenate([at_ref[...], ml_ref[...]], axis=1)
    h = x_ref[...] + jnp.dot(mix, wo_ref[...], preferred_element_type=F32)

    u = _rms(h, g_mlp_ref[...]).astype(BF16)
    for c0 in range(0, D_FF, ff_chunk):
        a = jnp.dot(u, wu_ref[:, c0:c0 + ff_chunk], preferred_element_type=F32)
        a = jnp.square(jnp.maximum(a, 0.0)).astype(BF16)
        h = h + jnp.dot(a, wd_ref[c0:c0 + ff_chunk, :], preferred_element_type=F32)

    gate = jax.nn.sigmoid(jnp.dot(_rms(h, g_ple_ref[...]).astype(BF16), wpg_ref[...], preferred_element_type=F32))
    h = h + gate * jnp.dot(p_ref[...].astype(BF16), wple_ref[...], preferred_element_type=F32)
    o_ref[...] = _rms(h, g_fin_ref[...]) if final_norm else h


def _tail(x, attn, ml, p, w_out, g_mlp, w_up, w_down, g_ple, w_pg, w_ple, g_fin, final_norm,
          tm=512, ff_chunk=1024):
    B, S, D = x.shape
    tok = lambda width: pl.BlockSpec((None, tm, width), lambda b, i: (b, i, 0))
    return pl.pallas_call(
        functools.partial(_tail_kernel, ff_chunk=ff_chunk, final_norm=final_norm),
        name="tail",
        grid=(B, S // tm),
        in_specs=[tok(D), tok(ATTN_WIDTH), tok(MLSTM_WIDTH), tok(PLE_DIM),
                  _resident(w_out.shape), _resident((1, D)), _resident(w_up.shape), _resident(w_down.shape),
                  _resident((1, D)), _resident(w_pg.shape), _resident(w_ple.shape), _resident((1, D))],
        out_specs=tok(D),
        out_shape=jax.ShapeDtypeStruct((B, S, D), F32),
        compiler_params=pltpu.CompilerParams(
            dimension_semantics=("parallel", "parallel"), vmem_limit_bytes=VMEM_LIMIT),
    )(x, attn, ml, p, w_out, g_mlp, w_up, w_down, g_ple, w_pg, w_ple, g_fin)


def _rope_tables(seq):
    half = ROPE_DIM // 2
    inv_freq = jnp.power(ROPE_THETA, -jnp.arange(half, dtype=F32) / half)
    ang = jnp.arange(seq, dtype=jnp.int32).astype(F32)[:, None] * inv_freq[None, :]
    cos, sin = jnp.cos(ang), jnp.sin(ang)
    rest = ATTN_HEAD_DIM - ROPE_DIM
    cos_head = jnp.concatenate([cos, cos, jnp.ones((seq, rest), F32)], axis=1)
    sin_head = jnp.concatenate([-sin, sin, jnp.zeros((seq, rest), F32)], axis=1)
    reps = LANES // ATTN_HEAD_DIM
    return jnp.tile(cos_head, (1, reps)), jnp.tile(sin_head, (1, reps))


def kernel(x, p, norm_mix_g, w_in, conv_w, conv_b, gate_b, mlstm_norm_g, w_out, norm_mlp_g, w_up, w_down,
           norm_ple_g, w_ple_gate, w_ple, final_norm_g):
    B, S, D = x.shape
    depth = w_in.shape[0]
    cos_t, sin_t = _rope_tables(S)
    row = lambda v: v.reshape(1, -1).astype(F32)
    n_gate = 2 * MLSTM_HEADS
    h = x
    for layer in range(depth):
        w = w_in[layer]
        w_main = jnp.concatenate([w[:, :ATTN_WIDTH] * (1.0 / math.sqrt(ATTN_HEAD_DIM)),
                                  w[:, ATTN_WIDTH:MAIN_WIDTH]], axis=1).astype(BF16)
        w_gate = jnp.pad(w[:, MAIN_WIDTH:], ((0, 0), (0, LANES - n_gate))).astype(BF16)
        gb = jnp.pad(gate_b[layer].astype(F32), (0, LANES - n_gate)).reshape(1, LANES)
        q, k, v, mqk, mv, mo, gates_t = _in_proj(h, row(norm_mix_g[layer]), w_main, w_gate, gb, cos_t, sin_t,
                                                 conv_w[layer].astype(F32), row(conv_b[layer]))
        attn = _dilated_attention(q, k, v)
        ml = _mlstm(mqk, mv, mo, gates_t, row(mlstm_norm_g[layer]))
        h = _tail(h, attn, ml, p[layer], w_out[layer].astype(BF16), row(norm_mlp_g[layer]),
                  w_up[layer].astype(BF16), w_down[layer].astype(BF16), row(norm_ple_g[layer]),
                  w_ple_gate[layer].astype(BF16), w_ple[layer].astype(BF16), row(final_norm_g),
                  final_norm=(layer == depth - 1))
    return h
```

```python
import functools
import math

import jax
import jax.numpy as jnp
from jax import lax
from jax.experimental import pallas as pl
from jax.experimental.pallas import tpu as pltpu

D_MODEL = 1024
ATTN_HEADS = 8
ATTN_HEAD_DIM = 64
ATTN_WIDTH = ATTN_HEADS * ATTN_HEAD_DIM
MLSTM_HEADS = 4
MLSTM_HEAD_DIM = 128
MLSTM_WIDTH = MLSTM_HEADS * MLSTM_HEAD_DIM
DILATIONS = (1, 4, 16)
ATTN_BLOCK = 128
ATTN_UNROLL = 8
ROPE_THETA = 500000.0
ROPE_DIM = ATTN_HEAD_DIM // 4
CONV_WIDTH = 4
D_FF = 4 * D_MODEL
PLE_DIM = 256
RMS_EPS = 1e-6

LANES = 128
HEAD_PAIRS = ATTN_WIDTH // LANES
MAIN_WIDTH = 3 * ATTN_WIDTH + 4 * MLSTM_WIDTH
MLSTM_CHUNK = 256
CONV_HALO = 8
VMEM_LIMIT = 56 * 1024 * 1024

F32 = jnp.float32
BF16 = jnp.bfloat16
NEG_INF = float("-inf")
LOG2E = 1.4426950408889634


def _rms(x, g):
    return x * lax.rsqrt(jnp.mean(x * x, axis=-1, keepdims=True) + RMS_EPS) * g


def _resident(shape):
    return pl.BlockSpec(shape, lambda *_: (0,) * len(shape), pipeline_mode=pl.Buffered(1))


def _in_proj_kernel(x_ref, g_ref, w_ref, wg_ref, gb_ref, cos_ref, sin_ref, cw_ref, cb_ref,
                    q_ref, k_ref, v_ref, mqk_ref, mv_ref, mo_ref, gt_ref, tail_ref, *, tm):
    @pl.when(pl.program_id(1) == 0)
    def _():
        tail_ref[:CONV_HALO, :] = jnp.zeros((CONV_HALO, tail_ref.shape[1]), F32)

    u = _rms(x_ref[...], g_ref[...]).astype(BF16)
    pos0 = pl.multiple_of(pl.program_id(1) * tm, tm)
    cos = cos_ref[pl.ds(pos0, tm), :]
    sin = sin_ref[pl.ds(pos0, tm), :]
    lane = lax.broadcasted_iota(jnp.int32, (tm, LANES), 1)
    first_half = (lane % ATTN_HEAD_DIM) < (ROPE_DIM // 2)

    def proj(c0, width):
        return jnp.dot(u, w_ref[:, c0:c0 + width], preferred_element_type=F32)

    def rope_store(y, o_ref):
        for j in range(HEAD_PAIRS):
            yj = y[:, j * LANES:(j + 1) * LANES]
            partner = jnp.where(first_half,
                                pltpu.roll(yj, LANES - ROPE_DIM // 2, axis=1),
                                pltpu.roll(yj, ROPE_DIM // 2, axis=1))
            o_ref[:, j * LANES:(j + 1) * LANES] = (yj * cos + partner * sin).astype(BF16)

    A, M = ATTN_WIDTH, MLSTM_WIDTH
    tail_ref[CONV_HALO:, :] = proj(3 * A, 2 * M)
    k_scale = 1.0 / math.sqrt(MLSTM_HEAD_DIM)
    rb = 128
    for c0 in range(0, 2 * M, LANES):
        cols = slice(c0, c0 + LANES)
        for r0 in range(0, tm, rb):
            acc = cb_ref[:, cols]
            for j in range(CONV_WIDTH):
                start = CONV_HALO + r0 - j
                acc = acc + cw_ref[CONV_WIDTH - 1 - j:CONV_WIDTH - j, cols] * tail_ref[start:start + rb, cols]
            act = acc * jax.nn.sigmoid(acc)
            mqk_ref[r0:r0 + rb, cols] = (act if c0 < M else act * k_scale).astype(BF16)
    tail_ref[:CONV_HALO, :] = tail_ref[tm:tm + CONV_HALO, :]
    rope_store(proj(0, A), q_ref)
    rope_store(proj(A, A), k_ref)
    v_ref[...] = proj(2 * A, A).astype(BF16)
    mv_ref[...] = proj(3 * A + 2 * M, M).astype(BF16)
    mo_ref[...] = proj(3 * A + 3 * M, M).astype(BF16)
    gates = jnp.dot(u, wg_ref[...], preferred_element_type=F32) + gb_ref[...]
    gt_ref[...] = gates.T[:2 * MLSTM_HEADS, :]


def _in_proj(x, g, w_main, w_gate, gate_b, cos_t, sin_t, conv_w, conv_b, tm=512):
    B, S, D = x.shape
    grid = (B, S // tm)
    tok = lambda width: pl.BlockSpec((None, tm, width), lambda b, i: (b, i, 0))
    attn_shape = jax.ShapeDtypeStruct((B, S, ATTN_WIDTH), BF16)
    return pl.pallas_call(
        functools.partial(_in_proj_kernel, tm=tm),
        name="in_proj",
        grid=grid,
        in_specs=[tok(D), _resident((1, D)), _resident(w_main.shape), _resident(w_gate.shape),
                  _resident((1, LANES)), _resident(cos_t.shape), _resident(sin_t.shape),
                  _resident(conv_w.shape), _resident(conv_b.shape)],
        out_specs=[tok(ATTN_WIDTH), tok(ATTN_WIDTH), tok(ATTN_WIDTH),
                   tok(2 * MLSTM_WIDTH), tok(MLSTM_WIDTH), tok(MLSTM_WIDTH),
                   pl.BlockSpec((None, 2 * MLSTM_HEADS, tm), lambda b, i: (b, 0, i))],
        out_shape=[attn_shape, attn_shape, attn_shape,
                   jax.ShapeDtypeStruct((B, S, 2 * MLSTM_WIDTH), BF16),
                   jax.ShapeDtypeStruct((B, S, MLSTM_WIDTH), BF16),
                   jax.ShapeDtypeStruct((B, S, MLSTM_WIDTH), BF16),
                   jax.ShapeDtypeStruct((B, 2 * MLSTM_HEADS, S), F32)],
        scratch_shapes=[pltpu.VMEM((CONV_HALO + tm, 2 * MLSTM_WIDTH), F32)],
        compiler_params=pltpu.CompilerParams(
            dimension_semantics=("parallel", "arbitrary"), vmem_limit_bytes=VMEM_LIMIT),
    )(x, g, w_main, w_gate, gate_b, cos_t, sin_t, conv_w, conv_b)


def _attn_kernel(q_ref, k_ref, v_ref, o_ref, stage, q4, k4, v4, q16, k16, v16,
                 acc1, acc4, acc16, st1, st4, st16, sd1, sd4, sd16, bias_ref, *, seq):
    blk = ATTN_BLOCK
    qi = lax.broadcasted_iota(jnp.int32, (2 * blk, 2 * blk), 0) % blk
    ki = lax.broadcasted_iota(jnp.int32, (2 * blk, 2 * blk), 1)
    dist = qi + blk - ki
    valid = (dist >= 0) & (dist <= blk)
    bias_ref[1] = jnp.where(valid, 0.0, NEG_INF)
    bias_ref[0] = jnp.where(valid & (ki >= blk), 0.0, NEG_INF)

    head0 = lax.broadcasted_iota(jnp.int32, (blk, LANES), 1) < ATTN_HEAD_DIM
    ones_kv = jnp.ones((2 * blk, LANES), BF16)

    conv_rows = 512
    for x_ref, views in ((q_ref, (q4, q16)), (k_ref, (k4, k16)), (v_ref, (v4, v16))):
        def to_f32(c, carry, x_ref=x_ref):
            rows = pl.ds(pl.multiple_of(c * conv_rows, conv_rows), conv_rows)
            stage[rows, :] = x_ref[rows, :].astype(F32)
            return carry
        lax.fori_loop(0, seq // conv_rows, to_f32, 0)
        for view, d in zip(views, DILATIONS[1:]):
            rows_v = 64
            def gather(c, carry, view=view, d=d):
                l0 = pl.multiple_of(c * rows_v, rows_v)
                for r in range(d):
                    view[pl.ds(l0, rows_v), r * LANES:(r + 1) * LANES] = (
                        stage[pl.ds(l0 * d + r, rows_v, stride=d), :].astype(BF16))
                return carry
            lax.fori_loop(0, seq // d // rows_v, gather, 0)

    patterns = ((q_ref, k_ref, v_ref, acc1, st1, sd1, DILATIONS[0]),
                (q4, k4, v4, acc4, st4, sd4, DILATIONS[1]),
                (q16, k16, v16, acc16, st16, sd16, DILATIONS[2]))

    def token_rows(ref, row0, r, d):
        if d == 1:
            return ref.at[pl.ds(row0, blk), :]
        return ref.at[pl.ds(row0 * d + r, blk, stride=d), :]

    def view_lanes(r, d):
        return slice(0, LANES) if d == 1 else pl.ds(pl.multiple_of(r * LANES, LANES), LANES)

    for (qv, kv, vv_ref, acc_ref, st_ref, sd_ref, d) in patterns:
        nb = seq // d // blk

        def block(i, carry, qv=qv, kv=kv, vv_ref=vv_ref, acc_ref=acc_ref, st_ref=st_ref, sd_ref=sd_ref, d=d, nb=nb):
            r = i // nb
            n = i % nb
            row0 = pl.multiple_of(n * blk, blk)
            prow0 = pl.multiple_of(jnp.maximum(n - 1, 0) * blk, blk)
            lanes = view_lanes(r, d)
            q = qv[pl.ds(row0, blk), lanes]
            zero = jnp.zeros_like(q)
            q2 = jnp.concatenate([jnp.where(head0, q, zero), jnp.where(head0, zero, q)], axis=0)
            kk = jnp.concatenate([kv[pl.ds(prow0, blk), lanes], kv[pl.ds(row0, blk), lanes]], axis=0)
            vv = jnp.concatenate([vv_ref[pl.ds(prow0, blk), lanes], vv_ref[pl.ds(row0, blk), lanes]], axis=0)
            s = lax.dot_general(q2, kk, (((1,), (1,)), ((), ())), preferred_element_type=F32)
            s = s + bias_ref[jnp.minimum(n, 1)]
            m = jnp.max(s, axis=1, keepdims=True)
            p = jnp.exp2(s - m).astype(BF16)
            res = jnp.dot(p, jnp.concatenate([vv, ones_kv], axis=1), preferred_element_type=F32)
            acc_ref[pl.ds(row0, blk), lanes] = jnp.where(head0, res[:blk, :LANES], res[blk:, :LANES])
            token_rows(st_ref, row0, r, d)[...] = jnp.where(head0, m[:blk], m[blk:])
            token_rows(sd_ref, row0, r, d)[...] = jnp.where(head0, res[:blk, LANES:], res[blk:, LANES:])
            return carry

        lax.fori_loop(0, d * nb, block, 0, unroll=ATTN_UNROLL)

    rows_b = 512

    def weights(c, carry):
        rows = pl.ds(pl.multiple_of(c * rows_b, rows_b), rows_b)
        a, b, e = st1[rows, :], st4[rows, :], st16[rows, :]
        top = jnp.maximum(jnp.maximum(a, b), e)
        ea, eb, ee = jnp.exp2(a - top), jnp.exp2(b - top), jnp.exp2(e - top)
        inv = 1.0 / (ea * sd1[rows, :] + eb * sd4[rows, :] + ee * sd16[rows, :])
        st1[rows, :] = ea * inv
        st4[rows, :] = eb * inv
        st16[rows, :] = ee * inv
        return carry

    lax.fori_loop(0, seq // rows_b, weights, 0)

    for (_, _, _, acc_ref, st_ref, _, d) in patterns:
        nb = seq // d // blk

        def scale(i, carry, acc_ref=acc_ref, st_ref=st_ref, d=d, nb=nb):
            r = i // nb
            n = i % nb
            row0 = pl.multiple_of(n * blk, blk)
            scaled = acc_ref[pl.ds(row0, blk), view_lanes(r, d)] * token_rows(st_ref, row0, r, d)[...]
            dst = token_rows(acc1, row0, r, d)
            dst[...] = scaled if d == 1 else dst[...] + scaled
            return carry

        lax.fori_loop(0, d * nb, scale, 0)

    def emit(c, carry):
        rows = pl.ds(pl.multiple_of(c * conv_rows, conv_rows), conv_rows)
        o_ref[rows, :] = acc1[rows, :].astype(BF16)
        return carry

    lax.fori_loop(0, seq // conv_rows, emit, 0)


def _dilated_attention(q, k, v):
    B, S, W = q.shape
    spec = pl.BlockSpec((None, S, LANES), lambda b, h: (b, 0, h))
    scratch = ([pltpu.VMEM((S, LANES), F32)]
               + [pltpu.VMEM((S // d, d * LANES), BF16) for d in DILATIONS[1:] for _ in range(3)]
               + [pltpu.VMEM((S // d, d * LANES), F32) for d in DILATIONS]
               + [pltpu.VMEM((S, LANES), F32) for _ in DILATIONS] * 2
               + [pltpu.VMEM((2, 2 * ATTN_BLOCK, 2 * ATTN_BLOCK), F32)])
    return pl.pallas_call(
        functools.partial(_attn_kernel, seq=S),
        name="dilated_attention",
        grid=(B, W // LANES),
        in_specs=[spec, spec, spec],
        out_specs=spec,
        out_shape=jax.ShapeDtypeStruct((B, S, W), BF16),
        scratch_shapes=scratch,
        compiler_params=pltpu.CompilerParams(
            dimension_semantics=("parallel", "parallel"), vmem_limit_bytes=VMEM_LIMIT),
    )(q, k, v)


def _mlstm_kernel(qk_ref, v_ref, mo_ref, gt_ref, ng_ref, o_ref, c_ref, m_ref, a_ref, b_ref, col_ref, *, rows):
    L, H, dh = MLSTM_CHUNK, MLSTM_HEADS, MLSTM_HEAD_DIM
    nc = rows // L

    @pl.when(pl.program_id(1) == 0)
    def _():
        c_ref[...] = jnp.zeros_like(c_ref)
        m_ref[...] = jnp.zeros_like(m_ref)

    gates = gt_ref[...]
    f_pre = gates[H:]
    b = (jnp.minimum(f_pre, 0.0) - jnp.log(1.0 + jnp.exp(-jnp.abs(f_pre)))) * LOG2E
    pos = lax.broadcasted_iota(jnp.int32, (H, rows), 1) % L
    for step in [1 << e for e in range(L.bit_length() - 1)]:
        b = b + jnp.where(pos >= step, pltpu.roll(b, step, axis=1), 0.0)
    a = gates[:H] * LOG2E - b
    amax = a
    for step in [1 << e for e in range(L.bit_length() - 1)]:
        amax = jnp.maximum(amax, jnp.where(pos >= step, pltpu.roll(amax, step, axis=1), NEG_INF))
    a_ref[...] = a
    b_ref[...] = b
    col_ref[...] = jnp.concatenate([b, amax, a, jnp.zeros((LANES - 3 * H, rows), F32)], axis=0).T

    causal = lax.broadcasted_iota(jnp.int32, (L, L), 1) <= lax.broadcasted_iota(jnp.int32, (L, L), 0)
    ones_blk = jnp.ones((L, dh), BF16)
    wide = lambda x: jnp.concatenate([x, x], axis=1)

    def head_chunk(h, c, r0):
        cols = slice(h * dh, (h + 1) * dh)
        q_b = qk_ref[pl.ds(r0, L), cols]
        k_b = qk_ref[pl.ds(r0, L), MLSTM_WIDTH + h * dh:MLSTM_WIDTH + (h + 1) * dh]
        v_b = v_ref[pl.ds(r0, L), cols]

        column = lambda j: jnp.broadcast_to(col_ref[pl.ds(r0, L), j:j + 1], (L, dh))
        b_rep, amax_rep, a_rep = column(h), column(H + h), column(2 * H + h)
        a_row = a_ref[h:h + 1, pl.ds(r0, L)]
        b_last = b_ref[h:h + 1, pl.ds(r0, L)][:, L - 1:L]
        m_prev = m_ref[h, 0:1, :]
        mx = jnp.maximum(m_prev, amax_rep)

        w_intra = jnp.exp2(jnp.where(causal, a_row - wide(mx), NEG_INF))
        qk = lax.dot_general(q_b, k_b, (((1,), (1,)), ((), ())), preferred_element_type=F32) * w_intra
        res = (wide(jnp.exp2(m_prev - mx)) * jnp.dot(q_b, c_ref[h].astype(BF16), preferred_element_type=F32)
               + jnp.dot(qk.astype(BF16), jnp.concatenate([v_b, ones_blk], axis=1), preferred_element_type=F32))
        hid = res[:, :dh] / jnp.maximum(jnp.abs(res[:, dh:]), jnp.exp2(-(b_rep + mx)))

        m_new = b_last + jnp.maximum(m_prev, jnp.max(a_row, axis=1, keepdims=True))
        decay = jnp.exp2(b_last + m_prev - m_new)
        ws = jnp.exp2(b_last + a_rep - m_new)
        wv = jnp.concatenate([(v_b.astype(F32) * ws).astype(BF16), ws.astype(BF16)], axis=1)
        c_ref[h] = wide(decay) * c_ref[h] + lax.dot_general(k_b, wv, (((0,), (0,)), ((), ())),
                                                            preferred_element_type=F32)
        m_ref[h] = jnp.broadcast_to(m_new, m_ref.shape[1:])

        hn = _rms(hid, ng_ref[:, cols])
        o_ref[pl.ds(r0, L), cols] = (jax.nn.sigmoid(mo_ref[pl.ds(r0, L), cols].astype(F32)) * hn).astype(BF16)

    def chunk(c, carry):
        r0 = pl.multiple_of(c * L, L)
        for h in range(H):
            head_chunk(h, c, r0)
        return carry

    lax.fori_loop(0, nc, chunk, 0)


def _mlstm(mqk, mv, mo, gates_t, norm_g, parts=2):
    B, S, W = mv.shape
    H, dh = MLSTM_HEADS, MLSTM_HEAD_DIM
    rows = S // parts
    tok = lambda width: pl.BlockSpec((None, rows, width), lambda b, i: (b, i, 0))
    return pl.pallas_call(
        functools.partial(_mlstm_kernel, rows=rows),
        name="mlstm",
        grid=(B, parts),
        in_specs=[tok(2 * W), tok(W), tok(W),
                  pl.BlockSpec((None, 2 * H, rows), lambda b, i: (b, 0, i)),
                  pl.BlockSpec(norm_g.shape, lambda b, i: (0, 0))],
        out_specs=tok(W),
        out_shape=jax.ShapeDtypeStruct((B, S, W), BF16),
        scratch_shapes=[pltpu.VMEM((H, dh, 2 * dh), F32), pltpu.VMEM((H, 8, LANES), F32),
                        pltpu.VMEM((H, rows), F32), pltpu.VMEM((H, rows), F32), pltpu.VMEM((rows, LANES), F32)],
        compiler_params=pltpu.CompilerParams(
            dimension_semantics=("parallel", "arbitrary"), vmem_limit_bytes=VMEM_LIMIT),
    )(mqk, mv, mo, gates_t, norm_g)


def _tail_kernel(x_ref, at_ref, ml_ref, p_ref, wo_ref, g_mlp_ref, wu_ref, wd_ref,
                 g_ple_ref, wpg_ref, wple_ref, g_fin_ref, o_ref, *, ff_chunk, final_norm):
    mix = jnp.concatenate([at_ref[...], ml_ref[...]], axis=1)
    h = x_ref[...] + jnp.dot(mix, wo_ref[...], preferred_element_type=F32)

    u = _rms(h, g_mlp_ref[...]).astype(BF16)
    for c0 in range(0, D_FF, ff_chunk):
        a = jnp.dot(u, wu_ref[:, c0:c0 + ff_chunk], preferred_element_type=F32)
        a = jnp.square(jnp.maximum(a, 0.0)).astype(BF16)
        h = h + jnp.dot(a, wd_ref[c0:c0 + ff_chunk, :], preferred_element_type=F32)

    gate = jax.nn.sigmoid(jnp.dot(_rms(h, g_ple_ref[...]).astype(BF16), wpg_ref[...], preferred_element_type=F32))
    h = h + gate * jnp.dot(p_ref[...].astype(BF16), wple_ref[...], preferred_element_type=F32)
    o_ref[...] = _rms(h, g_fin_ref[...]) if final_norm else h


def _tail(x, attn, ml, p, w_out, g_mlp, w_up, w_down, g_ple, w_pg, w_ple, g_fin, final_norm,
          tm=512, ff_chunk=1024):
    B, S, D = x.shape
    tok = lambda width: pl.BlockSpec((None, tm, width), lambda b, i: (b, i, 0))
    return pl.pallas_call(
        functools.partial(_tail_kernel, ff_chunk=ff_chunk, final_norm=final_norm),
        name="tail",
        grid=(B, S // tm),
        in_specs=[tok(D), tok(ATTN_WIDTH), tok(MLSTM_WIDTH), tok(PLE_DIM),
                  _resident(w_out.shape), _resident((1, D)), _resident(w_up.shape), _resident(w_down.shape),
                  _resident((1, D)), _resident(w_pg.shape), _resident(w_ple.shape), _resident((1, D))],
        out_specs=tok(D),
        out_shape=jax.ShapeDtypeStruct((B, S, D), F32),
        compiler_params=pltpu.CompilerParams(
            dimension_semantics=("parallel", "parallel"), vmem_limit_bytes=VMEM_LIMIT),
    )(x, attn, ml, p, w_out, g_mlp, w_up, w_down, g_ple, w_pg, w_ple, g_fin)


def _rope_tables(seq):
    half = ROPE_DIM // 2
    inv_freq = jnp.power(ROPE_THETA, -jnp.arange(half, dtype=F32) / half)
    ang = jnp.arange(seq, dtype=jnp.int32).astype(F32)[:, None] * inv_freq[None, :]
    cos, sin = jnp.cos(ang), jnp.sin(ang)
    rest = ATTN_HEAD_DIM - ROPE_DIM
    cos_head = jnp.concatenate([cos, cos, jnp.ones((seq, rest), F32)], axis=1)
    sin_head = jnp.concatenate([-sin, sin, jnp.zeros((seq, rest), F32)], axis=1)
    reps = LANES // ATTN_HEAD_DIM
    return jnp.tile(cos_head, (1, reps)), jnp.tile(sin_head, (1, reps))


def kernel(x, p, norm_mix_g, w_in, conv_w, conv_b, gate_b, mlstm_norm_g, w_out, norm_mlp_g, w_up, w_down,
           norm_ple_g, w_ple_gate, w_ple, final_norm_g):
    B, S, D = x.shape
    depth = w_in.shape[0]
    cos_t, sin_t = _rope_tables(S)
    row = lambda v: v.reshape(1, -1).astype(F32)
    n_gate = 2 * MLSTM_HEADS
    h = x
    for layer in range(depth):
        w = w_in[layer]
        w_main = jnp.concatenate([w[:, :ATTN_WIDTH] * (LOG2E / math.sqrt(ATTN_HEAD_DIM)),
                                  w[:, ATTN_WIDTH:MAIN_WIDTH]], axis=1).astype(BF16)
        w_gate = jnp.pad(w[:, MAIN_WIDTH:], ((0, 0), (0, LANES - n_gate))).astype(BF16)
        gb = jnp.pad(gate_b[layer].astype(F32), (0, LANES - n_gate)).reshape(1, LANES)
        q, k, v, mqk, mv, mo, gates_t = _in_proj(h, row(norm_mix_g[layer]), w_main, w_gate, gb, cos_t, sin_t,
                                                 conv_w[layer].astype(F32), row(conv_b[layer]))
        attn = _dilated_attention(q, k, v)
        ml = _mlstm(mqk, mv, mo, gates_t, row(mlstm_norm_g[layer]))
        h = _tail(h, attn, ml, p[layer], w_out[layer].astype(BF16), row(norm_mlp_g[layer]),
                  w_up[layer].astype(BF16), w_down[layer].astype(BF16), row(norm_ple_g[layer]),
                  w_ple_gate[layer].astype(BF16), w_ple[layer].astype(BF16), row(final_norm_g),
                  final_norm=(layer == depth - 1))
    return h
```

```python
import functools
import math

import jax
import jax.numpy as jnp
from jax import lax
from jax.experimental import pallas as pl
from jax.experimental.pallas import tpu as pltpu

D_MODEL = 1024
ATTN_HEADS = 8
ATTN_HEAD_DIM = 64
ATTN_WIDTH = ATTN_HEADS * ATTN_HEAD_DIM
MLSTM_HEADS = 4
MLSTM_HEAD_DIM = 128
MLSTM_WIDTH = MLSTM_HEADS * MLSTM_HEAD_DIM
DILATIONS = (1, 4, 16)
ATTN_BLOCK = 128
ATTN_UNROLL = 8
ROPE_THETA = 500000.0
ROPE_DIM = ATTN_HEAD_DIM // 4
CONV_WIDTH = 4
D_FF = 4 * D_MODEL
PLE_DIM = 256
RMS_EPS = 1e-6

LANES = 128
HEAD_PAIRS = ATTN_WIDTH // LANES
MAIN_WIDTH = 3 * ATTN_WIDTH + 4 * MLSTM_WIDTH
MLSTM_CHUNK = 256
CONV_HALO = 8
VMEM_LIMIT = 56 * 1024 * 1024

F32 = jnp.float32
BF16 = jnp.bfloat16
NEG_INF = float("-inf")
LOG2E = 1.4426950408889634


def _rms(x, g):
    return x * lax.rsqrt(jnp.mean(x * x, axis=-1, keepdims=True) + RMS_EPS) * g


def _resident(shape):
    return pl.BlockSpec(shape, lambda *_: (0,) * len(shape), pipeline_mode=pl.Buffered(1))


def _in_proj_kernel(x_ref, g_ref, w_ref, wg_ref, gb_ref, cos_ref, sin_ref, cw_ref, cb_ref,
                    q_ref, k_ref, v_ref, mqk_ref, mv_ref, mo_ref, gt_ref, tail_ref, *, tm):
    @pl.when(pl.program_id(1) == 0)
    def _():
        tail_ref[:CONV_HALO, :] = jnp.zeros((CONV_HALO, tail_ref.shape[1]), F32)

    u = _rms(x_ref[...], g_ref[...]).astype(BF16)
    pos0 = pl.multiple_of(pl.program_id(1) * tm, tm)
    cos = cos_ref[pl.ds(pos0, tm), :]
    sin = sin_ref[pl.ds(pos0, tm), :]
    lane = lax.broadcasted_iota(jnp.int32, (tm, LANES), 1)
    first_half = (lane % ATTN_HEAD_DIM) < (ROPE_DIM // 2)

    def proj(c0, width):
        return jnp.dot(u, w_ref[:, c0:c0 + width], preferred_element_type=F32)

    def rope_store(y, o_ref):
        for j in range(HEAD_PAIRS):
            yj = y[:, j * LANES:(j + 1) * LANES]
            partner = jnp.where(first_half,
                                pltpu.roll(yj, LANES - ROPE_DIM // 2, axis=1),
                                pltpu.roll(yj, ROPE_DIM // 2, axis=1))
            o_ref[:, j * LANES:(j + 1) * LANES] = (yj * cos + partner * sin).astype(BF16)

    A, M = ATTN_WIDTH, MLSTM_WIDTH
    tail_ref[CONV_HALO:, :] = proj(3 * A, 2 * M)
    k_scale = 1.0 / math.sqrt(MLSTM_HEAD_DIM)
    rb = 128
    for c0 in range(0, 2 * M, LANES):
        cols = slice(c0, c0 + LANES)
        for r0 in range(0, tm, rb):
            acc = cb_ref[:, cols]
            for j in range(CONV_WIDTH):
                start = CONV_HALO + r0 - j
                acc = acc + cw_ref[CONV_WIDTH - 1 - j:CONV_WIDTH - j, cols] * tail_ref[start:start + rb, cols]
            act = acc * jax.nn.sigmoid(acc)
            mqk_ref[r0:r0 + rb, cols] = (act if c0 < M else act * k_scale).astype(BF16)
    tail_ref[:CONV_HALO, :] = tail_ref[tm:tm + CONV_HALO, :]
    rope_store(proj(0, A), q_ref)
    rope_store(proj(A, A), k_ref)
    v_ref[...] = proj(2 * A, A).astype(BF16)
    mv_ref[...] = proj(3 * A + 2 * M, M).astype(BF16)
    mo_ref[...] = proj(3 * A + 3 * M, M).astype(BF16)
    gates = jnp.dot(u, wg_ref[...], preferred_element_type=F32) + gb_ref[...]
    gt_ref[...] = gates.T[:2 * MLSTM_HEADS, :]


def _in_proj(x, g, w_main, w_gate, gate_b, cos_t, sin_t, conv_w, conv_b, tm=512):
    B, S, D = x.shape
    grid = (B, S // tm)
    tok = lambda width: pl.BlockSpec((None, tm, width), lambda b, i: (b, i, 0))
    attn_shape = jax.ShapeDtypeStruct((B, S, ATTN_WIDTH), BF16)
    return pl.pallas_call(
        functools.partial(_in_proj_kernel, tm=tm),
        name="in_proj",
        grid=grid,
        in_specs=[tok(D), _resident((1, D)), _resident(w_main.shape), _resident(w_gate.shape),
                  _resident((1, LANES)), _resident(cos_t.shape), _resident(sin_t.shape),
                  _resident(conv_w.shape), _resident(conv_b.shape)],
        out_specs=[tok(ATTN_WIDTH), tok(ATTN_WIDTH), tok(ATTN_WIDTH),
                   tok(2 * MLSTM_WIDTH), tok(MLSTM_WIDTH), tok(MLSTM_WIDTH),
                   pl.BlockSpec((None, 2 * MLSTM_HEADS, tm), lambda b, i: (b, 0, i))],
        out_shape=[attn_shape, attn_shape, attn_shape,
                   jax.ShapeDtypeStruct((B, S, 2 * MLSTM_WIDTH), BF16),
                   jax.ShapeDtypeStruct((B, S, MLSTM_WIDTH), BF16),
                   jax.ShapeDtypeStruct((B, S, MLSTM_WIDTH), BF16),
                   jax.ShapeDtypeStruct((B, 2 * MLSTM_HEADS, S), F32)],
        scratch_shapes=[pltpu.VMEM((CONV_HALO + tm, 2 * MLSTM_WIDTH), F32)],
        compiler_params=pltpu.CompilerParams(
            dimension_semantics=("parallel", "arbitrary"), vmem_limit_bytes=VMEM_LIMIT),
    )(x, g, w_main, w_gate, gate_b, cos_t, sin_t, conv_w, conv_b)


def _attn_kernel(q_ref, k_ref, v_ref, o_ref, stage, s4, q4, k4, v4, q16, k16, v16,
                 acc1, acc4, acc16, sm1, sm4, sm16, sd1, sd4, sd16, bias_ref, *, seq):
    blk = ATTN_BLOCK
    d4, d16 = DILATIONS[1], DILATIONS[2]
    assert DILATIONS[0] == 1 and d16 == d4 * d4
    qi = lax.broadcasted_iota(jnp.int32, (2 * blk, 2 * blk), 0) % blk
    ki = lax.broadcasted_iota(jnp.int32, (2 * blk, 2 * blk), 1)
    dist = qi + blk - ki
    valid = (dist >= 0) & (dist <= blk)
    bias_ref[1] = jnp.where(valid, 0.0, NEG_INF)
    bias_ref[0] = jnp.where(valid & (ki >= blk), 0.0, NEG_INF)

    head0 = lax.broadcasted_iota(jnp.int32, (blk, LANES), 1) < ATTN_HEAD_DIM
    ones_kv = jnp.ones((2 * blk, LANES), BF16)

    conv_rows = 512
    for x_ref, x4, x16 in ((q_ref, q4, q16), (k_ref, k4, k16), (v_ref, v4, v16)):
        def to_f32(c, carry, x_ref=x_ref):
            rows = pl.ds(pl.multiple_of(c * conv_rows, conv_rows), conv_rows)
            stage[rows, :] = x_ref[rows, :].astype(F32)
            return carry
        lax.fori_loop(0, seq // conv_rows, to_f32, 0)

        def by4(c, carry, x4=x4):
            l0 = pl.multiple_of(c * blk, blk)
            for r in range(d4):
                rows = stage[pl.ds(l0 * d4 + r, blk, stride=d4), :]
                s4[r, pl.ds(l0, blk), :] = rows
                x4[pl.ds(l0, blk), r * LANES:(r + 1) * LANES] = rows.astype(BF16)
            return carry
        lax.fori_loop(0, seq // d4 // blk, by4, 0)

        def by16(c, carry, x16=x16):
            l0 = pl.multiple_of(c * 64, 64)
            for r in range(d16):
                x16[pl.ds(l0, 64), r * LANES:(r + 1) * LANES] = (
                    s4[r % d4, pl.ds(l0 * d4 + r // d4, 64, stride=d4), :].astype(BF16))
            return carry
        lax.fori_loop(0, seq // d16 // 64, by16, 0)

    patterns = ((q_ref, k_ref, v_ref, acc1, sm1, sd1, 1),
                (q4, k4, v4, acc4, sm4, sd4, d4),
                (q16, k16, v16, acc16, sm16, sd16, d16))

    def result_rows(ref, row0, r, d):
        if d == 1:
            return ref.at[pl.ds(row0, blk), :]
        if d == d4:
            return ref.at[r, pl.ds(row0, blk), :]
        return ref.at[r % d4, pl.ds(row0 * d4 + r // d4, blk, stride=d4), :]

    def view_lanes(r, d):
        return slice(0, LANES) if d == 1 else pl.ds(pl.multiple_of(r * LANES, LANES), LANES)

    for (qv, kv, vv_ref, acc_ref, sm_ref, sd_ref, d) in patterns:
        nb = seq // d // blk

        def block(i, carry, qv=qv, kv=kv, vv_ref=vv_ref, acc_ref=acc_ref, sm_ref=sm_ref, sd_ref=sd_ref, d=d, nb=nb):
            r = i // nb
            n = i % nb
            row0 = pl.multiple_of(n * blk, blk)
            prow0 = pl.multiple_of(jnp.maximum(n - 1, 0) * blk, blk)
            lanes = view_lanes(r, d)
            q = qv[pl.ds(row0, blk), lanes]
            zero = jnp.zeros_like(q)
            q2 = jnp.concatenate([jnp.where(head0, q, zero), jnp.where(head0, zero, q)], axis=0)
            kk = jnp.concatenate([kv[pl.ds(prow0, blk), lanes], kv[pl.ds(row0, blk), lanes]], axis=0)
            vv = jnp.concatenate([vv_ref[pl.ds(prow0, blk), lanes], vv_ref[pl.ds(row0, blk), lanes]], axis=0)
            s = lax.dot_general(q2, kk, (((1,), (1,)), ((), ())), preferred_element_type=F32)
            s = s + bias_ref[jnp.minimum(n, 1)]
            m = jnp.max(s, axis=1, keepdims=True)
            p = jnp.exp2(s - m).astype(BF16)
            res = jnp.dot(p, jnp.concatenate([vv, ones_kv], axis=1), preferred_element_type=F32)
            result_rows(acc_ref, row0, r, d)[...] = jnp.where(head0, res[:blk, :LANES], res[blk:, :LANES])
            result_rows(sm_ref, row0, r, d)[...] = jnp.where(head0, m[:blk], m[blk:])
            result_rows(sd_ref, row0, r, d)[...] = jnp.where(head0, res[:blk, LANES:], res[blk:, LANES:])
            return carry

        lax.fori_loop(0, d * nb, block, 0, unroll=ATTN_UNROLL)

    rows_m = 256

    def mix(c, carry):
        l0 = pl.multiple_of(c * rows_m, rows_m)
        com = pl.ds(l0, rows_m)
        for r in range(d4):
            tok = pl.ds(l0 * d4 + r, rows_m, stride=d4)
            m1, m4, m16 = sm1[tok, :], sm4[r, com, :], sm16[r, com, :]
            top = jnp.maximum(jnp.maximum(m1, m4), m16)
            e1, e4, e16 = jnp.exp2(m1 - top), jnp.exp2(m4 - top), jnp.exp2(m16 - top)
            den = e1 * sd1[tok, :] + e4 * sd4[r, com, :] + e16 * sd16[r, com, :]
            num = e1 * acc1[tok, :] + e4 * acc4[r, com, :] + e16 * acc16[r, com, :]
            stage[tok, :] = num * (1.0 / den)
        return carry

    lax.fori_loop(0, seq // d4 // rows_m, mix, 0)

    def emit(c, carry):
        rows = pl.ds(pl.multiple_of(c * conv_rows, conv_rows), conv_rows)
        o_ref[rows, :] = stage[rows, :].astype(BF16)
        return carry

    lax.fori_loop(0, seq // conv_rows, emit, 0)


def _dilated_attention(q, k, v):
    B, S, W = q.shape
    d4 = DILATIONS[1]
    spec = pl.BlockSpec((None, S, LANES), lambda b, h: (b, 0, h))
    token = pltpu.VMEM((S, LANES), F32)
    by4 = pltpu.VMEM((d4, S // d4, LANES), F32)
    scratch = ([token, by4]
               + [pltpu.VMEM((S // d, d * LANES), BF16) for d in DILATIONS[1:] for _ in range(3)]
               + [token, by4, by4] * 3
               + [pltpu.VMEM((2, 2 * ATTN_BLOCK, 2 * ATTN_BLOCK), F32)])
    return pl.pallas_call(
        functools.partial(_attn_kernel, seq=S),
        name="dilated_attention",
        grid=(B, W // LANES),
        in_specs=[spec, spec, spec],
        out_specs=spec,
        out_shape=jax.ShapeDtypeStruct((B, S, W), BF16),
        scratch_shapes=scratch,
        compiler_params=pltpu.CompilerParams(
            dimension_semantics=("parallel", "parallel"), vmem_limit_bytes=VMEM_LIMIT),
    )(q, k, v)


def _mlstm_kernel(qk_ref, v_ref, mo_ref, gt_ref, ng_ref, o_ref, c_ref, m_ref, a_ref, b_ref, col_ref, *, rows):
    L, H, dh = MLSTM_CHUNK, MLSTM_HEADS, MLSTM_HEAD_DIM
    nc = rows // L

    @pl.when(pl.program_id(1) == 0)
    def _():
        c_ref[...] = jnp.zeros_like(c_ref)
        m_ref[...] = jnp.zeros_like(m_ref)

    gates = gt_ref[...]
    f_pre = gates[H:]
    b = (jnp.minimum(f_pre, 0.0) - jnp.log(1.0 + jnp.exp(-jnp.abs(f_pre)))) * LOG2E
    pos = lax.broadcasted_iota(jnp.int32, (H, rows), 1) % L
    for step in [1 << e for e in range(L.bit_length() - 1)]:
        b = b + jnp.where(pos >= step, pltpu.roll(b, step, axis=1), 0.0)
    a = gates[:H] * LOG2E - b
    amax = a
    for step in [1 << e for e in range(L.bit_length() - 1)]:
        amax = jnp.maximum(amax, jnp.where(pos >= step, pltpu.roll(amax, step, axis=1), NEG_INF))
    a_ref[...] = a
    b_ref[...] = b
    col_ref[...] = jnp.concatenate([b, amax, a, jnp.zeros((LANES - 3 * H, rows), F32)], axis=0).T

    causal = lax.broadcasted_iota(jnp.int32, (L, L), 1) <= lax.broadcasted_iota(jnp.int32, (L, L), 0)
    ones_blk = jnp.ones((L, dh), BF16)
    wide = lambda x: jnp.concatenate([x, x], axis=1)

    def head_chunk(h, c, r0):
        cols = slice(h * dh, (h + 1) * dh)
        q_b = qk_ref[pl.ds(r0, L), cols]
        k_b = qk_ref[pl.ds(r0, L), MLSTM_WIDTH + h * dh:MLSTM_WIDTH + (h + 1) * dh]
        v_b = v_ref[pl.ds(r0, L), cols]

        column = lambda j: jnp.broadcast_to(col_ref[pl.ds(r0, L), j:j + 1], (L, dh))
        b_rep, amax_rep, a_rep = column(h), column(H + h), column(2 * H + h)
        a_row = a_ref[h:h + 1, pl.ds(r0, L)]
        b_last = b_ref[h:h + 1, pl.ds(r0, L)][:, L - 1:L]
        m_prev = m_ref[h, 0:1, :]
        mx = jnp.maximum(m_prev, amax_rep)

        w_intra = jnp.exp2(jnp.where(causal, a_row - wide(mx), NEG_INF))
        qk = lax.dot_general(q_b, k_b, (((1,), (1,)), ((), ())), preferred_element_type=F32) * w_intra
        res = (wide(jnp.exp2(m_prev - mx)) * jnp.dot(q_b, c_ref[h].astype(BF16), preferred_element_type=F32)
               + jnp.dot(qk.astype(BF16), jnp.concatenate([v_b, ones_blk], axis=1), preferred_element_type=F32))
        hid = res[:, :dh] / jnp.maximum(jnp.abs(res[:, dh:]), jnp.exp2(-(b_rep + mx)))

        m_new = b_last + jnp.maximum(m_prev, jnp.max(a_row, axis=1, keepdims=True))
        decay = jnp.exp2(b_last + m_prev - m_new)
        ws = jnp.exp2(b_last + a_rep - m_new)
        wv = jnp.concatenate([(v_b.astype(F32) * ws).astype(BF16), ws.astype(BF16)], axis=1)
        c_ref[h] = wide(decay) * c_ref[h] + lax.dot_general(k_b, wv, (((0,), (0,)), ((), ())),
                                                            preferred_element_type=F32)
        m_ref[h] = jnp.broadcast_to(m_new, m_ref.shape[1:])

        hn = _rms(hid, ng_ref[:, cols])
        o_ref[pl.ds(r0, L), cols] = (jax.nn.sigmoid(mo_ref[pl.ds(r0, L), cols].astype(F32)) * hn).astype(BF16)

    def chunk(c, carry):
        r0 = pl.multiple_of(c * L, L)
        for h in range(H):
            head_chunk(h, c, r0)
        return carry

    lax.fori_loop(0, nc, chunk, 0)


def _mlstm(mqk, mv, mo, gates_t, norm_g, parts=2):
    B, S, W = mv.shape
    H, dh = MLSTM_HEADS, MLSTM_HEAD_DIM
    rows = S // parts
    tok = lambda width: pl.BlockSpec((None, rows, width), lambda b, i: (b, i, 0))
    return pl.pallas_call(
        functools.partial(_mlstm_kernel, rows=rows),
        name="mlstm",
        grid=(B, parts),
        in_specs=[tok(2 * W), tok(W), tok(W),
                  pl.BlockSpec((None, 2 * H, rows), lambda b, i: (b, 0, i)),
                  pl.BlockSpec(norm_g.shape, lambda b, i: (0, 0))],
        out_specs=tok(W),
        out_shape=jax.ShapeDtypeStruct((B, S, W), BF16),
        scratch_shapes=[pltpu.VMEM((H, dh, 2 * dh), F32), pltpu.VMEM((H, 8, LANES), F32),
                        pltpu.VMEM((H, rows), F32), pltpu.VMEM((H, rows), F32), pltpu.VMEM((rows, LANES), F32)],
        compiler_params=pltpu.CompilerParams(
            dimension_semantics=("parallel", "arbitrary"), vmem_limit_bytes=VMEM_LIMIT),
    )(mqk, mv, mo, gates_t, norm_g)


def _tail_kernel(x_ref, at_ref, ml_ref, p_ref, wo_ref, g_mlp_ref, wu_ref, wd_ref,
                 g_ple_ref, wpg_ref, wple_ref, g_fin_ref, o_ref, *, ff_chunk, final_norm):
    mix = jnp.concatenate([at_ref[...], ml_ref[...]], axis=1)
    h = x_ref[...] + jnp.dot(mix, wo_ref[...], preferred_element_type=F32)

    u = _rms(h, g_mlp_ref[...]).astype(BF16)
    for c0 in range(0, D_FF, ff_chunk):
        a = jnp.dot(u, wu_ref[:, c0:c0 + ff_chunk], preferred_element_type=F32)
        a = jnp.square(jnp.maximum(a, 0.0)).astype(BF16)
        h = h + jnp.dot(a, wd_ref[c0:c0 + ff_chunk, :], preferred_element_type=F32)

    gate = jax.nn.sigmoid(jnp.dot(_rms(h, g_ple_ref[...]).astype(BF16), wpg_ref[...], preferred_element_type=F32))
    h = h + gate * jnp.dot(p_ref[...].astype(BF16), wple_ref[...], preferred_element_type=F32)
    o_ref[...] = _rms(h, g_fin_ref[...]) if final_norm else h


def _tail(x, attn, ml, p, layer, w_out, g_mlp, w_up, w_down, g_ple, w_pg, w_ple, g_fin, final_norm,
          tm=512, ff_chunk=1024):
    B, S, D = x.shape
    tok = lambda width: pl.BlockSpec((None, tm, width), lambda b, i: (b, i, 0))
    p_spec = pl.BlockSpec((None, None, tm, PLE_DIM), lambda b, i: (layer, b, i, 0))
    return pl.pallas_call(
        functools.partial(_tail_kernel, ff_chunk=ff_chunk, final_norm=final_norm),
        name="tail",
        grid=(B, S // tm),
        in_specs=[tok(D), tok(ATTN_WIDTH), tok(MLSTM_WIDTH), p_spec,
                  _resident(w_out.shape), _resident((1, D)), _resident(w_up.shape), _resident(w_down.shape),
                  _resident((1, D)), _resident(w_pg.shape), _resident(w_ple.shape), _resident((1, D))],
        out_specs=tok(D),
        out_shape=jax.ShapeDtypeStruct((B, S, D), F32),
        compiler_params=pltpu.CompilerParams(
            dimension_semantics=("parallel", "parallel"), vmem_limit_bytes=VMEM_LIMIT),
    )(x, attn, ml, p, w_out, g_mlp, w_up, w_down, g_ple, w_pg, w_ple, g_fin)


def _rope_tables(seq):
    half = ROPE_DIM // 2
    inv_freq = jnp.power(ROPE_THETA, -jnp.arange(half, dtype=F32) / half)
    ang = jnp.arange(seq, dtype=jnp.int32).astype(F32)[:, None] * inv_freq[None, :]
    cos, sin = jnp.cos(ang), jnp.sin(ang)
    rest = ATTN_HEAD_DIM - ROPE_DIM
    cos_head = jnp.concatenate([cos, cos, jnp.ones((seq, rest), F32)], axis=1)
    sin_head = jnp.concatenate([-sin, sin, jnp.zeros((seq, rest), F32)], axis=1)
    reps = LANES // ATTN_HEAD_DIM
    return jnp.tile(cos_head, (1, reps)), jnp.tile(sin_head, (1, reps))


def kernel(x, p, norm_mix_g, w_in, conv_w, conv_b, gate_b, mlstm_norm_g, w_out, norm_mlp_g, w_up, w_down,
           norm_ple_g, w_ple_gate, w_ple, final_norm_g):
    B, S, D = x.shape
    depth = w_in.shape[0]
    cos_t, sin_t = _rope_tables(S)
    row = lambda v: v.reshape(1, -1).astype(F32)
    n_gate = 2 * MLSTM_HEADS
    h = x
    for layer in range(depth):
        w = w_in[layer]
        w_main = jnp.concatenate([w[:, :ATTN_WIDTH] * (LOG2E / math.sqrt(ATTN_HEAD_DIM)),
                                  w[:, ATTN_WIDTH:MAIN_WIDTH]], axis=1).astype(BF16)
        w_gate = jnp.pad(w[:, MAIN_WIDTH:], ((0, 0), (0, LANES - n_gate))).astype(BF16)
        gb = jnp.pad(gate_b[layer].astype(F32), (0, LANES - n_gate)).reshape(1, LANES)
        q, k, v, mqk, mv, mo, gates_t = _in_proj(h, row(norm_mix_g[layer]), w_main, w_gate, gb, cos_t, sin_t,
                                                 conv_w[layer].astype(F32), row(conv_b[layer]))
        attn = _dilated_attention(q, k, v)
        ml = _mlstm(mqk, mv, mo, gates_t, row(mlstm_norm_g[layer]))
        h = _tail(h, attn, ml, p, layer, w_out[layer].astype(BF16), row(norm_mlp_g[layer]),
                  w_up[layer].astype(BF16), w_down[layer].astype(BF16), row(norm_ple_g[layer]),
                  w_ple_gate[layer].astype(BF16), w_ple[layer].astype(BF16), row(final_norm_g),
                  final_norm=(layer == depth - 1))
    return h
```

```python
import functools
import math

import jax
import jax.numpy as jnp
from jax import lax
from jax.experimental import pallas as pl
from jax.experimental.pallas import tpu as pltpu

D_MODEL = 1024
ATTN_HEADS = 8
ATTN_HEAD_DIM = 64
ATTN_WIDTH = ATTN_HEADS * ATTN_HEAD_DIM
MLSTM_HEADS = 4
MLSTM_HEAD_DIM = 128
MLSTM_WIDTH = MLSTM_HEADS * MLSTM_HEAD_DIM
DILATIONS = (1, 4, 16)
ATTN_BLOCK = 128
ATTN_UNROLL = 8
ROPE_THETA = 500000.0
ROPE_DIM = ATTN_HEAD_DIM // 4
CONV_WIDTH = 4
D_FF = 4 * D_MODEL
PLE_DIM = 256
RMS_EPS = 1e-6

LANES = 128
HEAD_PAIRS = ATTN_WIDTH // LANES
MAIN_WIDTH = 3 * ATTN_WIDTH + 4 * MLSTM_WIDTH
MLSTM_CHUNK = 256
CONV_HALO = 8
VMEM_LIMIT = 56 * 1024 * 1024

F32 = jnp.float32
BF16 = jnp.bfloat16
NEG_INF = float("-inf")
LOG2E = 1.4426950408889634


def _rms(x, g):
    return x * lax.rsqrt(jnp.mean(x * x, axis=-1, keepdims=True) + RMS_EPS) * g


def _resident(shape):
    return pl.BlockSpec(shape, lambda *_: (0,) * len(shape), pipeline_mode=pl.Buffered(1))


def _in_proj_kernel(x_ref, g_ref, w_ref, wg_ref, gb_ref, cos_ref, sin_ref, cw_ref, cb_ref,
                    q_ref, k_ref, v_ref, mqk_ref, mv_ref, mo_ref, gt_ref, tail_ref, *, tm):
    @pl.when(pl.program_id(1) == 0)
    def _():
        tail_ref[:CONV_HALO, :] = jnp.zeros((CONV_HALO, tail_ref.shape[1]), F32)

    u = _rms(x_ref[...], g_ref[...]).astype(BF16)
    pos0 = pl.multiple_of(pl.program_id(1) * tm, tm)
    cos = cos_ref[pl.ds(pos0, tm), :]
    sin = sin_ref[pl.ds(pos0, tm), :]
    lane = lax.broadcasted_iota(jnp.int32, (tm, LANES), 1)
    first_half = (lane % ATTN_HEAD_DIM) < (ROPE_DIM // 2)

    def proj(c0, width):
        return jnp.dot(u, w_ref[:, c0:c0 + width], preferred_element_type=F32)

    def rope_store(y, o_ref):
        for j in range(HEAD_PAIRS):
            yj = y[:, j * LANES:(j + 1) * LANES]
            partner = jnp.where(first_half,
                                pltpu.roll(yj, LANES - ROPE_DIM // 2, axis=1),
                                pltpu.roll(yj, ROPE_DIM // 2, axis=1))
            o_ref[:, j * LANES:(j + 1) * LANES] = (yj * cos + partner * sin).astype(BF16)

    A, M = ATTN_WIDTH, MLSTM_WIDTH
    tail_ref[CONV_HALO:, :] = proj(3 * A, 2 * M)
    k_scale = 1.0 / math.sqrt(MLSTM_HEAD_DIM)
    rb = 128
    for c0 in range(0, 2 * M, LANES):
        cols = slice(c0, c0 + LANES)
        for r0 in range(0, tm, rb):
            acc = cb_ref[:, cols]
            for j in range(CONV_WIDTH):
                start = CONV_HALO + r0 - j
                acc = acc + cw_ref[CONV_WIDTH - 1 - j:CONV_WIDTH - j, cols] * tail_ref[start:start + rb, cols]
            act = acc * jax.nn.sigmoid(acc)
            mqk_ref[r0:r0 + rb, cols] = (act if c0 < M else act * k_scale).astype(BF16)
    tail_ref[:CONV_HALO, :] = tail_ref[tm:tm + CONV_HALO, :]
    rope_store(proj(0, A), q_ref)
    rope_store(proj(A, A), k_ref)
    v_ref[...] = proj(2 * A, A).astype(BF16)
    mv_ref[...] = proj(3 * A + 2 * M, M).astype(BF16)
    mo_ref[...] = proj(3 * A + 3 * M, M).astype(BF16)
    gates = jnp.dot(u, wg_ref[...], preferred_element_type=F32) + gb_ref[...]
    gt_ref[...] = gates.T[:2 * MLSTM_HEADS, :]


def _in_proj(x, g, w_main, w_gate, gate_b, cos_t, sin_t, conv_w, conv_b, tm=512):
    B, S, D = x.shape
    grid = (B, S // tm)
    tok = lambda width: pl.BlockSpec((None, tm, width), lambda b, i: (b, i, 0))
    attn_shape = jax.ShapeDtypeStruct((B, S, ATTN_WIDTH), BF16)
    return pl.pallas_call(
        functools.partial(_in_proj_kernel, tm=tm),
        name="in_proj",
        grid=grid,
        in_specs=[tok(D), _resident((1, D)), _resident(w_main.shape), _resident(w_gate.shape),
                  _resident((1, LANES)), _resident(cos_t.shape), _resident(sin_t.shape),
                  _resident(conv_w.shape), _resident(conv_b.shape)],
        out_specs=[tok(ATTN_WIDTH), tok(ATTN_WIDTH), tok(ATTN_WIDTH),
                   tok(2 * MLSTM_WIDTH), tok(MLSTM_WIDTH), tok(MLSTM_WIDTH),
                   pl.BlockSpec((None, 2 * MLSTM_HEADS, tm), lambda b, i: (b, 0, i))],
        out_shape=[attn_shape, attn_shape, attn_shape,
                   jax.ShapeDtypeStruct((B, S, 2 * MLSTM_WIDTH), BF16),
                   jax.ShapeDtypeStruct((B, S, MLSTM_WIDTH), BF16),
                   jax.ShapeDtypeStruct((B, S, MLSTM_WIDTH), BF16),
                   jax.ShapeDtypeStruct((B, 2 * MLSTM_HEADS, S), F32)],
        scratch_shapes=[pltpu.VMEM((CONV_HALO + tm, 2 * MLSTM_WIDTH), F32)],
        compiler_params=pltpu.CompilerParams(
            dimension_semantics=("parallel", "arbitrary"), vmem_limit_bytes=VMEM_LIMIT),
    )(x, g, w_main, w_gate, gate_b, cos_t, sin_t, conv_w, conv_b)


def _attn_kernel(q_ref, k_ref, v_ref, o_ref, stage, s4, q4, k4, v4, q16, k16, v16,
                 acc1, acc4, acc16, sm1, sm4, sm16, sd1, sd4, sd16, bias_ref, *, seq):
    blk = ATTN_BLOCK
    d4, d16 = DILATIONS[1], DILATIONS[2]
    assert DILATIONS[0] == 1 and d16 == d4 * d4
    qi = lax.broadcasted_iota(jnp.int32, (2 * blk, 2 * blk), 0) % blk
    ki = lax.broadcasted_iota(jnp.int32, (2 * blk, 2 * blk), 1)
    dist = qi + blk - ki
    valid = (dist >= 0) & (dist <= blk)
    bias_ref[1] = jnp.where(valid, 0.0, NEG_INF)
    bias_ref[0] = jnp.where(valid & (ki >= blk), 0.0, NEG_INF)

    head0 = lax.broadcasted_iota(jnp.int32, (blk, LANES), 1) < ATTN_HEAD_DIM
    ones_kv = jnp.ones((2 * blk, LANES), BF16)

    conv_rows = 512
    for x_ref, x4, x16 in ((q_ref, q4, q16), (k_ref, k4, k16), (v_ref, v4, v16)):
        def to_f32(c, carry, x_ref=x_ref):
            rows = pl.ds(pl.multiple_of(c * conv_rows, conv_rows), conv_rows)
            stage[rows, :] = x_ref[rows, :].astype(F32)
            return carry
        lax.fori_loop(0, seq // conv_rows, to_f32, 0)

        def by4(c, carry, x4=x4):
            l0 = pl.multiple_of(c * blk, blk)
            for r in range(d4):
                rows = stage[pl.ds(l0 * d4 + r, blk, stride=d4), :]
                s4[r, pl.ds(l0, blk), :] = rows
                x4[pl.ds(l0, blk), r * LANES:(r + 1) * LANES] = rows.astype(BF16)
            return carry
        lax.fori_loop(0, seq // d4 // blk, by4, 0)

        def by16(c, carry, x16=x16):
            l0 = pl.multiple_of(c * 64, 64)
            for r in range(d16):
                x16[pl.ds(l0, 64), r * LANES:(r + 1) * LANES] = (
                    s4[r % d4, pl.ds(l0 * d4 + r // d4, 64, stride=d4), :].astype(BF16))
            return carry
        lax.fori_loop(0, seq // d16 // 64, by16, 0)

    patterns = ((q_ref, k_ref, v_ref, acc1, sm1, sd1, 1),
                (q4, k4, v4, acc4, sm4, sd4, d4),
                (q16, k16, v16, acc16, sm16, sd16, d16))

    def result_rows(ref, row0, r, d):
        if d == 1:
            return ref.at[pl.ds(row0, blk), :]
        if d == d4:
            return ref.at[r, pl.ds(row0, blk), :]
        return ref.at[r % d4, pl.ds(row0 * d4 + r // d4, blk, stride=d4), :]

    def view_lanes(r, d):
        return slice(0, LANES) if d == 1 else pl.ds(pl.multiple_of(r * LANES, LANES), LANES)

    for (qv, kv, vv_ref, acc_ref, sm_ref, sd_ref, d) in patterns:
        nb = seq // d // blk

        def block(i, carry, qv=qv, kv=kv, vv_ref=vv_ref, acc_ref=acc_ref, sm_ref=sm_ref, sd_ref=sd_ref, d=d, nb=nb):
            r = i // nb
            n = i % nb
            row0 = pl.multiple_of(n * blk, blk)
            prow0 = pl.multiple_of(jnp.maximum(n - 1, 0) * blk, blk)
            lanes = view_lanes(r, d)
            q = qv[pl.ds(row0, blk), lanes]
            zero = jnp.zeros_like(q)
            q2 = jnp.concatenate([jnp.where(head0, q, zero), jnp.where(head0, zero, q)], axis=0)
            kk = jnp.concatenate([kv[pl.ds(prow0, blk), lanes], kv[pl.ds(row0, blk), lanes]], axis=0)
            vv = jnp.concatenate([vv_ref[pl.ds(prow0, blk), lanes], vv_ref[pl.ds(row0, blk), lanes]], axis=0)
            s = lax.dot_general(q2, kk, (((1,), (1,)), ((), ())), preferred_element_type=F32)
            s = s + bias_ref[jnp.minimum(n, 1)]
            m = jnp.max(s, axis=1, keepdims=True)
            p = jnp.exp2(s - m).astype(BF16)
            res = jnp.dot(p, jnp.concatenate([vv, ones_kv], axis=1), preferred_element_type=F32)
            result_rows(acc_ref, row0, r, d)[...] = jnp.where(head0, res[:blk, :LANES], res[blk:, :LANES])
            result_rows(sm_ref, row0, r, d)[...] = jnp.where(head0, m[:blk], m[blk:])
            result_rows(sd_ref, row0, r, d)[...] = jnp.where(head0, res[:blk, LANES:], res[blk:, LANES:])
            return carry

        lax.fori_loop(0, d * nb, block, 0, unroll=ATTN_UNROLL)

    rows_m = 256

    def mix(c, carry):
        l0 = pl.multiple_of(c * rows_m, rows_m)
        com = pl.ds(l0, rows_m)
        for r in range(d4):
            tok = pl.ds(l0 * d4 + r, rows_m, stride=d4)
            m1, m4, m16 = sm1[tok, :], sm4[r, com, :], sm16[r, com, :]
            top = jnp.maximum(jnp.maximum(m1, m4), m16)
            e1, e4, e16 = jnp.exp2(m1 - top), jnp.exp2(m4 - top), jnp.exp2(m16 - top)
            den = e1 * sd1[tok, :] + e4 * sd4[r, com, :] + e16 * sd16[r, com, :]
            num = e1 * acc1[tok, :] + e4 * acc4[r, com, :] + e16 * acc16[r, com, :]
            stage[tok, :] = num * (1.0 / den)
        return carry

    lax.fori_loop(0, seq // d4 // rows_m, mix, 0)

    def emit(c, carry):
        rows = pl.ds(pl.multiple_of(c * conv_rows, conv_rows), conv_rows)
        o_ref[rows, :] = stage[rows, :].astype(BF16)
        return carry

    lax.fori_loop(0, seq // conv_rows, emit, 0)


def _dilated_attention(q, k, v):
    B, S, W = q.shape
    d4 = DILATIONS[1]
    spec = pl.BlockSpec((None, S, LANES), lambda b, h: (b, 0, h))
    token = pltpu.VMEM((S, LANES), F32)
    by4 = pltpu.VMEM((d4, S // d4, LANES), F32)
    scratch = ([token, by4]
               + [pltpu.VMEM((S // d, d * LANES), BF16) for d in DILATIONS[1:] for _ in range(3)]
               + [token, by4, by4] * 3
               + [pltpu.VMEM((2, 2 * ATTN_BLOCK, 2 * ATTN_BLOCK), F32)])
    return pl.pallas_call(
        functools.partial(_attn_kernel, seq=S),
        name="dilated_attention",
        grid=(B, W // LANES),
        in_specs=[spec, spec, spec],
        out_specs=spec,
        out_shape=jax.ShapeDtypeStruct((B, S, W), BF16),
        scratch_shapes=scratch,
        compiler_params=pltpu.CompilerParams(
            dimension_semantics=("parallel", "parallel"), vmem_limit_bytes=VMEM_LIMIT),
    )(q, k, v)


def _gate_scans(gt_ref, a_ref, b_ref, col_ref):
    L, H = MLSTM_CHUNK, MLSTM_HEADS
    rows = gt_ref.shape[1]
    gates = gt_ref[...]
    f_pre = gates[H:]
    b = (jnp.minimum(f_pre, 0.0) - jnp.log(1.0 + jnp.exp(-jnp.abs(f_pre)))) * LOG2E
    pos = lax.broadcasted_iota(jnp.int32, (H, rows), 1) % L
    for step in [1 << e for e in range(L.bit_length() - 1)]:
        b = b + jnp.where(pos >= step, pltpu.roll(b, step, axis=1), 0.0)
    a = gates[:H] * LOG2E - b
    amax = a
    for step in [1 << e for e in range(L.bit_length() - 1)]:
        amax = jnp.maximum(amax, jnp.where(pos >= step, pltpu.roll(amax, step, axis=1), NEG_INF))
    a_ref[...] = a
    b_ref[...] = b
    col_ref[...] = jnp.concatenate([b, amax, a, jnp.zeros((LANES - 3 * H, rows), F32)], axis=0).T


def _mlstm_tile(qk_ref, v_ref, mo_ref, ng_ref, out_ref, c_ref, m_ref, a_ref, b_ref, col_ref, *, rows, base):
    L, H, dh = MLSTM_CHUNK, MLSTM_HEADS, MLSTM_HEAD_DIM
    causal = lax.broadcasted_iota(jnp.int32, (L, L), 1) <= lax.broadcasted_iota(jnp.int32, (L, L), 0)
    ones_blk = jnp.ones((L, dh), BF16)
    wide = lambda x: jnp.concatenate([x, x], axis=1)

    def head_chunk(h, r0):
        cols = slice(h * dh, (h + 1) * dh)
        tok = slice(r0, r0 + L)
        q_b = qk_ref[tok, cols]
        k_b = qk_ref[tok, MLSTM_WIDTH + h * dh:MLSTM_WIDTH + (h + 1) * dh]
        v_b = v_ref[tok, cols]

        pos = pl.ds(pl.multiple_of(base + r0, L), L)
        column = lambda j: jnp.broadcast_to(col_ref[pos, j:j + 1], (L, dh))
        b_rep, amax_rep, a_rep = column(h), column(H + h), column(2 * H + h)
        a_row = a_ref[h:h + 1, pos]
        b_last = b_ref[h:h + 1, pos][:, L - 1:L]
        m_prev = m_ref[h, 0:1, :]
        mx = jnp.maximum(m_prev, amax_rep)

        w_intra = jnp.exp2(jnp.where(causal, a_row - wide(mx), NEG_INF))
        qk = lax.dot_general(q_b, k_b, (((1,), (1,)), ((), ())), preferred_element_type=F32) * w_intra
        res = (wide(jnp.exp2(m_prev - mx)) * jnp.dot(q_b, c_ref[h].astype(BF16), preferred_element_type=F32)
               + jnp.dot(qk.astype(BF16), jnp.concatenate([v_b, ones_blk], axis=1), preferred_element_type=F32))
        hid = res[:, :dh] / jnp.maximum(jnp.abs(res[:, dh:]), jnp.exp2(-(b_rep + mx)))

        m_new = b_last + jnp.maximum(m_prev, jnp.max(a_row, axis=1, keepdims=True))
        decay = jnp.exp2(b_last + m_prev - m_new)
        ws = jnp.exp2(b_last + a_rep - m_new)
        wv = jnp.concatenate([(v_b.astype(F32) * ws).astype(BF16), ws.astype(BF16)], axis=1)
        c_ref[h] = wide(decay) * c_ref[h] + lax.dot_general(k_b, wv, (((0,), (0,)), ((), ())),
                                                            preferred_element_type=F32)
        m_ref[h] = jnp.broadcast_to(m_new, m_ref.shape[1:])

        hn = _rms(hid, ng_ref[:, cols])
        out_ref[tok, cols] = (jax.nn.sigmoid(mo_ref[tok, cols].astype(F32)) * hn).astype(BF16)

    for r0 in range(0, rows, L):
        for h in range(H):
            head_chunk(h, r0)


def _mlstm_tail_kernel(x_ref, at_ref, p_ref, gt_ref, qk0_ref, v0_ref, mo0_ref, qk_ref, v_ref, mo_ref,
                       ng_ref, wo_ref, g_mlp_ref, wu_ref, wd_ref, g_ple_ref, wpg_ref, wple_ref, g_fin_ref, o_ref,
                       ml_ref, c_ref, m_ref, a_ref, b_ref, col_ref, *, tm, ff_chunk, final_norm):
    state = (c_ref, m_ref, a_ref, b_ref, col_ref)
    i = pl.program_id(1)

    @pl.when(i == 0)
    def _():
        c_ref[...] = jnp.zeros_like(c_ref)
        m_ref[...] = jnp.zeros_like(m_ref)
        _gate_scans(gt_ref, a_ref, b_ref, col_ref)
        _mlstm_tile(qk0_ref, v0_ref, mo0_ref, ng_ref, ml_ref, *state, rows=tm, base=0)

    ml = ml_ref[...]
    nxt = jnp.minimum(i + 1, pl.num_programs(1) - 1)
    _mlstm_tile(qk_ref, v_ref, mo_ref, ng_ref, ml_ref, *state, rows=tm, base=nxt * tm)

    mix = jnp.concatenate([at_ref[...], ml], axis=1)
    h = x_ref[...] + jnp.dot(mix, wo_ref[...], preferred_element_type=F32)

    u = _rms(h, g_mlp_ref[...]).astype(BF16)
    for c0 in range(0, D_FF, ff_chunk):
        a = jnp.dot(u, wu_ref[:, c0:c0 + ff_chunk], preferred_element_type=F32)
        a = jnp.square(jnp.maximum(a, 0.0)).astype(BF16)
        h = h + jnp.dot(a, wd_ref[c0:c0 + ff_chunk, :], preferred_element_type=F32)

    gate = jax.nn.sigmoid(jnp.dot(_rms(h, g_ple_ref[...]).astype(BF16), wpg_ref[...], preferred_element_type=F32))
    h = h + gate * jnp.dot(p_ref[...].astype(BF16), wple_ref[...], preferred_element_type=F32)
    o_ref[...] = _rms(h, g_fin_ref[...]) if final_norm else h


def _mlstm_tail(x, attn, p, layer, mqk, mv, mo, gates_t, norm_g, w_out, g_mlp, w_up, w_down, g_ple, w_pg, w_ple,
                g_fin, final_norm, tm=512, ff_chunk=1024):
    B, S, D = x.shape
    H, dh, W = MLSTM_HEADS, MLSTM_HEAD_DIM, MLSTM_WIDTH
    n = S // tm
    tok = lambda width: pl.BlockSpec((None, tm, width), lambda b, i: (b, i, 0))
    first = lambda width: pl.BlockSpec((None, tm, width), lambda b, i: (b, 0, 0))
    nxt = lambda width: pl.BlockSpec((None, tm, width), lambda b, i: (b, jnp.minimum(i + 1, n - 1), 0))
    p_spec = pl.BlockSpec((None, None, tm, PLE_DIM), lambda b, i: (layer, b, i, 0))
    gt_spec = pl.BlockSpec((None, 2 * H, S), lambda b, i: (b, 0, 0))
    return pl.pallas_call(
        functools.partial(_mlstm_tail_kernel, tm=tm, ff_chunk=ff_chunk, final_norm=final_norm),
        name="mlstm_tail",
        grid=(B, n),
        in_specs=[tok(D), tok(ATTN_WIDTH), p_spec, gt_spec,
                  first(2 * W), first(W), first(W),
                  nxt(2 * W), nxt(W), nxt(W), _resident(norm_g.shape),
                  _resident(w_out.shape), _resident((1, D)), _resident(w_up.shape), _resident(w_down.shape),
                  _resident((1, D)), _resident(w_pg.shape), _resident(w_ple.shape), _resident((1, D))],
        out_specs=tok(D),
        out_shape=jax.ShapeDtypeStruct((B, S, D), F32),
        scratch_shapes=[pltpu.VMEM((tm, W), BF16),
                        pltpu.VMEM((H, dh, 2 * dh), F32), pltpu.VMEM((H, 8, LANES), F32),
                        pltpu.VMEM((H, S), F32), pltpu.VMEM((H, S), F32), pltpu.VMEM((S, LANES), F32)],
        compiler_params=pltpu.CompilerParams(
            dimension_semantics=("parallel", "arbitrary"), vmem_limit_bytes=VMEM_LIMIT),
    )(x, attn, p, gates_t, mqk, mv, mo, mqk, mv, mo, norm_g,
      w_out, g_mlp, w_up, w_down, g_ple, w_pg, w_ple, g_fin)


def _rope_tables(seq):
    half = ROPE_DIM // 2
    inv_freq = jnp.power(ROPE_THETA, -jnp.arange(half, dtype=F32) / half)
    ang = jnp.arange(seq, dtype=jnp.int32).astype(F32)[:, None] * inv_freq[None, :]
    cos, sin = jnp.cos(ang), jnp.sin(ang)
    rest = ATTN_HEAD_DIM - ROPE_DIM
    cos_head = jnp.concatenate([cos, cos, jnp.ones((seq, rest), F32)], axis=1)
    sin_head = jnp.concatenate([-sin, sin, jnp.zeros((seq, rest), F32)], axis=1)
    reps = LANES // ATTN_HEAD_DIM
    return jnp.tile(cos_head, (1, reps)), jnp.tile(sin_head, (1, reps))


def kernel(x, p, norm_mix_g, w_in, conv_w, conv_b, gate_b, mlstm_norm_g, w_out, norm_mlp_g, w_up, w_down,
           norm_ple_g, w_ple_gate, w_ple, final_norm_g):
    B, S, D = x.shape
    depth = w_in.shape[0]
    cos_t, sin_t = _rope_tables(S)
    row = lambda v: v.reshape(1, -1).astype(F32)
    n_gate = 2 * MLSTM_HEADS
    h = x
    for layer in range(depth):
        w = w_in[layer]
        w_main = jnp.concatenate([w[:, :ATTN_WIDTH] * (LOG2E / math.sqrt(ATTN_HEAD_DIM)),
                                  w[:, ATTN_WIDTH:MAIN_WIDTH]], axis=1).astype(BF16)
        w_gate = jnp.pad(w[:, MAIN_WIDTH:], ((0, 0), (0, LANES - n_gate))).astype(BF16)
        gb = jnp.pad(gate_b[layer].astype(F32), (0, LANES - n_gate)).reshape(1, LANES)
        q, k, v, mqk, mv, mo, gates_t = _in_proj(h, row(norm_mix_g[layer]), w_main, w_gate, gb, cos_t, sin_t,
                                                 conv_w[layer].astype(F32), row(conv_b[layer]))
        attn = _dilated_attention(q, k, v)
        h = _mlstm_tail(h, attn, p, layer, mqk, mv, mo, gates_t, row(mlstm_norm_g[layer]),
                        w_out[layer].astype(BF16), row(norm_mlp_g[layer]),
                        w_up[layer].astype(BF16), w_down[layer].astype(BF16), row(norm_ple_g[layer]),
                        w_ple_gate[layer].astype(BF16), w_ple[layer].astype(BF16), row(final_norm_g),
                        final_norm=(layer == depth - 1))
    return h
```

```python
import functools
import math

import jax
import jax.numpy as jnp
from jax import lax
from jax.experimental import pallas as pl
from jax.experimental.pallas import tpu as pltpu

D_MODEL = 1024
ATTN_HEADS = 8
ATTN_HEAD_DIM = 64
ATTN_WIDTH = ATTN_HEADS * ATTN_HEAD_DIM
MLSTM_HEADS = 4
MLSTM_HEAD_DIM = 128
MLSTM_WIDTH = MLSTM_HEADS * MLSTM_HEAD_DIM
DILATIONS = (1, 4, 16)
ATTN_BLOCK = 128
ATTN_UNROLL = 32
ROPE_THETA = 500000.0
ROPE_DIM = ATTN_HEAD_DIM // 4
CONV_WIDTH = 4
D_FF = 4 * D_MODEL
PLE_DIM = 256
RMS_EPS = 1e-6

LANES = 128
HEAD_PAIRS = ATTN_WIDTH // LANES
MAIN_WIDTH = 3 * ATTN_WIDTH + 4 * MLSTM_WIDTH
MLSTM_CHUNK = 256
CONV_HALO = 8
VMEM_LIMIT = 56 * 1024 * 1024

F32 = jnp.float32
BF16 = jnp.bfloat16
NEG_INF = float("-inf")
LOG2E = 1.4426950408889634


def _rms(x, g):
    return x * lax.rsqrt(jnp.mean(x * x, axis=-1, keepdims=True) + RMS_EPS) * g


def _resident(shape):
    return pl.BlockSpec(shape, lambda *_: (0,) * len(shape), pipeline_mode=pl.Buffered(1))


def _in_proj_kernel(x_ref, g_ref, w_ref, wg_ref, gb_ref, cos_ref, sin_ref, cw_ref, cb_ref,
                    q_ref, k_ref, v_ref, mqk_ref, mv_ref, mo_ref, gt_ref, tail_ref, *, tm):
    @pl.when(pl.program_id(1) == 0)
    def _():
        tail_ref[:CONV_HALO, :] = jnp.zeros((CONV_HALO, tail_ref.shape[1]), F32)

    u = _rms(x_ref[...], g_ref[...]).astype(BF16)
    pos0 = pl.multiple_of(pl.program_id(1) * tm, tm)
    cos = cos_ref[pl.ds(pos0, tm), :]
    sin = sin_ref[pl.ds(pos0, tm), :]
    lane = lax.broadcasted_iota(jnp.int32, (tm, LANES), 1)
    first_half = (lane % ATTN_HEAD_DIM) < (ROPE_DIM // 2)

    def proj(c0, width):
        return jnp.dot(u, w_ref[:, c0:c0 + width], preferred_element_type=F32)

    def rope_store(y, o_ref):
        for j in range(HEAD_PAIRS):
            yj = y[:, j * LANES:(j + 1) * LANES]
            partner = jnp.where(first_half,
                                pltpu.roll(yj, LANES - ROPE_DIM // 2, axis=1),
                                pltpu.roll(yj, ROPE_DIM // 2, axis=1))
            o_ref[:, j * LANES:(j + 1) * LANES] = (yj * cos + partner * sin).astype(BF16)

    A, M = ATTN_WIDTH, MLSTM_WIDTH
    tail_ref[CONV_HALO:, :] = proj(3 * A, 2 * M)
    k_scale = 1.0 / math.sqrt(MLSTM_HEAD_DIM)
    rb = 128
    for c0 in range(0, 2 * M, LANES):
        cols = slice(c0, c0 + LANES)
        for r0 in range(0, tm, rb):
            acc = cb_ref[:, cols]
            for j in range(CONV_WIDTH):
                start = CONV_HALO + r0 - j
                acc = acc + cw_ref[CONV_WIDTH - 1 - j:CONV_WIDTH - j, cols] * tail_ref[start:start + rb, cols]
            act = acc * jax.nn.sigmoid(acc)
            mqk_ref[r0:r0 + rb, cols] = (act if c0 < M else act * k_scale).astype(BF16)
    tail_ref[:CONV_HALO, :] = tail_ref[tm:tm + CONV_HALO, :]
    rope_store(proj(0, A), q_ref)
    rope_store(proj(A, A), k_ref)
    v_ref[...] = proj(2 * A, A).astype(BF16)
    mv_ref[...] = proj(3 * A + 2 * M, M).astype(BF16)
    mo_ref[...] = proj(3 * A + 3 * M, M).astype(BF16)
    gates = jnp.dot(u, wg_ref[...], preferred_element_type=F32) + gb_ref[...]
    gt_ref[...] = gates.T[:2 * MLSTM_HEADS, :]


def _in_proj(x, g, w_main, w_gate, gate_b, cos_t, sin_t, conv_w, conv_b, tm=512):
    B, S, D = x.shape
    grid = (B, S // tm)
    tok = lambda width: pl.BlockSpec((None, tm, width), lambda b, i: (b, i, 0))
    attn_shape = jax.ShapeDtypeStruct((B, S, ATTN_WIDTH), BF16)
    return pl.pallas_call(
        functools.partial(_in_proj_kernel, tm=tm),
        name="in_proj",
        grid=grid,
        in_specs=[tok(D), _resident((1, D)), _resident(w_main.shape), _resident(w_gate.shape),
                  _resident((1, LANES)), _resident(cos_t.shape), _resident(sin_t.shape),
                  _resident(conv_w.shape), _resident(conv_b.shape)],
        out_specs=[tok(ATTN_WIDTH), tok(ATTN_WIDTH), tok(ATTN_WIDTH),
                   tok(2 * MLSTM_WIDTH), tok(MLSTM_WIDTH), tok(MLSTM_WIDTH),
                   pl.BlockSpec((None, 2 * MLSTM_HEADS, tm), lambda b, i: (b, 0, i))],
        out_shape=[attn_shape, attn_shape, attn_shape,
                   jax.ShapeDtypeStruct((B, S, 2 * MLSTM_WIDTH), BF16),
                   jax.ShapeDtypeStruct((B, S, MLSTM_WIDTH), BF16),
                   jax.ShapeDtypeStruct((B, S, MLSTM_WIDTH), BF16),
                   jax.ShapeDtypeStruct((B, 2 * MLSTM_HEADS, S), F32)],
        scratch_shapes=[pltpu.VMEM((CONV_HALO + tm, 2 * MLSTM_WIDTH), F32)],
        compiler_params=pltpu.CompilerParams(
            dimension_semantics=("parallel", "arbitrary"), vmem_limit_bytes=VMEM_LIMIT),
    )(x, g, w_main, w_gate, gate_b, cos_t, sin_t, conv_w, conv_b)


def _attn_kernel(q_ref, k_ref, v_ref, o_ref, stage, s4, q4, k4, v4, q16, k16, v16,
                 acc1, acc4, acc16, sm1, sm4, sm16, sd1, sd4, sd16, bias_ref, *, seq):
    blk = ATTN_BLOCK
    d4, d16 = DILATIONS[1], DILATIONS[2]
    assert DILATIONS[0] == 1 and d16 == d4 * d4
    qi = lax.broadcasted_iota(jnp.int32, (2 * blk, 2 * blk), 0) % blk
    ki = lax.broadcasted_iota(jnp.int32, (2 * blk, 2 * blk), 1)
    dist = qi + blk - ki
    valid = (dist >= 0) & (dist <= blk)
    bias_ref[1] = jnp.where(valid, 0.0, NEG_INF)
    bias_ref[0] = jnp.where(valid & (ki >= blk), 0.0, NEG_INF)

    head0 = lax.broadcasted_iota(jnp.int32, (blk, LANES), 1) < ATTN_HEAD_DIM
    ones_kv = jnp.ones((2 * blk, LANES), BF16)

    conv_rows = 512
    for x_ref, x4, x16 in ((q_ref, q4, q16), (k_ref, k4, k16), (v_ref, v4, v16)):
        def to_f32(c, carry, x_ref=x_ref):
            rows = pl.ds(pl.multiple_of(c * conv_rows, conv_rows), conv_rows)
            stage[rows, :] = x_ref[rows, :].astype(F32)
            return carry
        lax.fori_loop(0, seq // conv_rows, to_f32, 0)

        def by4(c, carry, x4=x4):
            l0 = pl.multiple_of(c * blk, blk)
            for r in range(d4):
                rows = stage[pl.ds(l0 * d4 + r, blk, stride=d4), :]
                s4[r, pl.ds(l0, blk), :] = rows
                x4[pl.ds(l0, blk), r * LANES:(r + 1) * LANES] = rows.astype(BF16)
            return carry
        lax.fori_loop(0, seq // d4 // blk, by4, 0)

        def by16(c, carry, x16=x16):
            l0 = pl.multiple_of(c * 64, 64)
            for r in range(d16):
                x16[pl.ds(l0, 64), r * LANES:(r + 1) * LANES] = (
                    s4[r % d4, pl.ds(l0 * d4 + r // d4, 64, stride=d4), :].astype(BF16))
            return carry
        lax.fori_loop(0, seq // d16 // 64, by16, 0)

    patterns = ((q_ref, k_ref, v_ref, acc1, sm1, sd1, 1),
                (q4, k4, v4, acc4, sm4, sd4, d4),
                (q16, k16, v16, acc16, sm16, sd16, d16))

    def result_rows(ref, row0, r, d):
        if d == 1:
            return ref.at[pl.ds(row0, blk), :]
        if d == d4:
            return ref.at[r, pl.ds(row0, blk), :]
        return ref.at[r % d4, pl.ds(row0 * d4 + r // d4, blk, stride=d4), :]

    def view_lanes(r, d):
        return slice(0, LANES) if d == 1 else pl.ds(pl.multiple_of(r * LANES, LANES), LANES)

    for (qv, kv, vv_ref, acc_ref, sm_ref, sd_ref, d) in patterns:
        nb = seq // d // blk

        def block(i, carry, qv=qv, kv=kv, vv_ref=vv_ref, acc_ref=acc_ref, sm_ref=sm_ref, sd_ref=sd_ref, d=d, nb=nb):
            r = i // nb
            n = i % nb
            row0 = pl.multiple_of(n * blk, blk)
            prow0 = pl.multiple_of(jnp.maximum(n - 1, 0) * blk, blk)
            lanes = view_lanes(r, d)
            q = qv[pl.ds(row0, blk), lanes]
            zero = jnp.zeros_like(q)
            q2 = jnp.concatenate([jnp.where(head0, q, zero), jnp.where(head0, zero, q)], axis=0)
            kk = jnp.concatenate([kv[pl.ds(prow0, blk), lanes], kv[pl.ds(row0, blk), lanes]], axis=0)
            vv = jnp.concatenate([vv_ref[pl.ds(prow0, blk), lanes], vv_ref[pl.ds(row0, blk), lanes]], axis=0)
            s = lax.dot_general(q2, kk, (((1,), (1,)), ((), ())), preferred_element_type=F32)
            s = s + bias_ref[jnp.minimum(n, 1)]
            m = jnp.max(s, axis=1, keepdims=True)
            p = jnp.exp2(s - m).astype(BF16)
            res = jnp.dot(p, jnp.concatenate([vv, ones_kv], axis=1), preferred_element_type=F32)
            result_rows(acc_ref, row0, r, d)[...] = jnp.where(head0, res[:blk, :LANES], res[blk:, :LANES])
            result_rows(sm_ref, row0, r, d)[...] = jnp.where(head0, m[:blk], m[blk:])
            result_rows(sd_ref, row0, r, d)[...] = jnp.where(head0, res[:blk, LANES:], res[blk:, LANES:])
            return carry

        lax.fori_loop(0, d * nb, block, 0, unroll=ATTN_UNROLL)

    rows_m = 256

    def mix(c, carry):
        l0 = pl.multiple_of(c * rows_m, rows_m)
        com = pl.ds(l0, rows_m)
        for r in range(d4):
            tok = pl.ds(l0 * d4 + r, rows_m, stride=d4)
            m1, m4, m16 = sm1[tok, :], sm4[r, com, :], sm16[r, com, :]
            top = jnp.maximum(jnp.maximum(m1, m4), m16)
            e1, e4, e16 = jnp.exp2(m1 - top), jnp.exp2(m4 - top), jnp.exp2(m16 - top)
            den = e1 * sd1[tok, :] + e4 * sd4[r, com, :] + e16 * sd16[r, com, :]
            num = e1 * acc1[tok, :] + e4 * acc4[r, com, :] + e16 * acc16[r, com, :]
            stage[tok, :] = num * (1.0 / den)
        return carry

    lax.fori_loop(0, seq // d4 // rows_m, mix, 0)

    def emit(c, carry):
        rows = pl.ds(pl.multiple_of(c * conv_rows, conv_rows), conv_rows)
        o_ref[rows, :] = stage[rows, :].astype(BF16)
        return carry

    lax.fori_loop(0, seq // conv_rows, emit, 0)


def _dilated_attention(q, k, v):
    B, S, W = q.shape
    d4 = DILATIONS[1]
    spec = pl.BlockSpec((None, S, LANES), lambda b, h: (b, 0, h))
    token = pltpu.VMEM((S, LANES), F32)
    by4 = pltpu.VMEM((d4, S // d4, LANES), F32)
    scratch = ([token, by4]
               + [pltpu.VMEM((S // d, d * LANES), BF16) for d in DILATIONS[1:] for _ in range(3)]
               + [token, by4, by4] * 3
               + [pltpu.VMEM((2, 2 * ATTN_BLOCK, 2 * ATTN_BLOCK), F32)])
    return pl.pallas_call(
        functools.partial(_attn_kernel, seq=S),
        name="dilated_attention",
        grid=(B, W // LANES),
        in_specs=[spec, spec, spec],
        out_specs=spec,
        out_shape=jax.ShapeDtypeStruct((B, S, W), BF16),
        scratch_shapes=scratch,
        compiler_params=pltpu.CompilerParams(
            dimension_semantics=("parallel", "parallel"), vmem_limit_bytes=VMEM_LIMIT),
    )(q, k, v)


def _gate_scans(gt_ref, a_ref, b_ref, col_ref):
    L, H = MLSTM_CHUNK, MLSTM_HEADS
    rows = gt_ref.shape[1]
    gates = gt_ref[...]
    f_pre = gates[H:]
    b = (jnp.minimum(f_pre, 0.0) - jnp.log(1.0 + jnp.exp(-jnp.abs(f_pre)))) * LOG2E
    pos = lax.broadcasted_iota(jnp.int32, (H, rows), 1) % L
    for step in [1 << e for e in range(L.bit_length() - 1)]:
        b = b + jnp.where(pos >= step, pltpu.roll(b, step, axis=1), 0.0)
    a = gates[:H] * LOG2E - b
    amax = a
    for step in [1 << e for e in range(L.bit_length() - 1)]:
        amax = jnp.maximum(amax, jnp.where(pos >= step, pltpu.roll(amax, step, axis=1), NEG_INF))
    a_ref[...] = a
    b_ref[...] = b
    col_ref[...] = jnp.concatenate([b, amax, a, jnp.zeros((LANES - 3 * H, rows), F32)], axis=0).T


def _mlstm_tile(qk_ref, v_ref, mo_ref, ng_ref, out_ref, c_ref, m_ref, a_ref, b_ref, col_ref, *, rows, base):
    L, H, dh = MLSTM_CHUNK, MLSTM_HEADS, MLSTM_HEAD_DIM
    causal = lax.broadcasted_iota(jnp.int32, (L, L), 1) <= lax.broadcasted_iota(jnp.int32, (L, L), 0)
    ones_blk = jnp.ones((L, dh), BF16)
    wide = lambda x: jnp.concatenate([x, x], axis=1)

    def head_chunk(h, r0):
        cols = slice(h * dh, (h + 1) * dh)
        tok = slice(r0, r0 + L)
        q_b = qk_ref[tok, cols]
        k_b = qk_ref[tok, MLSTM_WIDTH + h * dh:MLSTM_WIDTH + (h + 1) * dh]
        v_b = v_ref[tok, cols]

        pos = pl.ds(pl.multiple_of(base + r0, L), L)
        column = lambda j: jnp.broadcast_to(col_ref[pos, j:j + 1], (L, dh))
        b_rep, amax_rep, a_rep = column(h), column(H + h), column(2 * H + h)
        a_row = a_ref[h:h + 1, pos]
        b_last = b_ref[h:h + 1, pos][:, L - 1:L]
        m_prev = m_ref[h, 0:1, :]
        mx = jnp.maximum(m_prev, amax_rep)

        w_intra = jnp.exp2(jnp.where(causal, a_row - wide(mx), NEG_INF))
        qk = lax.dot_general(q_b, k_b, (((1,), (1,)), ((), ())), preferred_element_type=F32) * w_intra
        res = (wide(jnp.exp2(m_prev - mx)) * jnp.dot(q_b, c_ref[h].astype(BF16), preferred_element_type=F32)
               + jnp.dot(qk.astype(BF16), jnp.concatenate([v_b, ones_blk], axis=1), preferred_element_type=F32))
        hid = res[:, :dh] / jnp.maximum(jnp.abs(res[:, dh:]), jnp.exp2(-(b_rep + mx)))

        m_new = b_last + jnp.maximum(m_prev, jnp.max(a_row, axis=1, keepdims=True))
        decay = jnp.exp2(b_last + m_prev - m_new)
        ws = jnp.exp2(b_last + a_rep - m_new)
        wv = jnp.concatenate([(v_b.astype(F32) * ws).astype(BF16), ws.astype(BF16)], axis=1)
        c_ref[h] = wide(decay) * c_ref[h] + lax.dot_general(k_b, wv, (((0,), (0,)), ((), ())),
                                                            preferred_element_type=F32)
        m_ref[h] = jnp.broadcast_to(m_new, m_ref.shape[1:])

        hn = _rms(hid, ng_ref[:, cols])
        out_ref[tok, cols] = (jax.nn.sigmoid(mo_ref[tok, cols].astype(F32)) * hn).astype(BF16)

    for r0 in range(0, rows, L):
        for h in range(H):
            head_chunk(h, r0)


def _mlstm_tail_kernel(x_ref, at_ref, p_ref, gt_ref, qk0_ref, v0_ref, mo0_ref, qk_ref, v_ref, mo_ref,
                       ng_ref, wo_ref, g_mlp_ref, wu_ref, wd_ref, g_ple_ref, wpg_ref, wple_ref, g_fin_ref, o_ref,
                       ml_ref, c_ref, m_ref, a_ref, b_ref, col_ref, *, tm, ff_chunk, final_norm):
    state = (c_ref, m_ref, a_ref, b_ref, col_ref)
    i = pl.program_id(1)

    @pl.when(i == 0)
    def _():
        c_ref[...] = jnp.zeros_like(c_ref)
        m_ref[...] = jnp.zeros_like(m_ref)
        _gate_scans(gt_ref, a_ref, b_ref, col_ref)
        _mlstm_tile(qk0_ref, v0_ref, mo0_ref, ng_ref, ml_ref, *state, rows=tm, base=0)

    ml = ml_ref[...]
    nxt = jnp.minimum(i + 1, pl.num_programs(1) - 1)
    _mlstm_tile(qk_ref, v_ref, mo_ref, ng_ref, ml_ref, *state, rows=tm, base=nxt * tm)

    mix = jnp.concatenate([at_ref[...], ml], axis=1)
    h = x_ref[...] + jnp.dot(mix, wo_ref[...], preferred_element_type=F32)

    u = _rms(h, g_mlp_ref[...]).astype(BF16)
    for c0 in range(0, D_FF, ff_chunk):
        a = jnp.dot(u, wu_ref[:, c0:c0 + ff_chunk], preferred_element_type=F32)
        a = jnp.square(jnp.maximum(a, 0.0)).astype(BF16)
        h = h + jnp.dot(a, wd_ref[c0:c0 + ff_chunk, :], preferred_element_type=F32)

    gate = jax.nn.sigmoid(jnp.dot(_rms(h, g_ple_ref[...]).astype(BF16), wpg_ref[...], preferred_element_type=F32))
    h = h + gate * jnp.dot(p_ref[...].astype(BF16), wple_ref[...], preferred_element_type=F32)
    o_ref[...] = _rms(h, g_fin_ref[...]) if final_norm else h


def _mlstm_tail(x, attn, p, layer, mqk, mv, mo, gates_t, norm_g, w_out, g_mlp, w_up, w_down, g_ple, w_pg, w_ple,
                g_fin, final_norm, tm=512, ff_chunk=1024):
    B, S, D = x.shape
    H, dh, W = MLSTM_HEADS, MLSTM_HEAD_DIM, MLSTM_WIDTH
    n = S // tm
    tok = lambda width: pl.BlockSpec((None, tm, width), lambda b, i: (b, i, 0))
    first = lambda width: pl.BlockSpec((None, tm, width), lambda b, i: (b, 0, 0))
    nxt = lambda width: pl.BlockSpec((None, tm, width), lambda b, i: (b, jnp.minimum(i + 1, n - 1), 0))
    p_spec = pl.BlockSpec((None, None, tm, PLE_DIM), lambda b, i: (layer, b, i, 0))
    gt_spec = pl.BlockSpec((None, 2 * H, S), lambda b, i: (b, 0, 0))
    return pl.pallas_call(
        functools.partial(_mlstm_tail_kernel, tm=tm, ff_chunk=ff_chunk, final_norm=final_norm),
        name="mlstm_tail",
        grid=(B, n),
        in_specs=[tok(D), tok(ATTN_WIDTH), p_spec, gt_spec,
                  first(2 * W), first(W), first(W),
                  nxt(2 * W), nxt(W), nxt(W), _resident(norm_g.shape),
                  _resident(w_out.shape), _resident((1, D)), _resident(w_up.shape), _resident(w_down.shape),
                  _resident((1, D)), _resident(w_pg.shape), _resident(w_ple.shape), _resident((1, D))],
        out_specs=tok(D),
        out_shape=jax.ShapeDtypeStruct((B, S, D), F32),
        scratch_shapes=[pltpu.VMEM((tm, W), BF16),
                        pltpu.VMEM((H, dh, 2 * dh), F32), pltpu.VMEM((H, 8, LANES), F32),
                        pltpu.VMEM((H, S), F32), pltpu.VMEM((H, S), F32), pltpu.VMEM((S, LANES), F32)],
        compiler_params=pltpu.CompilerParams(
            dimension_semantics=("parallel", "arbitrary"), vmem_limit_bytes=VMEM_LIMIT),
    )(x, attn, p, gates_t, mqk, mv, mo, mqk, mv, mo, norm_g,
      w_out, g_mlp, w_up, w_down, g_ple, w_pg, w_ple, g_fin)


def _rope_tables(seq):
    half = ROPE_DIM // 2
    inv_freq = jnp.power(ROPE_THETA, -jnp.arange(half, dtype=F32) / half)
    ang = jnp.arange(seq, dtype=jnp.int32).astype(F32)[:, None] * inv_freq[None, :]
    cos, sin = jnp.cos(ang), jnp.sin(ang)
    rest = ATTN_HEAD_DIM - ROPE_DIM
    cos_head = jnp.concatenate([cos, cos, jnp.ones((seq, rest), F32)], axis=1)
    sin_head = jnp.concatenate([-sin, sin, jnp.zeros((seq, rest), F32)], axis=1)
    reps = LANES // ATTN_HEAD_DIM
    return jnp.tile(cos_head, (1, reps)), jnp.tile(sin_head, (1, reps))


def kernel(x, p, norm_mix_g, w_in, conv_w, conv_b, gate_b, mlstm_norm_g, w_out, norm_mlp_g, w_up, w_down,
           norm_ple_g, w_ple_gate, w_ple, final_norm_g):
    B, S, D = x.shape
    depth = w_in.shape[0]
    cos_t, sin_t = _rope_tables(S)
    row = lambda v: v.reshape(1, -1).astype(F32)
    n_gate = 2 * MLSTM_HEADS
    h = x
    for layer in range(depth):
        w = w_in[layer]
        w_main = jnp.concatenate([w[:, :ATTN_WIDTH] * (LOG2E / math.sqrt(ATTN_HEAD_DIM)),
                                  w[:, ATTN_WIDTH:MAIN_WIDTH]], axis=1).astype(BF16)
        w_gate = jnp.pad(w[:, MAIN_WIDTH:], ((0, 0), (0, LANES - n_gate))).astype(BF16)
        gb = jnp.pad(gate_b[layer].astype(F32), (0, LANES - n_gate)).reshape(1, LANES)
        q, k, v, mqk, mv, mo, gates_t = _in_proj(h, row(norm_mix_g[layer]), w_main, w_gate, gb, cos_t, sin_t,
                                                 conv_w[layer].astype(F32), row(conv_b[layer]))
        attn = _dilated_attention(q, k, v)
        h = _mlstm_tail(h, attn, p, layer, mqk, mv, mo, gates_t, row(mlstm_norm_g[layer]),
                        w_out[layer].astype(BF16), row(norm_mlp_g[layer]),
                        w_up[layer].astype(BF16), w_down[layer].astype(BF16), row(norm_ple_g[layer]),
                        w_ple_gate[layer].astype(BF16), w_ple[layer].astype(BF16), row(final_norm_g),
                        final_norm=(layer == depth - 1))
    return h
```

```python
import functools
import math

import jax
import jax.numpy as jnp
from jax import lax
from jax.experimental import pallas as pl
from jax.experimental.pallas import tpu as pltpu

D_MODEL = 1024
ATTN_HEADS = 8
ATTN_HEAD_DIM = 64
ATTN_WIDTH = ATTN_HEADS * ATTN_HEAD_DIM
MLSTM_HEADS = 4
MLSTM_HEAD_DIM = 128
MLSTM_WIDTH = MLSTM_HEADS * MLSTM_HEAD_DIM
DILATIONS = (1, 4, 16)
ATTN_BLOCK = 128
ATTN_UNROLL = 32
ROPE_THETA = 500000.0
ROPE_DIM = ATTN_HEAD_DIM // 4
CONV_WIDTH = 4
D_FF = 4 * D_MODEL
PLE_DIM = 256
RMS_EPS = 1e-6

LANES = 128
HEAD_PAIRS = ATTN_WIDTH // LANES
MAIN_WIDTH = 3 * ATTN_WIDTH + 4 * MLSTM_WIDTH
MLSTM_CHUNK = 256
CONV_HALO = 8
VMEM_LIMIT = 56 * 1024 * 1024

F32 = jnp.float32
BF16 = jnp.bfloat16
NEG_INF = float("-inf")
LOG2E = 1.4426950408889634


def _rms(x, g):
    return x * lax.rsqrt(jnp.mean(x * x, axis=-1, keepdims=True) + RMS_EPS) * g


def _resident(shape):
    return pl.BlockSpec(shape, lambda *_: (0,) * len(shape), pipeline_mode=pl.Buffered(1))


def _in_proj_kernel(x_ref, g_ref, w32_ref, gb_ref, cos_ref, sin_ref, cw_ref, cb_ref,
                    q_ref, k_ref, v_ref, mqk_ref, mv_ref, mo_ref, gt_ref, tail_ref, w_ref, wg_ref, *, tm, sub):
    @pl.when((pl.program_id(0) == 0) & (pl.program_id(1) == 0))
    def _():
        gain = g_ref[...]
        q_scale = LOG2E / math.sqrt(ATTN_HEAD_DIM)
        for c0 in range(0, MAIN_WIDTH, ATTN_WIDTH):
            w = w32_ref[:, c0:c0 + ATTN_WIDTH] * gain
            w_ref[:, c0:c0 + ATTN_WIDTH] = (w * q_scale if c0 == 0 else w).astype(BF16)
        n_gate = 2 * MLSTM_HEADS
        wg_ref[...] = jnp.zeros_like(wg_ref)
        wg_ref[:, :n_gate] = (w32_ref[:, MAIN_WIDTH:MAIN_WIDTH + n_gate] * gain).astype(BF16)

    @pl.when(pl.program_id(1) == 0)
    def _():
        tail_ref[:CONV_HALO, :] = jnp.zeros((CONV_HALO, tail_ref.shape[1]), F32)

    A, M = ATTN_WIDTH, MLSTM_WIDTH
    lane = lax.broadcasted_iota(jnp.int32, (sub, LANES), 1)
    first_half = (lane % ATTN_HEAD_DIM) < (ROPE_DIM // 2)
    k_scale = 1.0 / math.sqrt(MLSTM_HEAD_DIM)
    rb = 128

    for s0 in range(0, tm, sub):
        rows = slice(s0, s0 + sub)
        u = _rms(x_ref[rows, :], 1.0).astype(BF16)
        pos0 = pl.multiple_of(pl.program_id(1) * tm + s0, sub)
        cos = cos_ref[pl.ds(pos0, sub), :]
        sin = sin_ref[pl.ds(pos0, sub), :]

        def proj(c0, width, u=u):
            return jnp.dot(u, w_ref[:, c0:c0 + width], preferred_element_type=F32)

        def rope_store(y, o_ref, cos=cos, sin=sin, rows=rows):
            for j in range(HEAD_PAIRS):
                yj = y[:, j * LANES:(j + 1) * LANES]
                partner = jnp.where(first_half,
                                    pltpu.roll(yj, LANES - ROPE_DIM // 2, axis=1),
                                    pltpu.roll(yj, ROPE_DIM // 2, axis=1))
                o_ref[rows, j * LANES:(j + 1) * LANES] = (yj * cos + partner * sin).astype(BF16)

        tail_ref[CONV_HALO + s0:CONV_HALO + s0 + sub, :] = proj(3 * A, 2 * M)
        for c0 in range(0, 2 * M, LANES):
            cols = slice(c0, c0 + LANES)
            for r0 in range(s0, s0 + sub, rb):
                acc = cb_ref[:, cols]
                for j in range(CONV_WIDTH):
                    start = CONV_HALO + r0 - j
                    acc = acc + cw_ref[CONV_WIDTH - 1 - j:CONV_WIDTH - j, cols] * tail_ref[start:start + rb, cols]
                act = acc * jax.nn.sigmoid(acc)
                mqk_ref[r0:r0 + rb, cols] = (act if c0 < M else act * k_scale).astype(BF16)
        rope_store(proj(0, A), q_ref)
        rope_store(proj(A, A), k_ref)
        v_ref[rows, :] = proj(2 * A, A).astype(BF16)
        mv_ref[rows, :] = proj(3 * A + 2 * M, M).astype(BF16)
        mo_ref[rows, :] = proj(3 * A + 3 * M, M).astype(BF16)
        gates = jnp.dot(u, wg_ref[...], preferred_element_type=F32) + gb_ref[...]
        gt_ref[:, rows] = gates.T[:2 * MLSTM_HEADS, :]
    tail_ref[:CONV_HALO, :] = tail_ref[tm:tm + CONV_HALO, :]


def _in_proj(x, g_col, w_in, gate_b, cos_t, sin_t, conv_w, conv_b, tm=1024, sub=256):
    B, S, D = x.shape
    grid = (B, S // tm)
    tok = lambda width: pl.BlockSpec((None, tm, width), lambda b, i: (b, i, 0))
    attn_shape = jax.ShapeDtypeStruct((B, S, ATTN_WIDTH), BF16)
    return pl.pallas_call(
        functools.partial(_in_proj_kernel, tm=tm, sub=sub),
        name="in_proj",
        grid=grid,
        in_specs=[tok(D), _resident((D, 1)), _resident(w_in.shape),
                  _resident((1, LANES)), _resident(cos_t.shape), _resident(sin_t.shape),
                  _resident(conv_w.shape), _resident(conv_b.shape)],
        out_specs=[tok(ATTN_WIDTH), tok(ATTN_WIDTH), tok(ATTN_WIDTH),
                   tok(2 * MLSTM_WIDTH), tok(MLSTM_WIDTH), tok(MLSTM_WIDTH),
                   pl.BlockSpec((None, 2 * MLSTM_HEADS, tm), lambda b, i: (b, 0, i))],
        out_shape=[attn_shape, attn_shape, attn_shape,
                   jax.ShapeDtypeStruct((B, S, 2 * MLSTM_WIDTH), BF16),
                   jax.ShapeDtypeStruct((B, S, MLSTM_WIDTH), BF16),
                   jax.ShapeDtypeStruct((B, S, MLSTM_WIDTH), BF16),
                   jax.ShapeDtypeStruct((B, 2 * MLSTM_HEADS, S), F32)],
        scratch_shapes=[pltpu.VMEM((CONV_HALO + tm, 2 * MLSTM_WIDTH), F32),
                        pltpu.VMEM((D, MAIN_WIDTH), BF16), pltpu.VMEM((D, LANES), BF16)],
        compiler_params=pltpu.CompilerParams(
            dimension_semantics=("arbitrary", "arbitrary"), vmem_limit_bytes=VMEM_LIMIT),
    )(x, g_col, w_in, gate_b, cos_t, sin_t, conv_w, conv_b)


def _attn_kernel(q_ref, k_ref, v_ref, o_ref, stage, s4, q4, k4, v4, q16, k16, v16,
                 acc1, acc4, acc16, sm1, sm4, sm16, sd1, sd4, sd16, bias_ref, *, seq):
    blk = ATTN_BLOCK
    d4, d16 = DILATIONS[1], DILATIONS[2]
    assert DILATIONS[0] == 1 and d16 == d4 * d4
    qi = lax.broadcasted_iota(jnp.int32, (2 * blk, 2 * blk), 0) % blk
    ki = lax.broadcasted_iota(jnp.int32, (2 * blk, 2 * blk), 1)
    dist = qi + blk - ki
    valid = (dist >= 0) & (dist <= blk)
    bias_ref[1] = jnp.where(valid, 0.0, NEG_INF)
    bias_ref[0] = jnp.where(valid & (ki >= blk), 0.0, NEG_INF)

    head0 = lax.broadcasted_iota(jnp.int32, (blk, LANES), 1) < ATTN_HEAD_DIM
    ones_kv = jnp.ones((2 * blk, LANES), BF16)

    conv_rows = 512
    for x_ref, x4, x16 in ((q_ref, q4, q16), (k_ref, k4, k16), (v_ref, v4, v16)):
        def to_f32(c, carry, x_ref=x_ref):
            rows = pl.ds(pl.multiple_of(c * conv_rows, conv_rows), conv_rows)
            stage[rows, :] = x_ref[rows, :].astype(F32)
            return carry
        lax.fori_loop(0, seq // conv_rows, to_f32, 0)

        def by4(c, carry, x4=x4):
            l0 = pl.multiple_of(c * blk, blk)
            for r in range(d4):
                rows = stage[pl.ds(l0 * d4 + r, blk, stride=d4), :]
                s4[r, pl.ds(l0, blk), :] = rows
                x4[pl.ds(l0, blk), r * LANES:(r + 1) * LANES] = rows.astype(BF16)
            return carry
        lax.fori_loop(0, seq // d4 // blk, by4, 0)

        def by16(c, carry, x16=x16):
            l0 = pl.multiple_of(c * 64, 64)
            for r in range(d16):
                x16[pl.ds(l0, 64), r * LANES:(r + 1) * LANES] = (
                    s4[r % d4, pl.ds(l0 * d4 + r // d4, 64, stride=d4), :].astype(BF16))
            return carry
        lax.fori_loop(0, seq // d16 // 64, by16, 0)

    patterns = ((q_ref, k_ref, v_ref, acc1, sm1, sd1, 1),
                (q4, k4, v4, acc4, sm4, sd4, d4),
                (q16, k16, v16, acc16, sm16, sd16, d16))

    def result_rows(ref, row0, r, d):
        if d == 1:
            return ref.at[pl.ds(row0, blk), :]
        if d == d4:
            return ref.at[r, pl.ds(row0, blk), :]
        return ref.at[r % d4, pl.ds(row0 * d4 + r // d4, blk, stride=d4), :]

    def view_lanes(r, d):
        return slice(0, LANES) if d == 1 else pl.ds(pl.multiple_of(r * LANES, LANES), LANES)

    for (qv, kv, vv_ref, acc_ref, sm_ref, sd_ref, d) in patterns:
        nb = seq // d // blk

        def block(i, carry, qv=qv, kv=kv, vv_ref=vv_ref, acc_ref=acc_ref, sm_ref=sm_ref, sd_ref=sd_ref, d=d, nb=nb):
            r = i // nb
            n = i % nb
            row0 = pl.multiple_of(n * blk, blk)
            prow0 = pl.multiple_of(jnp.maximum(n - 1, 0) * blk, blk)
            lanes = view_lanes(r, d)
            q = qv[pl.ds(row0, blk), lanes]
            zero = jnp.zeros_like(q)
            q2 = jnp.concatenate([jnp.where(head0, q, zero), jnp.where(head0, zero, q)], axis=0)
            kk = jnp.concatenate([kv[pl.ds(prow0, blk), lanes], kv[pl.ds(row0, blk), lanes]], axis=0)
            vv = jnp.concatenate([vv_ref[pl.ds(prow0, blk), lanes], vv_ref[pl.ds(row0, blk), lanes]], axis=0)
            s = lax.dot_general(q2, kk, (((1,), (1,)), ((), ())), preferred_element_type=F32)
            s = s + bias_ref[jnp.minimum(n, 1)]
            m = jnp.max(s, axis=1, keepdims=True)
            p = jnp.exp2(s - m).astype(BF16)
            res = jnp.dot(p, jnp.concatenate([vv, ones_kv], axis=1), preferred_element_type=F32)
            result_rows(acc_ref, row0, r, d)[...] = jnp.where(head0, res[:blk, :LANES], res[blk:, :LANES])
            result_rows(sm_ref, row0, r, d)[...] = jnp.where(head0, m[:blk], m[blk:])
            result_rows(sd_ref, row0, r, d)[...] = jnp.where(head0, res[:blk, LANES:], res[blk:, LANES:])
            return carry

        lax.fori_loop(0, d * nb, block, 0, unroll=ATTN_UNROLL)

    rows_m = 256

    def mix(c, carry):
        l0 = pl.multiple_of(c * rows_m, rows_m)
        com = pl.ds(l0, rows_m)
        for r in range(d4):
            tok = pl.ds(l0 * d4 + r, rows_m, stride=d4)
            m1, m4, m16 = sm1[tok, :], sm4[r, com, :], sm16[r, com, :]
            top = jnp.maximum(jnp.maximum(m1, m4), m16)
            e1, e4, e16 = jnp.exp2(m1 - top), jnp.exp2(m4 - top), jnp.exp2(m16 - top)
            den = e1 * sd1[tok, :] + e4 * sd4[r, com, :] + e16 * sd16[r, com, :]
            num = e1 * acc1[tok, :] + e4 * acc4[r, com, :] + e16 * acc16[r, com, :]
            stage[tok, :] = num * (1.0 / den)
        return carry

    lax.fori_loop(0, seq // d4 // rows_m, mix, 0)

    def emit(c, carry):
        rows = pl.ds(pl.multiple_of(c * conv_rows, conv_rows), conv_rows)
        o_ref[rows, :] = stage[rows, :].astype(BF16)
        return carry

    lax.fori_loop(0, seq // conv_rows, emit, 0)


def _dilated_attention(q, k, v):
    B, S, W = q.shape
    d4 = DILATIONS[1]
    spec = pl.BlockSpec((None, S, LANES), lambda b, h: (b, 0, h))
    token = pltpu.VMEM((S, LANES), F32)
    by4 = pltpu.VMEM((d4, S // d4, LANES), F32)
    scratch = ([token, by4]
               + [pltpu.VMEM((S // d, d * LANES), BF16) for d in DILATIONS[1:] for _ in range(3)]
               + [token, by4, by4] * 3
               + [pltpu.VMEM((2, 2 * ATTN_BLOCK, 2 * ATTN_BLOCK), F32)])
    return pl.pallas_call(
        functools.partial(_attn_kernel, seq=S),
        name="dilated_attention",
        grid=(B, W // LANES),
        in_specs=[spec, spec, spec],
        out_specs=spec,
        out_shape=jax.ShapeDtypeStruct((B, S, W), BF16),
        scratch_shapes=scratch,
        compiler_params=pltpu.CompilerParams(
            dimension_semantics=("parallel", "parallel"), vmem_limit_bytes=VMEM_LIMIT),
    )(q, k, v)


def _gate_scans(gt_ref, a_ref, b_ref, col_ref):
    L, H = MLSTM_CHUNK, MLSTM_HEADS
    rows = gt_ref.shape[1]
    gates = gt_ref[...]
    f_pre = gates[H:]
    b = (jnp.minimum(f_pre, 0.0) - jnp.log(1.0 + jnp.exp(-jnp.abs(f_pre)))) * LOG2E
    pos = lax.broadcasted_iota(jnp.int32, (H, rows), 1) % L
    for step in [1 << e for e in range(L.bit_length() - 1)]:
        b = b + jnp.where(pos >= step, pltpu.roll(b, step, axis=1), 0.0)
    a = gates[:H] * LOG2E - b
    amax = a
    for step in [1 << e for e in range(L.bit_length() - 1)]:
        amax = jnp.maximum(amax, jnp.where(pos >= step, pltpu.roll(amax, step, axis=1), NEG_INF))
    a_ref[...] = a
    b_ref[...] = b
    col_ref[...] = jnp.concatenate([b, amax, a, jnp.zeros((LANES - 3 * H, rows), F32)], axis=0).T


def _mlstm_tile(qk_ref, v_ref, mo_ref, ng_ref, out_ref, c_ref, m_ref, a_ref, b_ref, col_ref, *, rows, base):
    L, H, dh = MLSTM_CHUNK, MLSTM_HEADS, MLSTM_HEAD_DIM
    causal = lax.broadcasted_iota(jnp.int32, (L, L), 1) <= lax.broadcasted_iota(jnp.int32, (L, L), 0)
    ones_blk = jnp.ones((L, dh), BF16)
    wide = lambda x: jnp.concatenate([x, x], axis=1)

    def head_chunk(h, r0):
        cols = slice(h * dh, (h + 1) * dh)
        tok = slice(r0, r0 + L)
        q_b = qk_ref[tok, cols]
        k_b = qk_ref[tok, MLSTM_WIDTH + h * dh:MLSTM_WIDTH + (h + 1) * dh]
        v_b = v_ref[tok, cols]

        pos = pl.ds(pl.multiple_of(base + r0, L), L)
        column = lambda j: jnp.broadcast_to(col_ref[pos, j:j + 1], (L, dh))
        b_rep, amax_rep, a_rep = column(h), column(H + h), column(2 * H + h)
        a_row = a_ref[h:h + 1, pos]
        b_last = b_ref[h:h + 1, pos][:, L - 1:L]
        m_prev = m_ref[h, 0:1, :]
        mx = jnp.maximum(m_prev, amax_rep)

        w_intra = jnp.exp2(jnp.where(causal, a_row - wide(mx), NEG_INF))
        qk = lax.dot_general(q_b, k_b, (((1,), (1,)), ((), ())), preferred_element_type=F32) * w_intra
        res = (wide(jnp.exp2(m_prev - mx)) * jnp.dot(q_b, c_ref[h].astype(BF16), preferred_element_type=F32)
               + jnp.dot(qk.astype(BF16), jnp.concatenate([v_b, ones_blk], axis=1), preferred_element_type=F32))
        hid = res[:, :dh] / jnp.maximum(jnp.abs(res[:, dh:]), jnp.exp2(-(b_rep + mx)))

        m_new = b_last + jnp.maximum(m_prev, jnp.max(a_row, axis=1, keepdims=True))
        decay = jnp.exp2(b_last + m_prev - m_new)
        ws = jnp.exp2(b_last + a_rep - m_new)
        wv = jnp.concatenate([(v_b.astype(F32) * ws).astype(BF16), ws.astype(BF16)], axis=1)
        c_ref[h] = wide(decay) * c_ref[h] + lax.dot_general(k_b, wv, (((0,), (0,)), ((), ())),
                                                            preferred_element_type=F32)
        m_ref[h] = jnp.broadcast_to(m_new, m_ref.shape[1:])

        hn = _rms(hid, ng_ref[:, cols])
        out_ref[tok, cols] = (jax.nn.sigmoid(mo_ref[tok, cols].astype(F32)) * hn).astype(BF16)

    for r0 in range(0, rows, L):
        for h in range(H):
            head_chunk(h, r0)


def _mlstm_tail_kernel(x_ref, at_ref, p_ref, gt_ref, qk0_ref, v0_ref, mo0_ref, qk_ref, v_ref, mo_ref,
                       ng_ref, wo_ref, g_mlp_ref, wu_ref, wd_ref, g_ple_ref, wpg_ref, wple_ref, g_fin_ref, o_ref,
                       ml_ref, c_ref, m_ref, a_ref, b_ref, col_ref, *, tm, ff_chunk, final_norm):
    state = (c_ref, m_ref, a_ref, b_ref, col_ref)
    i = pl.program_id(1)

    @pl.when(i == 0)
    def _():
        c_ref[...] = jnp.zeros_like(c_ref)
        m_ref[...] = jnp.zeros_like(m_ref)
        _gate_scans(gt_ref, a_ref, b_ref, col_ref)
        _mlstm_tile(qk0_ref, v0_ref, mo0_ref, ng_ref, ml_ref, *state, rows=tm, base=0)

    ml = ml_ref[...]
    nxt = jnp.minimum(i + 1, pl.num_programs(1) - 1)
    _mlstm_tile(qk_ref, v_ref, mo_ref, ng_ref, ml_ref, *state, rows=tm, base=nxt * tm)

    mix = jnp.concatenate([at_ref[...], ml], axis=1)
    h = x_ref[...] + jnp.dot(mix, wo_ref[...], preferred_element_type=F32)

    u = _rms(h, g_mlp_ref[...]).astype(BF16)
    for c0 in range(0, D_FF, ff_chunk):
        a = jnp.dot(u, wu_ref[:, c0:c0 + ff_chunk], preferred_element_type=F32)
        a = jnp.square(jnp.maximum(a, 0.0)).astype(BF16)
        h = h + jnp.dot(a, wd_ref[c0:c0 + ff_chunk, :], preferred_element_type=F32)

    gate = jax.nn.sigmoid(jnp.dot(_rms(h, g_ple_ref[...]).astype(BF16), wpg_ref[...], preferred_element_type=F32))
    h = h + gate * jnp.dot(p_ref[...].astype(BF16), wple_ref[...], preferred_element_type=F32)
    o_ref[...] = _rms(h, g_fin_ref[...]) if final_norm else h


def _mlstm_tail(x, attn, p, layer, mqk, mv, mo, gates_t, norm_g, w_out, g_mlp, w_up, w_down, g_ple, w_pg, w_ple,
                g_fin, final_norm, tm=512, ff_chunk=1024):
    B, S, D = x.shape
    H, dh, W = MLSTM_HEADS, MLSTM_HEAD_DIM, MLSTM_WIDTH
    n = S // tm
    tok = lambda width: pl.BlockSpec((None, tm, width), lambda b, i: (b, i, 0))
    first = lambda width: pl.BlockSpec((None, tm, width), lambda b, i: (b, 0, 0))
    nxt = lambda width: pl.BlockSpec((None, tm, width), lambda b, i: (b, jnp.minimum(i + 1, n - 1), 0))
    p_spec = pl.BlockSpec((None, None, tm, PLE_DIM), lambda b, i: (layer, b, i, 0))
    gt_spec = pl.BlockSpec((None, 2 * H, S), lambda b, i: (b, 0, 0))
    return pl.pallas_call(
        functools.partial(_mlstm_tail_kernel, tm=tm, ff_chunk=ff_chunk, final_norm=final_norm),
        name="mlstm_tail",
        grid=(B, n),
        in_specs=[tok(D), tok(ATTN_WIDTH), p_spec, gt_spec,
                  first(2 * W), first(W), first(W),
                  nxt(2 * W), nxt(W), nxt(W), _resident(norm_g.shape),
                  _resident(w_out.shape), _resident((1, D)), _resident(w_up.shape), _resident(w_down.shape),
                  _resident((1, D)), _resident(w_pg.shape), _resident(w_ple.shape), _resident((1, D))],
        out_specs=tok(D),
        out_shape=jax.ShapeDtypeStruct((B, S, D), F32),
        scratch_shapes=[pltpu.VMEM((tm, W), BF16),
                        pltpu.VMEM((H, dh, 2 * dh), F32), pltpu.VMEM((H, 8, LANES), F32),
                        pltpu.VMEM((H, S), F32), pltpu.VMEM((H, S), F32), pltpu.VMEM((S, LANES), F32)],
        compiler_params=pltpu.CompilerParams(
            dimension_semantics=("parallel", "arbitrary"), vmem_limit_bytes=VMEM_LIMIT),
    )(x, attn, p, gates_t, mqk, mv, mo, mqk, mv, mo, norm_g,
      w_out, g_mlp, w_up, w_down, g_ple, w_pg, w_ple, g_fin)


def _rope_tables(seq):
    half = ROPE_DIM // 2
    inv_freq = jnp.power(ROPE_THETA, -jnp.arange(half, dtype=F32) / half)
    ang = jnp.arange(seq, dtype=jnp.int32).astype(F32)[:, None] * inv_freq[None, :]
    cos, sin = jnp.cos(ang), jnp.sin(ang)
    rest = ATTN_HEAD_DIM - ROPE_DIM
    cos_head = jnp.concatenate([cos, cos, jnp.ones((seq, rest), F32)], axis=1)
    sin_head = jnp.concatenate([-sin, sin, jnp.zeros((seq, rest), F32)], axis=1)
    reps = LANES // ATTN_HEAD_DIM
    return jnp.tile(cos_head, (1, reps)), jnp.tile(sin_head, (1, reps))


def kernel(x, p, norm_mix_g, w_in, conv_w, conv_b, gate_b, mlstm_norm_g, w_out, norm_mlp_g, w_up, w_down,
           norm_ple_g, w_ple_gate, w_ple, final_norm_g):
    B, S, D = x.shape
    depth = w_in.shape[0]
    cos_t, sin_t = _rope_tables(S)
    row = lambda v: v.reshape(1, -1).astype(F32)
    n_gate = 2 * MLSTM_HEADS
    h = x
    for layer in range(depth):
        gb = jnp.pad(gate_b[layer].astype(F32), (0, LANES - n_gate)).reshape(1, LANES)
        q, k, v, mqk, mv, mo, gates_t = _in_proj(h, norm_mix_g[layer].astype(F32).reshape(D, 1), w_in[layer], gb,
                                                 cos_t, sin_t, conv_w[layer].astype(F32), row(conv_b[layer]))
        attn = _dilated_attention(q, k, v)
        h = _mlstm_tail(h, attn, p, layer, mqk, mv, mo, gates_t, row(mlstm_norm_g[layer]),
                        w_out[layer].astype(BF16), row(norm_mlp_g[layer]),
                        w_up[layer].astype(BF16), w_down[layer].astype(BF16), row(norm_ple_g[layer]),
                        w_ple_gate[layer].astype(BF16), w_ple[layer].astype(BF16), row(final_norm_g),
                        final_norm=(layer == depth - 1))
    return h
```

```python
import functools
import math

import jax
import jax.numpy as jnp
from jax import lax
from jax.experimental import pallas as pl
from jax.experimental.pallas import tpu as pltpu

D_MODEL = 1024
ATTN_HEADS = 8
ATTN_HEAD_DIM = 64
ATTN_WIDTH = ATTN_HEADS * ATTN_HEAD_DIM
MLSTM_HEADS = 4
MLSTM_HEAD_DIM = 128
MLSTM_WIDTH = MLSTM_HEADS * MLSTM_HEAD_DIM
DILATIONS = (1, 4, 16)
ATTN_BLOCK = 128
ATTN_UNROLL = 32
ROPE_THETA = 500000.0
ROPE_DIM = ATTN_HEAD_DIM // 4
CONV_WIDTH = 4
D_FF = 4 * D_MODEL
PLE_DIM = 256
RMS_EPS = 1e-6

LANES = 128
HEAD_PAIRS = ATTN_WIDTH // LANES
MAIN_WIDTH = 3 * ATTN_WIDTH + 4 * MLSTM_WIDTH
MLSTM_CHUNK = 256
CONV_HALO = 8
VMEM_LIMIT = 56 * 1024 * 1024

F32 = jnp.float32
BF16 = jnp.bfloat16
NEG_INF = float("-inf")
LOG2E = 1.4426950408889634


def _rms(x, g):
    return x * lax.rsqrt(jnp.mean(x * x, axis=-1, keepdims=True) + RMS_EPS) * g


def _resident(shape):
    return pl.BlockSpec(shape, lambda *_: (0,) * len(shape), pipeline_mode=pl.Buffered(1))


def _in_proj_kernel(x_ref, g_ref, w32_ref, gb_ref, cos_ref, sin_ref, cw_ref, cb_ref,
                    q_ref, k_ref, v_ref, mqk_ref, mv_ref, mo_ref, gt_ref, tail_ref, w_ref, wg_ref, *, tm, sub):
    @pl.when((pl.program_id(0) == 0) & (pl.program_id(1) == 0))
    def _():
        gain = g_ref[...]
        q_scale = LOG2E / math.sqrt(ATTN_HEAD_DIM)
        for c0 in range(0, MAIN_WIDTH, ATTN_WIDTH):
            w = w32_ref[c0:c0 + ATTN_WIDTH, :] * gain
            w_ref[:, c0:c0 + ATTN_WIDTH] = (w * q_scale if c0 == 0 else w).T.astype(BF16)
        n_gate = 2 * MLSTM_HEADS
        w_gates = w32_ref[MAIN_WIDTH:MAIN_WIDTH + n_gate, :] * gain
        wg_ref[...] = jnp.concatenate([w_gates, jnp.zeros((LANES - n_gate, w_gates.shape[1]), F32)],
                                      axis=0).T.astype(BF16)

    @pl.when(pl.program_id(1) == 0)
    def _():
        tail_ref[:CONV_HALO, :] = jnp.zeros((CONV_HALO, tail_ref.shape[1]), F32)

    A, M = ATTN_WIDTH, MLSTM_WIDTH
    lane = lax.broadcasted_iota(jnp.int32, (sub, LANES), 1)
    first_half = (lane % ATTN_HEAD_DIM) < (ROPE_DIM // 2)
    k_scale = 1.0 / math.sqrt(MLSTM_HEAD_DIM)
    rb = 128

    for s0 in range(0, tm, sub):
        rows = slice(s0, s0 + sub)
        u = _rms(x_ref[rows, :], 1.0).astype(BF16)
        pos0 = pl.multiple_of(pl.program_id(1) * tm + s0, sub)
        cos = cos_ref[pl.ds(pos0, sub), :]
        sin = sin_ref[pl.ds(pos0, sub), :]

        def proj(c0, width, u=u):
            return jnp.dot(u, w_ref[:, c0:c0 + width], preferred_element_type=F32)

        def rope_store(y, o_ref, cos=cos, sin=sin, rows=rows):
            for j in range(HEAD_PAIRS):
                yj = y[:, j * LANES:(j + 1) * LANES]
                partner = jnp.where(first_half,
                                    pltpu.roll(yj, LANES - ROPE_DIM // 2, axis=1),
                                    pltpu.roll(yj, ROPE_DIM // 2, axis=1))
                o_ref[rows, j * LANES:(j + 1) * LANES] = (yj * cos + partner * sin).astype(BF16)

        tail_ref[CONV_HALO + s0:CONV_HALO + s0 + sub, :] = proj(3 * A, 2 * M)
        for c0 in range(0, 2 * M, LANES):
            cols = slice(c0, c0 + LANES)
            for r0 in range(s0, s0 + sub, rb):
                acc = cb_ref[:, cols]
                for j in range(CONV_WIDTH):
                    start = CONV_HALO + r0 - j
                    acc = acc + cw_ref[CONV_WIDTH - 1 - j:CONV_WIDTH - j, cols] * tail_ref[start:start + rb, cols]
                act = acc * jax.nn.sigmoid(acc)
                mqk_ref[r0:r0 + rb, cols] = (act if c0 < M else act * k_scale).astype(BF16)
        rope_store(proj(0, A), q_ref)
        rope_store(proj(A, A), k_ref)
        v_ref[rows, :] = proj(2 * A, A).astype(BF16)
        mv_ref[rows, :] = proj(3 * A + 2 * M, M).astype(BF16)
        mo_ref[rows, :] = proj(3 * A + 3 * M, M).astype(BF16)
        gates = jnp.dot(u, wg_ref[...], preferred_element_type=F32) + gb_ref[...]
        gt_ref[:, rows] = gates.T[:2 * MLSTM_HEADS, :]
    tail_ref[:CONV_HALO, :] = tail_ref[tm:tm + CONV_HALO, :]


def _in_proj(x, g, w_in_t, gate_b, cos_t, sin_t, conv_w, conv_b, tm=1024, sub=256):
    B, S, D = x.shape
    grid = (B, S // tm)
    tok = lambda width: pl.BlockSpec((None, tm, width), lambda b, i: (b, i, 0))
    attn_shape = jax.ShapeDtypeStruct((B, S, ATTN_WIDTH), BF16)
    return pl.pallas_call(
        functools.partial(_in_proj_kernel, tm=tm, sub=sub),
        name="in_proj",
        grid=grid,
        in_specs=[tok(D), _resident((1, D)), _resident(w_in_t.shape),
                  _resident((1, LANES)), _resident(cos_t.shape), _resident(sin_t.shape),
                  _resident(conv_w.shape), _resident(conv_b.shape)],
        out_specs=[tok(ATTN_WIDTH), tok(ATTN_WIDTH), tok(ATTN_WIDTH),
                   tok(2 * MLSTM_WIDTH), tok(MLSTM_WIDTH), tok(MLSTM_WIDTH),
                   pl.BlockSpec((None, 2 * MLSTM_HEADS, tm), lambda b, i: (b, 0, i))],
        out_shape=[attn_shape, attn_shape, attn_shape,
                   jax.ShapeDtypeStruct((B, S, 2 * MLSTM_WIDTH), BF16),
                   jax.ShapeDtypeStruct((B, S, MLSTM_WIDTH), BF16),
                   jax.ShapeDtypeStruct((B, S, MLSTM_WIDTH), BF16),
                   jax.ShapeDtypeStruct((B, 2 * MLSTM_HEADS, S), F32)],
        scratch_shapes=[pltpu.VMEM((CONV_HALO + tm, 2 * MLSTM_WIDTH), F32),
                        pltpu.VMEM((D, MAIN_WIDTH), BF16), pltpu.VMEM((D, LANES), BF16)],
        compiler_params=pltpu.CompilerParams(
            dimension_semantics=("arbitrary", "arbitrary"), vmem_limit_bytes=VMEM_LIMIT),
    )(x, g, w_in_t, gate_b, cos_t, sin_t, conv_w, conv_b)


def _attn_kernel(q_ref, k_ref, v_ref, o_ref, stage, s4, q4, k4, v4, q16, k16, v16,
                 acc1, acc4, acc16, sm1, sm4, sm16, sd1, sd4, sd16, bias_ref, *, seq):
    blk = ATTN_BLOCK
    d4, d16 = DILATIONS[1], DILATIONS[2]
    assert DILATIONS[0] == 1 and d16 == d4 * d4
    qi = lax.broadcasted_iota(jnp.int32, (2 * blk, 2 * blk), 0) % blk
    ki = lax.broadcasted_iota(jnp.int32, (2 * blk, 2 * blk), 1)
    dist = qi + blk - ki
    valid = (dist >= 0) & (dist <= blk)
    bias_ref[1] = jnp.where(valid, 0.0, NEG_INF)
    bias_ref[0] = jnp.where(valid & (ki >= blk), 0.0, NEG_INF)

    head0 = lax.broadcasted_iota(jnp.int32, (blk, LANES), 1) < ATTN_HEAD_DIM
    ones_kv = jnp.ones((2 * blk, LANES), BF16)

    conv_rows = 512
    for x_ref, x4, x16 in ((q_ref, q4, q16), (k_ref, k4, k16), (v_ref, v4, v16)):
        def to_f32(c, carry, x_ref=x_ref):
            rows = pl.ds(pl.multiple_of(c * conv_rows, conv_rows), conv_rows)
            stage[rows, :] = x_ref[rows, :].astype(F32)
            return carry
        lax.fori_loop(0, seq // conv_rows, to_f32, 0)

        def by4(c, carry, x4=x4):
            l0 = pl.multiple_of(c * blk, blk)
            for r in range(d4):
                rows = stage[pl.ds(l0 * d4 + r, blk, stride=d4), :]
                s4[r, pl.ds(l0, blk), :] = rows
                x4[pl.ds(l0, blk), r * LANES:(r + 1) * LANES] = rows.astype(BF16)
            return carry
        lax.fori_loop(0, seq // d4 // blk, by4, 0)

        def by16(c, carry, x16=x16):
            l0 = pl.multiple_of(c * 64, 64)
            for r in range(d16):
                x16[pl.ds(l0, 64), r * LANES:(r + 1) * LANES] = (
                    s4[r % d4, pl.ds(l0 * d4 + r // d4, 64, stride=d4), :].astype(BF16))
            return carry
        lax.fori_loop(0, seq // d16 // 64, by16, 0)

    patterns = ((q_ref, k_ref, v_ref, acc1, sm1, sd1, 1),
                (q4, k4, v4, acc4, sm4, sd4, d4),
                (q16, k16, v16, acc16, sm16, sd16, d16))

    def result_rows(ref, row0, r, d):
        if d == 1:
            return ref.at[pl.ds(row0, blk), :]
        if d == d4:
            return ref.at[r, pl.ds(row0, blk), :]
        return ref.at[r % d4, pl.ds(row0 * d4 + r // d4, blk, stride=d4), :]

    def view_lanes(r, d):
        return slice(0, LANES) if d == 1 else pl.ds(pl.multiple_of(r * LANES, LANES), LANES)

    for (qv, kv, vv_ref, acc_ref, sm_ref, sd_ref, d) in patterns:
        nb = seq // d // blk

        def block(i, carry, qv=qv, kv=kv, vv_ref=vv_ref, acc_ref=acc_ref, sm_ref=sm_ref, sd_ref=sd_ref, d=d, nb=nb):
            r = i // nb
            n = i % nb
            row0 = pl.multiple_of(n * blk, blk)
            prow0 = pl.multiple_of(jnp.maximum(n - 1, 0) * blk, blk)
            lanes = view_lanes(r, d)
            q = qv[pl.ds(row0, blk), lanes]
            zero = jnp.zeros_like(q)
            q2 = jnp.concatenate([jnp.where(head0, q, zero), jnp.where(head0, zero, q)], axis=0)
            kk = jnp.concatenate([kv[pl.ds(prow0, blk), lanes], kv[pl.ds(row0, blk), lanes]], axis=0)
            vv = jnp.concatenate([vv_ref[pl.ds(prow0, blk), lanes], vv_ref[pl.ds(row0, blk), lanes]], axis=0)
            s = lax.dot_general(q2, kk, (((1,), (1,)), ((), ())), preferred_element_type=F32)
            s = s + bias_ref[jnp.minimum(n, 1)]
            m = jnp.max(s, axis=1, keepdims=True)
            p = jnp.exp2(s - m).astype(BF16)
            res = jnp.dot(p, jnp.concatenate([vv, ones_kv], axis=1), preferred_element_type=F32)
            result_rows(acc_ref, row0, r, d)[...] = jnp.where(head0, res[:blk, :LANES], res[blk:, :LANES])
            result_rows(sm_ref, row0, r, d)[...] = jnp.where(head0, m[:blk], m[blk:])
            result_rows(sd_ref, row0, r, d)[...] = jnp.where(head0, res[:blk, LANES:], res[blk:, LANES:])
            return carry

        lax.fori_loop(0, d * nb, block, 0, unroll=ATTN_UNROLL)

    rows_m = 256

    def mix(c, carry):
        l0 = pl.multiple_of(c * rows_m, rows_m)
        com = pl.ds(l0, rows_m)
        for r in range(d4):
            tok = pl.ds(l0 * d4 + r, rows_m, stride=d4)
            m1, m4, m16 = sm1[tok, :], sm4[r, com, :], sm16[r, com, :]
            top = jnp.maximum(jnp.maximum(m1, m4), m16)
            e1, e4, e16 = jnp.exp2(m1 - top), jnp.exp2(m4 - top), jnp.exp2(m16 - top)
            den = e1 * sd1[tok, :] + e4 * sd4[r, com, :] + e16 * sd16[r, com, :]
            num = e1 * acc1[tok, :] + e4 * acc4[r, com, :] + e16 * acc16[r, com, :]
            stage[tok, :] = num * (1.0 / den)
        return carry

    lax.fori_loop(0, seq // d4 // rows_m, mix, 0)

    def emit(c, carry):
        rows = pl.ds(pl.multiple_of(c * conv_rows, conv_rows), conv_rows)
        o_ref[rows, :] = stage[rows, :].astype(BF16)
        return carry

    lax.fori_loop(0, seq // conv_rows, emit, 0)


def _dilated_attention(q, k, v):
    B, S, W = q.shape
    d4 = DILATIONS[1]
    spec = pl.BlockSpec((None, S, LANES), lambda b, h: (b, 0, h))
    token = pltpu.VMEM((S, LANES), F32)
    by4 = pltpu.VMEM((d4, S // d4, LANES), F32)
    scratch = ([token, by4]
               + [pltpu.VMEM((S // d, d * LANES), BF16) for d in DILATIONS[1:] for _ in range(3)]
               + [token, by4, by4] * 3
               + [pltpu.VMEM((2, 2 * ATTN_BLOCK, 2 * ATTN_BLOCK), F32)])
    return pl.pallas_call(
        functools.partial(_attn_kernel, seq=S),
        name="dilated_attention",
        grid=(B, W // LANES),
        in_specs=[spec, spec, spec],
        out_specs=spec,
        out_shape=jax.ShapeDtypeStruct((B, S, W), BF16),
        scratch_shapes=scratch,
        compiler_params=pltpu.CompilerParams(
            dimension_semantics=("parallel", "parallel"), vmem_limit_bytes=VMEM_LIMIT),
    )(q, k, v)


def _gate_scans(gt_ref, a_ref, b_ref, col_ref):
    L, H = MLSTM_CHUNK, MLSTM_HEADS
    rows = gt_ref.shape[1]
    gates = gt_ref[...]
    f_pre = gates[H:]
    b = (jnp.minimum(f_pre, 0.0) - jnp.log(1.0 + jnp.exp(-jnp.abs(f_pre)))) * LOG2E
    pos = lax.broadcasted_iota(jnp.int32, (H, rows), 1) % L
    for step in [1 << e for e in range(L.bit_length() - 1)]:
        b = b + jnp.where(pos >= step, pltpu.roll(b, step, axis=1), 0.0)
    a = gates[:H] * LOG2E - b
    amax = a
    for step in [1 << e for e in range(L.bit_length() - 1)]:
        amax = jnp.maximum(amax, jnp.where(pos >= step, pltpu.roll(amax, step, axis=1), NEG_INF))
    a_ref[...] = a
    b_ref[...] = b
    col_ref[...] = jnp.concatenate([b, amax, a, jnp.zeros((LANES - 3 * H, rows), F32)], axis=0).T


def _mlstm_tile(qk_ref, v_ref, mo_ref, ng_ref, out_ref, c_ref, m_ref, a_ref, b_ref, col_ref, *, rows, base):
    L, H, dh = MLSTM_CHUNK, MLSTM_HEADS, MLSTM_HEAD_DIM
    causal = lax.broadcasted_iota(jnp.int32, (L, L), 1) <= lax.broadcasted_iota(jnp.int32, (L, L), 0)
    ones_blk = jnp.ones((L, dh), BF16)
    wide = lambda x: jnp.concatenate([x, x], axis=1)

    def head_chunk(h, r0):
        cols = slice(h * dh, (h + 1) * dh)
        tok = slice(r0, r0 + L)
        q_b = qk_ref[tok, cols]
        k_b = qk_ref[tok, MLSTM_WIDTH + h * dh:MLSTM_WIDTH + (h + 1) * dh]
        v_b = v_ref[tok, cols]

        pos = pl.ds(pl.multiple_of(base + r0, L), L)
        column = lambda j: jnp.broadcast_to(col_ref[pos, j:j + 1], (L, dh))
        b_rep, amax_rep, a_rep = column(h), column(H + h), column(2 * H + h)
        a_row = a_ref[h:h + 1, pos]
        b_last = b_ref[h:h + 1, pos][:, L - 1:L]
        m_prev = m_ref[h, 0:1, :]
        mx = jnp.maximum(m_prev, amax_rep)

        w_intra = jnp.exp2(jnp.where(causal, a_row - wide(mx), NEG_INF))
        qk = lax.dot_general(q_b, k_b, (((1,), (1,)), ((), ())), preferred_element_type=F32) * w_intra
        res = (wide(jnp.exp2(m_prev - mx)) * jnp.dot(q_b, c_ref[h].astype(BF16), preferred_element_type=F32)
               + jnp.dot(qk.astype(BF16), jnp.concatenate([v_b, ones_blk], axis=1), preferred_element_type=F32))
        hid = res[:, :dh] / jnp.maximum(jnp.abs(res[:, dh:]), jnp.exp2(-(b_rep + mx)))

        m_new = b_last + jnp.maximum(m_prev, jnp.max(a_row, axis=1, keepdims=True))
        decay = jnp.exp2(b_last + m_prev - m_new)
        ws = jnp.exp2(b_last + a_rep - m_new)
        wv = jnp.concatenate([(v_b.astype(F32) * ws).astype(BF16), ws.astype(BF16)], axis=1)
        c_ref[h] = wide(decay) * c_ref[h] + lax.dot_general(k_b, wv, (((0,), (0,)), ((), ())),
                                                            preferred_element_type=F32)
        m_ref[h] = jnp.broadcast_to(m_new, m_ref.shape[1:])

        hn = _rms(hid, ng_ref[:, cols])
        out_ref[tok, cols] = (jax.nn.sigmoid(mo_ref[tok, cols].astype(F32)) * hn).astype(BF16)

    for r0 in range(0, rows, L):
        for h in range(H):
            head_chunk(h, r0)


def _mlstm_tail_kernel(x_ref, at_ref, p_ref, gt_ref, qk0_ref, v0_ref, mo0_ref, qk_ref, v_ref, mo_ref,
                       ng_ref, wo_ref, g_mlp_ref, wu_ref, wd_ref, g_ple_ref, wpg_ref, wple_ref, g_fin_ref, o_ref,
                       ml_ref, c_ref, m_ref, a_ref, b_ref, col_ref, *, tm, ff_chunk, final_norm):
    state = (c_ref, m_ref, a_ref, b_ref, col_ref)
    i = pl.program_id(1)

    @pl.when(i == 0)
    def _():
        c_ref[...] = jnp.zeros_like(c_ref)
        m_ref[...] = jnp.zeros_like(m_ref)
        _gate_scans(gt_ref, a_ref, b_ref, col_ref)
        _mlstm_tile(qk0_ref, v0_ref, mo0_ref, ng_ref, ml_ref, *state, rows=tm, base=0)

    ml = ml_ref[...]
    nxt = jnp.minimum(i + 1, pl.num_programs(1) - 1)
    _mlstm_tile(qk_ref, v_ref, mo_ref, ng_ref, ml_ref, *state, rows=tm, base=nxt * tm)

    mix = jnp.concatenate([at_ref[...], ml], axis=1)
    h = x_ref[...] + jnp.dot(mix, wo_ref[...], preferred_element_type=F32)

    u = _rms(h, g_mlp_ref[...]).astype(BF16)
    for c0 in range(0, D_FF, ff_chunk):
        a = jnp.dot(u, wu_ref[:, c0:c0 + ff_chunk], preferred_element_type=F32)
        a = jnp.square(jnp.maximum(a, 0.0)).astype(BF16)
        h = h + jnp.dot(a, wd_ref[c0:c0 + ff_chunk, :], preferred_element_type=F32)

    gate = jax.nn.sigmoid(jnp.dot(_rms(h, g_ple_ref[...]).astype(BF16), wpg_ref[...], preferred_element_type=F32))
    h = h + gate * jnp.dot(p_ref[...].astype(BF16), wple_ref[...], preferred_element_type=F32)
    o_ref[...] = _rms(h, g_fin_ref[...]) if final_norm else h


def _mlstm_tail(x, attn, p, layer, mqk, mv, mo, gates_t, norm_g, w_out, g_mlp, w_up, w_down, g_ple, w_pg, w_ple,
                g_fin, final_norm, tm=512, ff_chunk=1024):
    B, S, D = x.shape
    H, dh, W = MLSTM_HEADS, MLSTM_HEAD_DIM, MLSTM_WIDTH
    n = S // tm
    tok = lambda width: pl.BlockSpec((None, tm, width), lambda b, i: (b, i, 0))
    first = lambda width: pl.BlockSpec((None, tm, width), lambda b, i: (b, 0, 0))
    nxt = lambda width: pl.BlockSpec((None, tm, width), lambda b, i: (b, jnp.minimum(i + 1, n - 1), 0))
    p_spec = pl.BlockSpec((None, None, tm, PLE_DIM), lambda b, i: (layer, b, i, 0))
    gt_spec = pl.BlockSpec((None, 2 * H, S), lambda b, i: (b, 0, 0))
    return pl.pallas_call(
        functools.partial(_mlstm_tail_kernel, tm=tm, ff_chunk=ff_chunk, final_norm=final_norm),
        name="mlstm_tail",
        grid=(B, n),
        in_specs=[tok(D), tok(ATTN_WIDTH), p_spec, gt_spec,
                  first(2 * W), first(W), first(W),
                  nxt(2 * W), nxt(W), nxt(W), _resident(norm_g.shape),
                  _resident(w_out.shape), _resident((1, D)), _resident(w_up.shape), _resident(w_down.shape),
                  _resident((1, D)), _resident(w_pg.shape), _resident(w_ple.shape), _resident((1, D))],
        out_specs=tok(D),
        out_shape=jax.ShapeDtypeStruct((B, S, D), F32),
        scratch_shapes=[pltpu.VMEM((tm, W), BF16),
                        pltpu.VMEM((H, dh, 2 * dh), F32), pltpu.VMEM((H, 8, LANES), F32),
                        pltpu.VMEM((H, S), F32), pltpu.VMEM((H, S), F32), pltpu.VMEM((S, LANES), F32)],
        compiler_params=pltpu.CompilerParams(
            dimension_semantics=("parallel", "arbitrary"), vmem_limit_bytes=VMEM_LIMIT),
    )(x, attn, p, gates_t, mqk, mv, mo, mqk, mv, mo, norm_g,
      w_out, g_mlp, w_up, w_down, g_ple, w_pg, w_ple, g_fin)


def _rope_tables(seq):
    half = ROPE_DIM // 2
    inv_freq = jnp.power(ROPE_THETA, -jnp.arange(half, dtype=F32) / half)
    ang = jnp.arange(seq, dtype=jnp.int32).astype(F32)[:, None] * inv_freq[None, :]
    cos, sin = jnp.cos(ang), jnp.sin(ang)
    rest = ATTN_HEAD_DIM - ROPE_DIM
    cos_head = jnp.concatenate([cos, cos, jnp.ones((seq, rest), F32)], axis=1)
    sin_head = jnp.concatenate([-sin, sin, jnp.zeros((seq, rest), F32)], axis=1)
    reps = LANES // ATTN_HEAD_DIM
    return jnp.tile(cos_head, (1, reps)), jnp.tile(sin_head, (1, reps))


def kernel(x, p, norm_mix_g, w_in, conv_w, conv_b, gate_b, mlstm_norm_g, w_out, norm_mlp_g, w_up, w_down,
           norm_ple_g, w_ple_gate, w_ple, final_norm_g):
    B, S, D = x.shape
    depth = w_in.shape[0]
    cos_t, sin_t = _rope_tables(S)
    row = lambda v: v.reshape(1, -1).astype(F32)
    n_gate = 2 * MLSTM_HEADS
    h = x
    for layer in range(depth):
        gb = jnp.pad(gate_b[layer].astype(F32), (0, LANES - n_gate)).reshape(1, LANES)
        q, k, v, mqk, mv, mo, gates_t = _in_proj(h, row(norm_mix_g[layer]), jnp.swapaxes(w_in[layer], 0, 1), gb,
                                                 cos_t, sin_t, conv_w[layer].astype(F32), row(conv_b[layer]))
        attn = _dilated_attention(q, k, v)
        h = _mlstm_tail(h, attn, p, layer, mqk, mv, mo, gates_t, row(mlstm_norm_g[layer]),
                        w_out[layer].astype(BF16), row(norm_mlp_g[layer]),
                        w_up[layer].astype(BF16), w_down[layer].astype(BF16), row(norm_ple_g[layer]),
                        w_ple_gate[layer].astype(BF16), w_ple[layer].astype(BF16), row(final_norm_g),
                        final_norm=(layer == depth - 1))
    return h
```

```python
import functools
import itertools
import math

import jax
import jax.numpy as jnp
from jax import lax
from jax.experimental import pallas as pl
from jax.experimental.pallas import tpu as pltpu

D_MODEL = 1024
ATTN_HEADS = 8
ATTN_HEAD_DIM = 64
ATTN_WIDTH = ATTN_HEADS * ATTN_HEAD_DIM
MLSTM_HEADS = 4
MLSTM_HEAD_DIM = 128
MLSTM_WIDTH = MLSTM_HEADS * MLSTM_HEAD_DIM
DILATIONS = (1, 4, 16)
ATTN_BLOCK = 128
ROPE_THETA = 500000.0
ROPE_DIM = ATTN_HEAD_DIM // 4
CONV_WIDTH = 4
D_FF = 4 * D_MODEL
PLE_DIM = 256
RMS_EPS = 1e-6

LANES = 128
HEAD_PAIRS = ATTN_WIDTH // LANES
MAIN_WIDTH = 3 * ATTN_WIDTH + 4 * MLSTM_WIDTH
MLSTM_CHUNK = 256
CONV_HALO = 8
VMEM_LIMIT = 56 * 1024 * 1024

F32 = jnp.float32
BF16 = jnp.bfloat16
NEG_INF = float("-inf")
LOG2E = 1.4426950408889634


def _rms(x, g):
    return x * lax.rsqrt(jnp.mean(x * x, axis=-1, keepdims=True) + RMS_EPS) * g


def _resident(shape):
    return pl.BlockSpec(shape, lambda *_: (0,) * len(shape), pipeline_mode=pl.Buffered(1))


def _in_proj_kernel(x_ref, g_ref, w32_ref, gb_ref, cos_ref, sin_ref, cw_ref, cb_ref,
                    q_ref, k_ref, v_ref, mqk_ref, mv_ref, mo_ref, gt_ref, tail_ref, w_ref, wg_ref, *, tm, sub):
    @pl.when((pl.program_id(0) == 0) & (pl.program_id(1) == 0))
    def _():
        gain = g_ref[...]
        q_scale = LOG2E / math.sqrt(ATTN_HEAD_DIM)
        for c0 in range(0, MAIN_WIDTH, ATTN_WIDTH):
            w = w32_ref[c0:c0 + ATTN_WIDTH, :] * gain
            w_ref[:, c0:c0 + ATTN_WIDTH] = (w * q_scale if c0 == 0 else w).T.astype(BF16)
        n_gate = 2 * MLSTM_HEADS
        w_gates = w32_ref[MAIN_WIDTH:MAIN_WIDTH + n_gate, :] * gain
        wg_ref[...] = jnp.concatenate([w_gates, jnp.zeros((LANES - n_gate, w_gates.shape[1]), F32)],
                                      axis=0).T.astype(BF16)

    @pl.when(pl.program_id(1) == 0)
    def _():
        tail_ref[:CONV_HALO, :] = jnp.zeros((CONV_HALO, tail_ref.shape[1]), F32)

    A, M = ATTN_WIDTH, MLSTM_WIDTH
    lane = lax.broadcasted_iota(jnp.int32, (sub, LANES), 1)
    first_half = (lane % ATTN_HEAD_DIM) < (ROPE_DIM // 2)
    k_scale = 1.0 / math.sqrt(MLSTM_HEAD_DIM)
    rb = 128

    for s0 in range(0, tm, sub):
        rows = slice(s0, s0 + sub)
        u = _rms(x_ref[rows, :], 1.0).astype(BF16)
        pos0 = pl.multiple_of(pl.program_id(1) * tm + s0, sub)
        cos = cos_ref[pl.ds(pos0, sub), :]
        sin = sin_ref[pl.ds(pos0, sub), :]

        def proj(c0, width, u=u):
            return jnp.dot(u, w_ref[:, c0:c0 + width], preferred_element_type=F32)

        def rope_store(y, o_ref, cos=cos, sin=sin, rows=rows):
            for j in range(HEAD_PAIRS):
                yj = y[:, j * LANES:(j + 1) * LANES]
                partner = jnp.where(first_half,
                                    pltpu.roll(yj, LANES - ROPE_DIM // 2, axis=1),
                                    pltpu.roll(yj, ROPE_DIM // 2, axis=1))
                o_ref[rows, j * LANES:(j + 1) * LANES] = (yj * cos + partner * sin).astype(BF16)

        tail_ref[CONV_HALO + s0:CONV_HALO + s0 + sub, :] = proj(3 * A, 2 * M)
        for c0 in range(0, 2 * M, LANES):
            cols = slice(c0, c0 + LANES)
            for r0 in range(s0, s0 + sub, rb):
                acc = cb_ref[:, cols]
                for j in range(CONV_WIDTH):
                    start = CONV_HALO + r0 - j
                    acc = acc + cw_ref[CONV_WIDTH - 1 - j:CONV_WIDTH - j, cols] * tail_ref[start:start + rb, cols]
                act = acc * jax.nn.sigmoid(acc)
                mqk_ref[r0:r0 + rb, cols] = (act if c0 < M else act * k_scale).astype(BF16)
        rope_store(proj(0, A), q_ref)
        rope_store(proj(A, A), k_ref)
        v_ref[rows, :] = proj(2 * A, A).astype(BF16)
        mv_ref[rows, :] = proj(3 * A + 2 * M, M).astype(BF16)
        mo_ref[rows, :] = proj(3 * A + 3 * M, M).astype(BF16)
        gates = jnp.dot(u, wg_ref[...], preferred_element_type=F32) + gb_ref[...]
        gt_ref[:, rows] = gates.T[:2 * MLSTM_HEADS, :]
    tail_ref[:CONV_HALO, :] = tail_ref[tm:tm + CONV_HALO, :]


def _in_proj(x, g, w_in_t, gate_b, cos_t, sin_t, conv_w, conv_b, tm=1024, sub=256):
    B, S, D = x.shape
    grid = (B, S // tm)
    tok = lambda width: pl.BlockSpec((None, tm, width), lambda b, i: (b, i, 0))
    attn_shape = jax.ShapeDtypeStruct((B, S, ATTN_WIDTH), BF16)
    return pl.pallas_call(
        functools.partial(_in_proj_kernel, tm=tm, sub=sub),
        name="in_proj",
        grid=grid,
        in_specs=[tok(D), _resident((1, D)), _resident(w_in_t.shape),
                  _resident((1, LANES)), _resident(cos_t.shape), _resident(sin_t.shape),
                  _resident(conv_w.shape), _resident(conv_b.shape)],
        out_specs=[tok(ATTN_WIDTH), tok(ATTN_WIDTH), tok(ATTN_WIDTH),
                   tok(2 * MLSTM_WIDTH), tok(MLSTM_WIDTH), tok(MLSTM_WIDTH),
                   pl.BlockSpec((None, 2 * MLSTM_HEADS, tm), lambda b, i: (b, 0, i))],
        out_shape=[attn_shape, attn_shape, attn_shape,
                   jax.ShapeDtypeStruct((B, S, 2 * MLSTM_WIDTH), BF16),
                   jax.ShapeDtypeStruct((B, S, MLSTM_WIDTH), BF16),
                   jax.ShapeDtypeStruct((B, S, MLSTM_WIDTH), BF16),
                   jax.ShapeDtypeStruct((B, 2 * MLSTM_HEADS, S), F32)],
        scratch_shapes=[pltpu.VMEM((CONV_HALO + tm, 2 * MLSTM_WIDTH), F32),
                        pltpu.VMEM((D, MAIN_WIDTH), BF16), pltpu.VMEM((D, LANES), BF16)],
        compiler_params=pltpu.CompilerParams(
            dimension_semantics=("arbitrary", "arbitrary"), vmem_limit_bytes=VMEM_LIMIT),
    )(x, g, w_in_t, gate_b, cos_t, sin_t, conv_w, conv_b)


def _attn_kernel(q_ref, k_ref, v_ref, qn_ref, kn_ref, vn_ref, o_ref, stage, s4, *rest, seq):
    views = (rest[0:6], rest[6:12])
    acc1, acc4, acc16, sm1, sm4, sm16, sd1, sd4, sd16, bias_ref = rest[12:]
    blk = ATTN_BLOCK
    d4, d16 = DILATIONS[1], DILATIONS[2]
    assert DILATIONS[0] == 1 and d16 == d4 * d4
    step = pl.program_id(0) * pl.num_programs(1) + pl.program_id(1)
    qi = lax.broadcasted_iota(jnp.int32, (2 * blk, 2 * blk), 0) % blk
    ki = lax.broadcasted_iota(jnp.int32, (2 * blk, 2 * blk), 1)
    dist = qi + blk - ki
    valid = (dist >= 0) & (dist <= blk)
    bias_ref[1] = jnp.where(valid, 0.0, NEG_INF)
    bias_ref[0] = jnp.where(valid & (ki >= blk), 0.0, NEG_INF)

    head0 = lax.broadcasted_iota(jnp.int32, (blk, LANES), 1) < ATTN_HEAD_DIM
    ones_kv = jnp.ones((2 * blk, LANES), BF16)

    def view_pieces(x_refs, out):
        piece_rows = 512
        pieces = []
        for x_ref, x4, x16 in zip(x_refs, out[0:3], out[3:6]):
            for r0 in range(0, seq, piece_rows):
                def to_f32(x_ref=x_ref, r0=r0):
                    stage[r0:r0 + piece_rows, :] = x_ref[r0:r0 + piece_rows, :].astype(F32)
                pieces.append(to_f32)
            for r in range(d4):
                for l0 in range(0, seq // d4, piece_rows):
                    def by4(x4=x4, r=r, l0=l0):
                        rows = stage[pl.ds(l0 * d4 + r, piece_rows, stride=d4), :]
                        s4[r, l0:l0 + piece_rows, :] = rows
                        x4[l0:l0 + piece_rows, r * LANES:(r + 1) * LANES] = rows.astype(BF16)
                    pieces.append(by4)
            for r in range(d16):
                def by16(x16=x16, r=r):
                    x16[:, r * LANES:(r + 1) * LANES] = (
                        s4[r % d4, pl.ds(r // d4, seq // d16, stride=d4), :].astype(BF16))
                pieces.append(by16)
        return pieces

    def result_rows(ref, row0, r, d):
        if d == 1:
            return ref.at[pl.ds(row0, blk), :]
        if d == d4:
            return ref.at[r, pl.ds(row0, blk), :]
        return ref.at[r % d4, pl.ds(row0 * d4 + r // d4, blk, stride=d4), :]

    def attend(cur, between):
        between = iter(between)
        patterns = ((q_ref, k_ref, v_ref, acc1, sm1, sd1, 1),
                    (cur[0], cur[1], cur[2], acc4, sm4, sd4, d4),
                    (cur[3], cur[4], cur[5], acc16, sm16, sd16, d16))
        for (qv, kv, vv_ref, acc_ref, sm_ref, sd_ref, d) in patterns:
            nb = seq // d // blk
            for r in range(d):
                lanes = slice(r * LANES, (r + 1) * LANES)
                for n in range(nb):
                    row0, prow0 = n * blk, max(n - 1, 0) * blk
                    q = qv[row0:row0 + blk, lanes]
                    zero = jnp.zeros_like(q)
                    q2 = jnp.concatenate([jnp.where(head0, q, zero), jnp.where(head0, zero, q)], axis=0)
                    kk = jnp.concatenate([kv[prow0:prow0 + blk, lanes], kv[row0:row0 + blk, lanes]], axis=0)
                    vv = jnp.concatenate([vv_ref[prow0:prow0 + blk, lanes], vv_ref[row0:row0 + blk, lanes]], axis=0)
                    s = lax.dot_general(q2, kk, (((1,), (1,)), ((), ())), preferred_element_type=F32)
                    s = s + bias_ref[min(n, 1)]
                    m = jnp.max(s, axis=1, keepdims=True)
                    p = jnp.exp2(s - m).astype(BF16)
                    res = jnp.dot(p, jnp.concatenate([vv, ones_kv], axis=1), preferred_element_type=F32)
                    result_rows(acc_ref, row0, r, d)[...] = jnp.where(head0, res[:blk, :LANES], res[blk:, :LANES])
                    result_rows(sm_ref, row0, r, d)[...] = jnp.where(head0, m[:blk], m[blk:])
                    result_rows(sd_ref, row0, r, d)[...] = jnp.where(head0, res[:blk, LANES:], res[blk:, LANES:])
                    for piece in itertools.islice(between, 1):
                        piece()
        for piece in between:
            piece()

    @pl.when(step == 0)
    def _():
        for piece in view_pieces((q_ref, k_ref, v_ref), views[0]):
            piece()

    for parity in range(2):
        @pl.when(step % 2 == parity)
        def _(parity=parity):
            attend(views[parity], view_pieces((qn_ref, kn_ref, vn_ref), views[1 - parity]))

    rows_m = 256

    def mix(c, carry):
        l0 = pl.multiple_of(c * rows_m, rows_m)
        com = pl.ds(l0, rows_m)
        for r in range(d4):
            tok = pl.ds(l0 * d4 + r, rows_m, stride=d4)
            m1, m4, m16 = sm1[tok, :], sm4[r, com, :], sm16[r, com, :]
            top = jnp.maximum(jnp.maximum(m1, m4), m16)
            e1, e4, e16 = jnp.exp2(m1 - top), jnp.exp2(m4 - top), jnp.exp2(m16 - top)
            den = e1 * sd1[tok, :] + e4 * sd4[r, com, :] + e16 * sd16[r, com, :]
            num = e1 * acc1[tok, :] + e4 * acc4[r, com, :] + e16 * acc16[r, com, :]
            stage[tok, :] = num * (1.0 / den)
        return carry

    lax.fori_loop(0, seq // d4 // rows_m, mix, 0)

    conv_rows = 512

    def emit(c, carry):
        rows = pl.ds(pl.multiple_of(c * conv_rows, conv_rows), conv_rows)
        o_ref[rows, :] = stage[rows, :].astype(BF16)
        return carry

    lax.fori_loop(0, seq // conv_rows, emit, 0)


def _dilated_attention(q, k, v):
    B, S, W = q.shape
    hp = W // LANES
    d4 = DILATIONS[1]
    spec = pl.BlockSpec((None, S, LANES), lambda b, h: (b, 0, h))

    def following(b, h):
        t = jnp.minimum(b * hp + h + 1, B * hp - 1)
        return (t // hp, 0, t % hp)

    nxt = pl.BlockSpec((None, S, LANES), following)
    token = pltpu.VMEM((S, LANES), F32)
    by4 = pltpu.VMEM((d4, S // d4, LANES), F32)
    view_set = [pltpu.VMEM((S // d, d * LANES), BF16) for d in DILATIONS[1:] for _ in range(3)]
    scratch = ([token, by4] + view_set * 2 + [token, by4, by4] * 3
               + [pltpu.VMEM((2, 2 * ATTN_BLOCK, 2 * ATTN_BLOCK), F32)])
    return pl.pallas_call(
        functools.partial(_attn_kernel, seq=S),
        name="dilated_attention",
        grid=(B, hp),
        in_specs=[spec, spec, spec, nxt, nxt, nxt],
        out_specs=spec,
        out_shape=jax.ShapeDtypeStruct((B, S, W), BF16),
        scratch_shapes=scratch,
        compiler_params=pltpu.CompilerParams(
            dimension_semantics=("arbitrary", "arbitrary"), vmem_limit_bytes=VMEM_LIMIT),
    )(q, k, v, q, k, v)


def _gate_scans(gt_ref, a_ref, b_ref, col_ref):
    L, H = MLSTM_CHUNK, MLSTM_HEADS
    rows = gt_ref.shape[1]
    gates = gt_ref[...]
    f_pre = gates[H:]
    b = (jnp.minimum(f_pre, 0.0) - jnp.log(1.0 + jnp.exp(-jnp.abs(f_pre)))) * LOG2E
    pos = lax.broadcasted_iota(jnp.int32, (H, rows), 1) % L
    for step in [1 << e for e in range(L.bit_length() - 1)]:
        b = b + jnp.where(pos >= step, pltpu.roll(b, step, axis=1), 0.0)
    a = gates[:H] * LOG2E - b
    amax = a
    for step in [1 << e for e in range(L.bit_length() - 1)]:
        amax = jnp.maximum(amax, jnp.where(pos >= step, pltpu.roll(amax, step, axis=1), NEG_INF))
    a_ref[...] = a
    b_ref[...] = b
    col_ref[...] = jnp.concatenate([b, amax, a, jnp.zeros((LANES - 3 * H, rows), F32)], axis=0).T


def _mlstm_tile(qk_ref, v_ref, mo_ref, ng_ref, out_ref, c_ref, m_ref, a_ref, b_ref, col_ref, *, rows, base):
    L, H, dh = MLSTM_CHUNK, MLSTM_HEADS, MLSTM_HEAD_DIM
    causal = lax.broadcasted_iota(jnp.int32, (L, L), 1) <= lax.broadcasted_iota(jnp.int32, (L, L), 0)
    ones_blk = jnp.ones((L, dh), BF16)
    wide = lambda x: jnp.concatenate([x, x], axis=1)

    def head_chunk(h, r0):
        cols = slice(h * dh, (h + 1) * dh)
        tok = slice(r0, r0 + L)
        q_b = qk_ref[tok, cols]
        k_b = qk_ref[tok, MLSTM_WIDTH + h * dh:MLSTM_WIDTH + (h + 1) * dh]
        v_b = v_ref[tok, cols]

        pos = pl.ds(pl.multiple_of(base + r0, L), L)
        column = lambda j: jnp.broadcast_to(col_ref[pos, j:j + 1], (L, dh))
        b_rep, amax_rep, a_rep = column(h), column(H + h), column(2 * H + h)
        a_row = a_ref[h:h + 1, pos]
        b_last = b_ref[h:h + 1, pos][:, L - 1:L]
        m_prev = m_ref[h, 0:1, :]
        mx = jnp.maximum(m_prev, amax_rep)

        w_intra = jnp.exp2(jnp.where(causal, a_row - wide(mx), NEG_INF))
        qk = lax.dot_general(q_b, k_b, (((1,), (1,)), ((), ())), preferred_element_type=F32) * w_intra
        res = (wide(jnp.exp2(m_prev - mx)) * jnp.dot(q_b, c_ref[h].astype(BF16), preferred_element_type=F32)
               + jnp.dot(qk.astype(BF16), jnp.concatenate([v_b, ones_blk], axis=1), preferred_element_type=F32))
        hid = res[:, :dh] / jnp.maximum(jnp.abs(res[:, dh:]), jnp.exp2(-(b_rep + mx)))

        m_new = b_last + jnp.maximum(m_prev, jnp.max(a_row, axis=1, keepdims=True))
        decay = jnp.exp2(b_last + m_prev - m_new)
        ws = jnp.exp2(b_last + a_rep - m_new)
        wv = jnp.concatenate([(v_b.astype(F32) * ws).astype(BF16), ws.astype(BF16)], axis=1)
        c_ref[h] = wide(decay) * c_ref[h] + lax.dot_general(k_b, wv, (((0,), (0,)), ((), ())),
                                                            preferred_element_type=F32)
        m_ref[h] = jnp.broadcast_to(m_new, m_ref.shape[1:])

        hn = _rms(hid, ng_ref[:, cols])
        out_ref[tok, cols] = (jax.nn.sigmoid(mo_ref[tok, cols].astype(F32)) * hn).astype(BF16)

    for r0 in range(0, rows, L):
        for h in range(H):
            head_chunk(h, r0)


def _mlstm_tail_kernel(x_ref, at_ref, p_ref, gt_ref, qk0_ref, v0_ref, mo0_ref, qk_ref, v_ref, mo_ref,
                       ng_ref, wo_ref, g_mlp_ref, wu_ref, wd_ref, g_ple_ref, wpg_ref, wple_ref, g_fin_ref, o_ref,
                       ml_ref, c_ref, m_ref, a_ref, b_ref, col_ref, *, tm, ff_chunk, final_norm):
    state = (c_ref, m_ref, a_ref, b_ref, col_ref)
    i = pl.program_id(1)

    @pl.when(i == 0)
    def _():
        c_ref[...] = jnp.zeros_like(c_ref)
        m_ref[...] = jnp.zeros_like(m_ref)
        _gate_scans(gt_ref, a_ref, b_ref, col_ref)
        _mlstm_tile(qk0_ref, v0_ref, mo0_ref, ng_ref, ml_ref, *state, rows=tm, base=0)

    ml = ml_ref[...]
    nxt = jnp.minimum(i + 1, pl.num_programs(1) - 1)
    _mlstm_tile(qk_ref, v_ref, mo_ref, ng_ref, ml_ref, *state, rows=tm, base=nxt * tm)

    mix = jnp.concatenate([at_ref[...], ml], axis=1)
    h = x_ref[...] + jnp.dot(mix, wo_ref[...], preferred_element_type=F32)

    u = _rms(h, g_mlp_ref[...]).astype(BF16)
    for c0 in range(0, D_FF, ff_chunk):
        a = jnp.dot(u, wu_ref[:, c0:c0 + ff_chunk], preferred_element_type=F32)
        a = jnp.square(jnp.maximum(a, 0.0)).astype(BF16)
        h = h + jnp.dot(a, wd_ref[c0:c0 + ff_chunk, :], preferred_element_type=F32)

    gate = jax.nn.sigmoid(jnp.dot(_rms(h, g_ple_ref[...]).astype(BF16), wpg_ref[...], preferred_element_type=F32))
    h = h + gate * jnp.dot(p_ref[...].astype(BF16), wple_ref[...], preferred_element_type=F32)
    o_ref[...] = _rms(h, g_fin_ref[...]) if final_norm else h


def _mlstm_tail(x, attn, p, layer, mqk, mv, mo, gates_t, norm_g, w_out, g_mlp, w_up, w_down, g_ple, w_pg, w_ple,
                g_fin, final_norm, tm=512, ff_chunk=1024):
    B, S, D = x.shape
    H, dh, W = MLSTM_HEADS, MLSTM_HEAD_DIM, MLSTM_WIDTH
    n = S // tm
    tok = lambda width: pl.BlockSpec((None, tm, width), lambda b, i: (b, i, 0))
    first = lambda width: pl.BlockSpec((None, tm, width), lambda b, i: (b, 0, 0))
    nxt = lambda width: pl.BlockSpec((None, tm, width), lambda b, i: (b, jnp.minimum(i + 1, n - 1), 0))
    p_spec = pl.BlockSpec((None, None, tm, PLE_DIM), lambda b, i: (layer, b, i, 0))
    gt_spec = pl.BlockSpec((None, 2 * H, S), lambda b, i: (b, 0, 0))
    return pl.pallas_call(
        functools.partial(_mlstm_tail_kernel, tm=tm, ff_chunk=ff_chunk, final_norm=final_norm),
        name="mlstm_tail",
        grid=(B, n),
        in_specs=[tok(D), tok(ATTN_WIDTH), p_spec, gt_spec,
                  first(2 * W), first(W), first(W),
                  nxt(2 * W), nxt(W), nxt(W), _resident(norm_g.shape),
                  _resident(w_out.shape), _resident((1, D)), _resident(w_up.shape), _resident(w_down.shape),
                  _resident((1, D)), _resident(w_pg.shape), _resident(w_ple.shape), _resident((1, D))],
        out_specs=tok(D),
        out_shape=jax.ShapeDtypeStruct((B, S, D), F32),
        scratch_shapes=[pltpu.VMEM((tm, W), BF16),
                        pltpu.VMEM((H, dh, 2 * dh), F32), pltpu.VMEM((H, 8, LANES), F32),
                        pltpu.VMEM((H, S), F32), pltpu.VMEM((H, S), F32), pltpu.VMEM((S, LANES), F32)],
        compiler_params=pltpu.CompilerParams(
            dimension_semantics=("parallel", "arbitrary"), vmem_limit_bytes=VMEM_LIMIT),
    )(x, attn, p, gates_t, mqk, mv, mo, mqk, mv, mo, norm_g,
      w_out, g_mlp, w_up, w_down, g_ple, w_pg, w_ple, g_fin)


def _rope_tables(seq):
    half = ROPE_DIM // 2
    inv_freq = jnp.power(ROPE_THETA, -jnp.arange(half, dtype=F32) / half)
    ang = jnp.arange(seq, dtype=jnp.int32).astype(F32)[:, None] * inv_freq[None, :]
    cos, sin = jnp.cos(ang), jnp.sin(ang)
    rest = ATTN_HEAD_DIM - ROPE_DIM
    cos_head = jnp.concatenate([cos, cos, jnp.ones((seq, rest), F32)], axis=1)
    sin_head = jnp.concatenate([-sin, sin, jnp.zeros((seq, rest), F32)], axis=1)
    reps = LANES // ATTN_HEAD_DIM
    return jnp.tile(cos_head, (1, reps)), jnp.tile(sin_head, (1, reps))


def kernel(x, p, norm_mix_g, w_in, conv_w, conv_b, gate_b, mlstm_norm_g, w_out, norm_mlp_g, w_up, w_down,
           norm_ple_g, w_ple_gate, w_ple, final_norm_g):
    B, S, D = x.shape
    depth = w_in.shape[0]
    cos_t, sin_t = _rope_tables(S)
    row = lambda v: v.reshape(1, -1).astype(F32)
    n_gate = 2 * MLSTM_HEADS
    h = x
    for layer in range(depth):
        gb = jnp.pad(gate_b[layer].astype(F32), (0, LANES - n_gate)).reshape(1, LANES)
        q, k, v, mqk, mv, mo, gates_t = _in_proj(h, row(norm_mix_g[layer]), jnp.swapaxes(w_in[layer], 0, 1), gb,
                                                 cos_t, sin_t, conv_w[layer].astype(F32), row(conv_b[layer]))
        attn = _dilated_attention(q, k, v)
        h = _mlstm_tail(h, attn, p, layer, mqk, mv, mo, gates_t, row(mlstm_norm_g[layer]),
                        w_out[layer].astype(BF16), row(norm_mlp_g[layer]),
                        w_up[layer].astype(BF16), w_down[layer].astype(BF16), row(norm_ple_g[layer]),
                        w_ple_gate[layer].astype(BF16), w_ple[layer].astype(BF16), row(final_norm_g),
                        final_norm=(layer == depth - 1))
    return h
```

```python
import functools
import math

import jax
import jax.numpy as jnp
from jax import lax
from jax.experimental import pallas as pl
from jax.experimental.pallas import tpu as pltpu

D_MODEL = 1024
ATTN_HEADS = 8
ATTN_HEAD_DIM = 64
ATTN_WIDTH = ATTN_HEADS * ATTN_HEAD_DIM
MLSTM_HEADS = 4
MLSTM_HEAD_DIM = 128
MLSTM_WIDTH = MLSTM_HEADS * MLSTM_HEAD_DIM
DILATIONS = (1, 4, 16)
ATTN_BLOCK = 128
ATTN_UNROLL = 32
ROPE_THETA = 500000.0
ROPE_DIM = ATTN_HEAD_DIM // 4
CONV_WIDTH = 4
D_FF = 4 * D_MODEL
PLE_DIM = 256
RMS_EPS = 1e-6

LANES = 128
HEAD_PAIRS = ATTN_WIDTH // LANES
MAIN_WIDTH = 3 * ATTN_WIDTH + 4 * MLSTM_WIDTH
MLSTM_CHUNK = 256
CONV_HALO = 8
VMEM_LIMIT = 56 * 1024 * 1024

F32 = jnp.float32
BF16 = jnp.bfloat16
NEG_INF = float("-inf")
LOG2E = 1.4426950408889634


def _rms(x, g):
    return x * lax.rsqrt(jnp.mean(x * x, axis=-1, keepdims=True) + RMS_EPS) * g


def _resident(shape):
    return pl.BlockSpec(shape, lambda *_: (0,) * len(shape), pipeline_mode=pl.Buffered(1))


def _in_proj_kernel(x_ref, g_ref, w32_ref, gb_ref, cos_ref, sin_ref, cw_ref, cb_ref,
                    q_ref, k_ref, v_ref, mqk_ref, mv_ref, mo_ref, gt_ref, tail_ref, w_ref, wg_ref, *, tm, sub):
    @pl.when((pl.program_id(0) == 0) & (pl.program_id(1) == 0))
    def _():
        gain = g_ref[...]
        q_scale = LOG2E / math.sqrt(ATTN_HEAD_DIM)
        for c0 in range(0, MAIN_WIDTH, ATTN_WIDTH):
            w = w32_ref[c0:c0 + ATTN_WIDTH, :] * gain
            w_ref[:, c0:c0 + ATTN_WIDTH] = (w * q_scale if c0 == 0 else w).T.astype(BF16)
        n_gate = 2 * MLSTM_HEADS
        w_gates = w32_ref[MAIN_WIDTH:MAIN_WIDTH + n_gate, :] * gain
        wg_ref[...] = jnp.concatenate([w_gates, jnp.zeros((LANES - n_gate, w_gates.shape[1]), F32)],
                                      axis=0).T.astype(BF16)

    @pl.when(pl.program_id(1) == 0)
    def _():
        tail_ref[:CONV_HALO, :] = jnp.zeros((CONV_HALO, tail_ref.shape[1]), F32)

    A, M = ATTN_WIDTH, MLSTM_WIDTH
    lane = lax.broadcasted_iota(jnp.int32, (sub, LANES), 1)
    first_half = (lane % ATTN_HEAD_DIM) < (ROPE_DIM // 2)
    k_scale = 1.0 / math.sqrt(MLSTM_HEAD_DIM)
    rb = 128

    for s0 in range(0, tm, sub):
        rows = slice(s0, s0 + sub)
        u = _rms(x_ref[rows, :], 1.0).astype(BF16)
        pos0 = pl.multiple_of(pl.program_id(1) * tm + s0, sub)
        cos = cos_ref[pl.ds(pos0, sub), :]
        sin = sin_ref[pl.ds(pos0, sub), :]

        def proj(c0, width, u=u):
            return jnp.dot(u, w_ref[:, c0:c0 + width], preferred_element_type=F32)

        def rope_store(y, o_ref, cos=cos, sin=sin, rows=rows):
            for j in range(HEAD_PAIRS):
                yj = y[:, j * LANES:(j + 1) * LANES]
                partner = jnp.where(first_half,
                                    pltpu.roll(yj, LANES - ROPE_DIM // 2, axis=1),
                                    pltpu.roll(yj, ROPE_DIM // 2, axis=1))
                o_ref[rows, j * LANES:(j + 1) * LANES] = (yj * cos + partner * sin).astype(BF16)

        tail_ref[CONV_HALO + s0:CONV_HALO + s0 + sub, :] = proj(3 * A, 2 * M)
        for c0 in range(0, 2 * M, LANES):
            cols = slice(c0, c0 + LANES)
            for r0 in range(s0, s0 + sub, rb):
                acc = cb_ref[:, cols]
                for j in range(CONV_WIDTH):
                    start = CONV_HALO + r0 - j
                    acc = acc + cw_ref[CONV_WIDTH - 1 - j:CONV_WIDTH - j, cols] * tail_ref[start:start + rb, cols]
                act = acc * jax.nn.sigmoid(acc)
                mqk_ref[r0:r0 + rb, cols] = (act if c0 < M else act * k_scale).astype(BF16)
        rope_store(proj(0, A), q_ref)
        rope_store(proj(A, A), k_ref)
        v_ref[rows, :] = proj(2 * A, A).astype(BF16)
        mv_ref[rows, :] = proj(3 * A + 2 * M, M).astype(BF16)
        mo_ref[rows, :] = proj(3 * A + 3 * M, M).astype(BF16)
        gates = jnp.dot(u, wg_ref[...], preferred_element_type=F32) + gb_ref[...]
        gt_ref[:, rows] = gates.T[:2 * MLSTM_HEADS, :]
    tail_ref[:CONV_HALO, :] = tail_ref[tm:tm + CONV_HALO, :]


def _in_proj(x, g, w_in_t, gate_b, cos_t, sin_t, conv_w, conv_b, tm=1024, sub=256):
    B, S, D = x.shape
    grid = (B, S // tm)
    tok = lambda width: pl.BlockSpec((None, tm, width), lambda b, i: (b, i, 0))
    attn_shape = jax.ShapeDtypeStruct((B, S, ATTN_WIDTH), BF16)
    return pl.pallas_call(
        functools.partial(_in_proj_kernel, tm=tm, sub=sub),
        name="in_proj",
        grid=grid,
        in_specs=[tok(D), _resident((1, D)), _resident(w_in_t.shape),
                  _resident((1, LANES)), _resident(cos_t.shape), _resident(sin_t.shape),
                  _resident(conv_w.shape), _resident(conv_b.shape)],
        out_specs=[tok(ATTN_WIDTH), tok(ATTN_WIDTH), tok(ATTN_WIDTH),
                   tok(2 * MLSTM_WIDTH), tok(MLSTM_WIDTH), tok(MLSTM_WIDTH),
                   pl.BlockSpec((None, 2 * MLSTM_HEADS, tm), lambda b, i: (b, 0, i))],
        out_shape=[attn_shape, attn_shape, attn_shape,
                   jax.ShapeDtypeStruct((B, S, 2 * MLSTM_WIDTH), BF16),
                   jax.ShapeDtypeStruct((B, S, MLSTM_WIDTH), BF16),
                   jax.ShapeDtypeStruct((B, S, MLSTM_WIDTH), BF16),
                   jax.ShapeDtypeStruct((B, 2 * MLSTM_HEADS, S), F32)],
        scratch_shapes=[pltpu.VMEM((CONV_HALO + tm, 2 * MLSTM_WIDTH), F32),
                        pltpu.VMEM((D, MAIN_WIDTH), BF16), pltpu.VMEM((D, LANES), BF16)],
        compiler_params=pltpu.CompilerParams(
            dimension_semantics=("arbitrary", "arbitrary"), vmem_limit_bytes=VMEM_LIMIT),
    )(x, g, w_in_t, gate_b, cos_t, sin_t, conv_w, conv_b)


def _attn_kernel(q_ref, k_ref, v_ref, o_ref, stage, s4, q4, k4, v4, q16, k16, v16,
                 acc1, acc4, acc16, sm1, sm4, sm16, sd1, sd4, sd16, bias_ref, *, seq):
    blk = ATTN_BLOCK
    d4, d16 = DILATIONS[1], DILATIONS[2]
    assert DILATIONS[0] == 1 and d16 == d4 * d4
    qi = lax.broadcasted_iota(jnp.int32, (2 * blk, 2 * blk), 0) % blk
    ki = lax.broadcasted_iota(jnp.int32, (2 * blk, 2 * blk), 1)
    dist = qi + blk - ki
    valid = (dist >= 0) & (dist <= blk)
    bias_ref[1] = jnp.where(valid, 0.0, NEG_INF)
    bias_ref[0] = jnp.where(valid & (ki >= blk), 0.0, NEG_INF)

    head0 = lax.broadcasted_iota(jnp.int32, (blk, LANES), 1) < ATTN_HEAD_DIM
    ones_kv = jnp.ones((2 * blk, LANES), BF16)

    conv_rows = 512
    for x_ref, x4, x16 in ((q_ref, q4, q16), (k_ref, k4, k16), (v_ref, v4, v16)):
        def to_f32(c, carry, x_ref=x_ref):
            rows = pl.ds(pl.multiple_of(c * conv_rows, conv_rows), conv_rows)
            stage[rows, :] = x_ref[rows, :].astype(F32)
            return carry
        lax.fori_loop(0, seq // conv_rows, to_f32, 0)

        def by4(c, carry, x4=x4):
            l0 = pl.multiple_of(c * blk, blk)
            for r in range(d4):
                rows = stage[pl.ds(l0 * d4 + r, blk, stride=d4), :]
                s4[r, pl.ds(l0, blk), :] = rows
                x4[pl.ds(l0, blk), r * LANES:(r + 1) * LANES] = rows.astype(BF16)
            return carry
        lax.fori_loop(0, seq // d4 // blk, by4, 0)

        def by16(c, carry, x16=x16):
            l0 = pl.multiple_of(c * 64, 64)
            for r in range(d16):
                x16[pl.ds(l0, 64), r * LANES:(r + 1) * LANES] = (
                    s4[r % d4, pl.ds(l0 * d4 + r // d4, 64, stride=d4), :].astype(BF16))
            return carry
        lax.fori_loop(0, seq // d16 // 64, by16, 0)

    patterns = ((q_ref, k_ref, v_ref, acc1, sm1, sd1, 1),
                (q4, k4, v4, acc4, sm4, sd4, d4),
                (q16, k16, v16, acc16, sm16, sd16, d16))

    def result_rows(ref, row0, r, d):
        if d == 1:
            return ref.at[pl.ds(row0, blk), :]
        if d == d4:
            return ref.at[r, pl.ds(row0, blk), :]
        return ref.at[r % d4, pl.ds(row0 * d4 + r // d4, blk, stride=d4), :]

    def view_lanes(r, d):
        return slice(0, LANES) if d == 1 else pl.ds(pl.multiple_of(r * LANES, LANES), LANES)

    for (qv, kv, vv_ref, acc_ref, sm_ref, sd_ref, d) in patterns:
        nb = seq // d // blk

        def block(i, carry, qv=qv, kv=kv, vv_ref=vv_ref, acc_ref=acc_ref, sm_ref=sm_ref, sd_ref=sd_ref, d=d, nb=nb):
            r = i // nb
            n = i % nb
            row0 = pl.multiple_of(n * blk, blk)
            prow0 = pl.multiple_of(jnp.maximum(n - 1, 0) * blk, blk)
            lanes = view_lanes(r, d)
            q = qv[pl.ds(row0, blk), lanes]
            zero = jnp.zeros_like(q)
            q2 = jnp.concatenate([jnp.where(head0, q, zero), jnp.where(head0, zero, q)], axis=0)
            kk = jnp.concatenate([kv[pl.ds(prow0, blk), lanes], kv[pl.ds(row0, blk), lanes]], axis=0)
            vv = jnp.concatenate([vv_ref[pl.ds(prow0, blk), lanes], vv_ref[pl.ds(row0, blk), lanes]], axis=0)
            s = lax.dot_general(q2, kk, (((1,), (1,)), ((), ())), preferred_element_type=F32)
            s = s + bias_ref[jnp.minimum(n, 1)]
            m = jnp.max(s, axis=1, keepdims=True)
            p = jnp.exp2(s - m).astype(BF16)
            res = jnp.dot(p, jnp.concatenate([vv, ones_kv], axis=1), preferred_element_type=F32)
            result_rows(acc_ref, row0, r, d)[...] = jnp.where(head0, res[:blk, :LANES], res[blk:, :LANES])
            result_rows(sm_ref, row0, r, d)[...] = jnp.where(head0, m[:blk], m[blk:])
            result_rows(sd_ref, row0, r, d)[...] = jnp.where(head0, res[:blk, LANES:], res[blk:, LANES:])
            return carry

        lax.fori_loop(0, d * nb, block, 0, unroll=ATTN_UNROLL)

    rows_m = 256

    def mix(c, carry):
        l0 = pl.multiple_of(c * rows_m, rows_m)
        com = pl.ds(l0, rows_m)
        for r in range(d4):
            tok = pl.ds(l0 * d4 + r, rows_m, stride=d4)
            m1, m4, m16 = sm1[tok, :], sm4[r, com, :], sm16[r, com, :]
            top = jnp.maximum(jnp.maximum(m1, m4), m16)
            e1, e4, e16 = jnp.exp2(m1 - top), jnp.exp2(m4 - top), jnp.exp2(m16 - top)
            den = e1 * sd1[tok, :] + e4 * sd4[r, com, :] + e16 * sd16[r, com, :]
            num = e1 * acc1[tok, :] + e4 * acc4[r, com, :] + e16 * acc16[r, com, :]
            stage[tok, :] = num * (1.0 / den)
        return carry

    lax.fori_loop(0, seq // d4 // rows_m, mix, 0)

    def emit(c, carry):
        rows = pl.ds(pl.multiple_of(c * conv_rows, conv_rows), conv_rows)
        o_ref[rows, :] = stage[rows, :].astype(BF16)
        return carry

    lax.fori_loop(0, seq // conv_rows, emit, 0)


def _dilated_attention(q, k, v):
    B, S, W = q.shape
    d4 = DILATIONS[1]
    spec = pl.BlockSpec((None, S, LANES), lambda b, h: (b, 0, h))
    token = pltpu.VMEM((S, LANES), F32)
    by4 = pltpu.VMEM((d4, S // d4, LANES), F32)
    scratch = ([token, by4]
               + [pltpu.VMEM((S // d, d * LANES), BF16) for d in DILATIONS[1:] for _ in range(3)]
               + [token, by4, by4] * 3
               + [pltpu.VMEM((2, 2 * ATTN_BLOCK, 2 * ATTN_BLOCK), F32)])
    return pl.pallas_call(
        functools.partial(_attn_kernel, seq=S),
        name="dilated_attention",
        grid=(B, W // LANES),
        in_specs=[spec, spec, spec],
        out_specs=spec,
        out_shape=jax.ShapeDtypeStruct((B, S, W), BF16),
        scratch_shapes=scratch,
        compiler_params=pltpu.CompilerParams(
            dimension_semantics=("parallel", "parallel"), vmem_limit_bytes=VMEM_LIMIT),
    )(q, k, v)


def _gate_scans(gt_ref, a_ref, b_ref, col_ref):
    L, H = MLSTM_CHUNK, MLSTM_HEADS
    rows = gt_ref.shape[1]
    gates = gt_ref[...]
    f_pre = gates[H:]
    b = (jnp.minimum(f_pre, 0.0) - jnp.log(1.0 + jnp.exp(-jnp.abs(f_pre)))) * LOG2E
    pos = lax.broadcasted_iota(jnp.int32, (H, rows), 1) % L
    for step in [1 << e for e in range(L.bit_length() - 1)]:
        b = b + jnp.where(pos >= step, pltpu.roll(b, step, axis=1), 0.0)
    a = gates[:H] * LOG2E - b
    amax = a
    for step in [1 << e for e in range(L.bit_length() - 1)]:
        amax = jnp.maximum(amax, jnp.where(pos >= step, pltpu.roll(amax, step, axis=1), NEG_INF))
    a_ref[...] = a
    b_ref[...] = b
    col_ref[...] = jnp.concatenate([b, amax, a, jnp.zeros((LANES - 3 * H, rows), F32)], axis=0).T


def _mlstm_tile(qk_ref, v_ref, mo_ref, ng_ref, out_ref, c_ref, m_ref, a_ref, b_ref, col_ref, *, rows, base, fresh):
    L, H, dh = MLSTM_CHUNK, MLSTM_HEADS, MLSTM_HEAD_DIM
    causal = lax.broadcasted_iota(jnp.int32, (L, L), 1) <= lax.broadcasted_iota(jnp.int32, (L, L), 0)
    ones_blk = jnp.ones((L, dh), BF16)
    wide = lambda x: jnp.concatenate([x, x], axis=1)

    def head_chunk(h, r0):
        cols = slice(h * dh, (h + 1) * dh)
        tok = slice(r0, r0 + L)
        q_b = qk_ref[tok, cols]
        k_b = qk_ref[tok, MLSTM_WIDTH + h * dh:MLSTM_WIDTH + (h + 1) * dh]
        v_b = v_ref[tok, cols]

        pos = pl.ds(pl.multiple_of(base + r0, L), L)
        column = lambda j: jnp.broadcast_to(col_ref[pos, j:j + 1], (L, dh))
        b_rep, amax_rep, a_rep = column(h), column(H + h), column(2 * H + h)
        a_row = a_ref[h:h + 1, pos]
        b_last = b_ref[h:h + 1, pos][:, L - 1:L]
        c_prev = c_ref[h]
        m_prev = m_ref[h, 0:1, :]
        if r0 == 0 and fresh is not None:
            c_prev = jnp.where(fresh, 0.0, c_prev)
            m_prev = jnp.where(fresh, 0.0, m_prev)
        mx = jnp.maximum(m_prev, amax_rep)

        w_intra = jnp.exp2(jnp.where(causal, a_row - wide(mx), NEG_INF))
        qk = lax.dot_general(q_b, k_b, (((1,), (1,)), ((), ())), preferred_element_type=F32) * w_intra
        res = (wide(jnp.exp2(m_prev - mx)) * jnp.dot(q_b, c_prev.astype(BF16), preferred_element_type=F32)
               + jnp.dot(qk.astype(BF16), jnp.concatenate([v_b, ones_blk], axis=1), preferred_element_type=F32))
        hid = res[:, :dh] / jnp.maximum(jnp.abs(res[:, dh:]), jnp.exp2(-(b_rep + mx)))

        m_new = b_last + jnp.maximum(m_prev, jnp.max(a_row, axis=1, keepdims=True))
        decay = jnp.exp2(b_last + m_prev - m_new)
        ws = jnp.exp2(b_last + a_rep - m_new)
        wv = jnp.concatenate([(v_b.astype(F32) * ws).astype(BF16), ws.astype(BF16)], axis=1)
        c_ref[h] = wide(decay) * c_prev + lax.dot_general(k_b, wv, (((0,), (0,)), ((), ())),
                                                          preferred_element_type=F32)
        m_ref[h] = jnp.broadcast_to(m_new, m_ref.shape[1:])

        hn = _rms(hid, ng_ref[:, cols])
        out_ref[tok, cols] = (jax.nn.sigmoid(mo_ref[tok, cols].astype(F32)) * hn).astype(BF16)

    for r0 in range(0, rows, L):
        for h in range(H):
            head_chunk(h, r0)


def _mlstm_tail_kernel(x_ref, at_ref, p_ref, gt_ref, qk0_ref, v0_ref, mo0_ref, qk_ref, v_ref, mo_ref,
                       ng_ref, wo_ref, g_mlp_ref, wu_ref, wd_ref, g_ple_ref, wpg_ref, wple_ref, g_fin_ref, o_ref,
                       ml_ref, c_ref, m_ref, a_ref, b_ref, col_ref, *, tm, sub, ff_chunk, final_norm):
    state = (c_ref, m_ref, a_ref, b_ref, col_ref)
    n = pl.num_programs(1)
    t = pl.program_id(0) * n + pl.program_id(1)
    nxt = jnp.minimum(t + 1, pl.num_programs(0) * n - 1)
    nxt_tile = nxt % n

    @pl.when(t == 0)
    def _():
        c_ref[...] = jnp.zeros_like(c_ref)
        m_ref[...] = jnp.zeros_like(m_ref)
        _gate_scans(gt_ref, a_ref, b_ref, col_ref)
        _mlstm_tile(qk0_ref, v0_ref, mo0_ref, ng_ref, ml_ref, *state, rows=tm, base=0, fresh=None)

    @pl.when((nxt_tile == 0) & (nxt > t))
    def _():
        _gate_scans(gt_ref, a_ref, b_ref, col_ref)

    ml = ml_ref[...]
    _mlstm_tile(qk_ref, v_ref, mo_ref, ng_ref, ml_ref, *state, rows=tm, base=nxt_tile * tm, fresh=nxt_tile == 0)

    for s0 in range(0, tm, sub):
        rows = slice(s0, s0 + sub)
        mix = jnp.concatenate([at_ref[rows, :], ml[rows, :]], axis=1)
        h = x_ref[rows, :] + jnp.dot(mix, wo_ref[...], preferred_element_type=F32)

        u = _rms(h, g_mlp_ref[...]).astype(BF16)
        for c0 in range(0, D_FF, ff_chunk):
            a = jnp.dot(u, wu_ref[:, c0:c0 + ff_chunk], preferred_element_type=F32)
            a = jnp.square(jnp.maximum(a, 0.0)).astype(BF16)
            h = h + jnp.dot(a, wd_ref[c0:c0 + ff_chunk, :], preferred_element_type=F32)

        gate = jax.nn.sigmoid(jnp.dot(_rms(h, g_ple_ref[...]).astype(BF16), wpg_ref[...],
                                      preferred_element_type=F32))
        h = h + gate * jnp.dot(p_ref[rows, :].astype(BF16), wple_ref[...], preferred_element_type=F32)
        o_ref[rows, :] = _rms(h, g_fin_ref[...]) if final_norm else h


def _mlstm_tail(x, attn, p, layer, mqk, mv, mo, gates_t, norm_g, w_out, g_mlp, w_up, w_down, g_ple, w_pg, w_ple,
                g_fin, final_norm, tm=512, sub=512, ff_chunk=1024):
    B, S, D = x.shape
    H, dh, W = MLSTM_HEADS, MLSTM_HEAD_DIM, MLSTM_WIDTH
    n = S // tm
    tok = lambda width: pl.BlockSpec((None, tm, width), lambda b, i: (b, i, 0))
    first = lambda width: pl.BlockSpec((None, tm, width), lambda b, i: (0, 0, 0))

    def following(b, i):
        t = jnp.minimum(b * n + i + 1, B * n - 1)
        return t // n, t % n

    nxt = lambda width: pl.BlockSpec((None, tm, width), lambda b, i: (*following(b, i), 0))
    p_spec = pl.BlockSpec((None, None, tm, PLE_DIM), lambda b, i: (layer, b, i, 0))
    gt_spec = pl.BlockSpec((None, 2 * H, S), lambda b, i: (following(b, i)[0], 0, 0))
    return pl.pallas_call(
        functools.partial(_mlstm_tail_kernel, tm=tm, sub=sub, ff_chunk=ff_chunk, final_norm=final_norm),
        name="mlstm_tail",
        grid=(B, n),
        in_specs=[tok(D), tok(ATTN_WIDTH), p_spec, gt_spec,
                  first(2 * W), first(W), first(W),
                  nxt(2 * W), nxt(W), nxt(W), _resident(norm_g.shape),
                  _resident(w_out.shape), _resident((1, D)), _resident(w_up.shape), _resident(w_down.shape),
                  _resident((1, D)), _resident(w_pg.shape), _resident(w_ple.shape), _resident((1, D))],
        out_specs=tok(D),
        out_shape=jax.ShapeDtypeStruct((B, S, D), F32),
        scratch_shapes=[pltpu.VMEM((tm, W), BF16),
                        pltpu.VMEM((H, dh, 2 * dh), F32), pltpu.VMEM((H, 8, LANES), F32),
                        pltpu.VMEM((H, S), F32), pltpu.VMEM((H, S), F32), pltpu.VMEM((S, LANES), F32)],
        compiler_params=pltpu.CompilerParams(
            dimension_semantics=("arbitrary", "arbitrary"), vmem_limit_bytes=VMEM_LIMIT),
    )(x, attn, p, gates_t, mqk, mv, mo, mqk, mv, mo, norm_g,
      w_out, g_mlp, w_up, w_down, g_ple, w_pg, w_ple, g_fin)


def _rope_tables(seq):
    half = ROPE_DIM // 2
    inv_freq = jnp.power(ROPE_THETA, -jnp.arange(half, dtype=F32) / half)
    ang = jnp.arange(seq, dtype=jnp.int32).astype(F32)[:, None] * inv_freq[None, :]
    cos, sin = jnp.cos(ang), jnp.sin(ang)
    rest = ATTN_HEAD_DIM - ROPE_DIM
    cos_head = jnp.concatenate([cos, cos, jnp.ones((seq, rest), F32)], axis=1)
    sin_head = jnp.concatenate([-sin, sin, jnp.zeros((seq, rest), F32)], axis=1)
    reps = LANES // ATTN_HEAD_DIM
    return jnp.tile(cos_head, (1, reps)), jnp.tile(sin_head, (1, reps))


def kernel(x, p, norm_mix_g, w_in, conv_w, conv_b, gate_b, mlstm_norm_g, w_out, norm_mlp_g, w_up, w_down,
           norm_ple_g, w_ple_gate, w_ple, final_norm_g):
    B, S, D = x.shape
    depth = w_in.shape[0]
    cos_t, sin_t = _rope_tables(S)
    row = lambda v: v.reshape(1, -1).astype(F32)
    n_gate = 2 * MLSTM_HEADS
    h = x
    for layer in range(depth):
        gb = jnp.pad(gate_b[layer].astype(F32), (0, LANES - n_gate)).reshape(1, LANES)
        q, k, v, mqk, mv, mo, gates_t = _in_proj(h, row(norm_mix_g[layer]), jnp.swapaxes(w_in[layer], 0, 1), gb,
                                                 cos_t, sin_t, conv_w[layer].astype(F32), row(conv_b[layer]))
        attn = _dilated_attention(q, k, v)
        h = _mlstm_tail(h, attn, p, layer, mqk, mv, mo, gates_t, row(mlstm_norm_g[layer]),
                        w_out[layer].astype(BF16), row(norm_mlp_g[layer]),
                        w_up[layer].astype(BF16), w_down[layer].astype(BF16), row(norm_ple_g[layer]),
                        w_ple_gate[layer].astype(BF16), w_ple[layer].astype(BF16), row(final_norm_g),
                        final_norm=(layer == depth - 1))
    return h
```

```python
import functools
import math

import jax
import jax.numpy as jnp
from jax import lax
from jax.experimental import pallas as pl
from jax.experimental.pallas import tpu as pltpu

D_MODEL = 1024
ATTN_HEADS = 8
ATTN_HEAD_DIM = 64
ATTN_WIDTH = ATTN_HEADS * ATTN_HEAD_DIM
MLSTM_HEADS = 4
MLSTM_HEAD_DIM = 128
MLSTM_WIDTH = MLSTM_HEADS * MLSTM_HEAD_DIM
DILATIONS = (1, 4, 16)
ATTN_BLOCK = 128
ATTN_UNROLL = 32
ROPE_THETA = 500000.0
ROPE_DIM = ATTN_HEAD_DIM // 4
CONV_WIDTH = 4
D_FF = 4 * D_MODEL
PLE_DIM = 256
RMS_EPS = 1e-6

LANES = 128
HEAD_PAIRS = ATTN_WIDTH // LANES
MAIN_WIDTH = 3 * ATTN_WIDTH + 4 * MLSTM_WIDTH
MLSTM_CHUNK = 256
CONV_HALO = 8
VMEM_LIMIT = 56 * 1024 * 1024

F32 = jnp.float32
BF16 = jnp.bfloat16
NEG_INF = float("-inf")
LOG2E = 1.4426950408889634


def _rms(x, g):
    return x * lax.rsqrt(jnp.mean(x * x, axis=-1, keepdims=True) + RMS_EPS) * g


def _resident(shape):
    return pl.BlockSpec(shape, lambda *_: (0,) * len(shape), pipeline_mode=pl.Buffered(1))


def _in_proj_kernel(x_ref, g_ref, w32_ref, gb_ref, cos_ref, sin_ref, cw_ref, cb_ref,
                    q_ref, k_ref, v_ref, mqk_ref, mv_ref, mo_ref, gt_ref, tail_ref, w_ref, wg_ref, *, tm, sub):
    @pl.when((pl.program_id(0) == 0) & (pl.program_id(1) == 0))
    def _():
        gain = g_ref[...]
        q_scale = LOG2E / math.sqrt(ATTN_HEAD_DIM)
        for c0 in range(0, MAIN_WIDTH, ATTN_WIDTH):
            w = w32_ref[c0:c0 + ATTN_WIDTH, :] * gain
            w_ref[:, c0:c0 + ATTN_WIDTH] = (w * q_scale if c0 == 0 else w).T.astype(BF16)
        n_gate = 2 * MLSTM_HEADS
        w_gates = w32_ref[MAIN_WIDTH:MAIN_WIDTH + n_gate, :] * gain
        wg_ref[...] = jnp.concatenate([w_gates, jnp.zeros((LANES - n_gate, w_gates.shape[1]), F32)],
                                      axis=0).T.astype(BF16)

    @pl.when(pl.program_id(1) == 0)
    def _():
        tail_ref[:, :CONV_HALO, :] = jnp.zeros((tail_ref.shape[0], CONV_HALO, LANES), F32)

    A, M = ATTN_WIDTH, MLSTM_WIDTH
    lane = lax.broadcasted_iota(jnp.int32, (sub, LANES), 1)
    first_half = (lane % ATTN_HEAD_DIM) < (ROPE_DIM // 2)
    k_scale = 1.0 / math.sqrt(MLSTM_HEAD_DIM)
    rb = 128

    for s0 in range(0, tm, sub):
        rows = slice(s0, s0 + sub)
        u = _rms(x_ref[rows, :], 1.0).astype(BF16)
        pos0 = pl.multiple_of(pl.program_id(1) * tm + s0, sub)
        cos = cos_ref[pl.ds(pos0, sub), :]
        sin = sin_ref[pl.ds(pos0, sub), :]

        def proj(c0, width, u=u):
            return jnp.dot(u, w_ref[:, c0:c0 + width], preferred_element_type=F32)

        def rope_store(y, o_ref, cos=cos, sin=sin, rows=rows):
            for j in range(HEAD_PAIRS):
                yj = y[:, j * LANES:(j + 1) * LANES]
                partner = jnp.where(first_half,
                                    pltpu.roll(yj, LANES - ROPE_DIM // 2, axis=1),
                                    pltpu.roll(yj, ROPE_DIM // 2, axis=1))
                o_ref[rows, j * LANES:(j + 1) * LANES] = (yj * cos + partner * sin).astype(BF16)

        y = proj(3 * A, 2 * M)
        for c0 in range(0, 2 * M, LANES):
            cols = slice(c0, c0 + LANES)
            slab = tail_ref.at[c0 // LANES]
            slab[CONV_HALO + s0:CONV_HALO + s0 + sub, :] = y[:, cols]
            for r0 in range(s0, s0 + sub, rb):
                acc = cb_ref[:, cols]
                for j in range(CONV_WIDTH):
                    start = CONV_HALO + r0 - j
                    acc = acc + cw_ref[CONV_WIDTH - 1 - j:CONV_WIDTH - j, cols] * slab[pl.ds(start, rb, stride=1), :]
                act = acc * jax.nn.sigmoid(acc)
                mqk_ref[r0:r0 + rb, cols] = (act if c0 < M else act * k_scale).astype(BF16)
        rope_store(proj(0, A), q_ref)
        rope_store(proj(A, A), k_ref)
        v_ref[rows, :] = proj(2 * A, A).astype(BF16)
        mv_ref[rows, :] = proj(3 * A + 2 * M, M).astype(BF16)
        mo_ref[rows, :] = proj(3 * A + 3 * M, M).astype(BF16)
        gates = jnp.dot(u, wg_ref[...], preferred_element_type=F32) + gb_ref[...]
        gt_ref[:, rows] = gates.T[:2 * MLSTM_HEADS, :]
    tail_ref[:, :CONV_HALO, :] = tail_ref[:, tm:tm + CONV_HALO, :]


def _in_proj(x, g, w_in_t, gate_b, cos_t, sin_t, conv_w, conv_b, tm=1024, sub=256):
    B, S, D = x.shape
    grid = (B, S // tm)
    tok = lambda width: pl.BlockSpec((None, tm, width), lambda b, i: (b, i, 0))
    attn_shape = jax.ShapeDtypeStruct((B, S, ATTN_WIDTH), BF16)
    return pl.pallas_call(
        functools.partial(_in_proj_kernel, tm=tm, sub=sub),
        name="in_proj",
        grid=grid,
        in_specs=[tok(D), _resident((1, D)), _resident(w_in_t.shape),
                  _resident((1, LANES)), _resident(cos_t.shape), _resident(sin_t.shape),
                  _resident(conv_w.shape), _resident(conv_b.shape)],
        out_specs=[tok(ATTN_WIDTH), tok(ATTN_WIDTH), tok(ATTN_WIDTH),
                   tok(2 * MLSTM_WIDTH), tok(MLSTM_WIDTH), tok(MLSTM_WIDTH),
                   pl.BlockSpec((None, 2 * MLSTM_HEADS, tm), lambda b, i: (b, 0, i))],
        out_shape=[attn_shape, attn_shape, attn_shape,
                   jax.ShapeDtypeStruct((B, S, 2 * MLSTM_WIDTH), BF16),
                   jax.ShapeDtypeStruct((B, S, MLSTM_WIDTH), BF16),
                   jax.ShapeDtypeStruct((B, S, MLSTM_WIDTH), BF16),
                   jax.ShapeDtypeStruct((B, 2 * MLSTM_HEADS, S), F32)],
        scratch_shapes=[pltpu.VMEM((2 * MLSTM_WIDTH // LANES, CONV_HALO + tm, LANES), F32),
                        pltpu.VMEM((D, MAIN_WIDTH), BF16), pltpu.VMEM((D, LANES), BF16)],
        compiler_params=pltpu.CompilerParams(
            dimension_semantics=("arbitrary", "arbitrary"), vmem_limit_bytes=VMEM_LIMIT),
    )(x, g, w_in_t, gate_b, cos_t, sin_t, conv_w, conv_b)


def _attn_kernel(q_ref, k_ref, v_ref, o_ref, stage, s4, q4, k4, v4, q16, k16, v16,
                 acc1, acc4, acc16, sm1, sm4, sm16, sd1, sd4, sd16, bias_ref, *, seq):
    blk = ATTN_BLOCK
    d4, d16 = DILATIONS[1], DILATIONS[2]
    assert DILATIONS[0] == 1 and d16 == d4 * d4
    qi = lax.broadcasted_iota(jnp.int32, (2 * blk, 2 * blk), 0) % blk
    ki = lax.broadcasted_iota(jnp.int32, (2 * blk, 2 * blk), 1)
    dist = qi + blk - ki
    valid = (dist >= 0) & (dist <= blk)
    bias_ref[1] = jnp.where(valid, 0.0, NEG_INF)
    bias_ref[0] = jnp.where(valid & (ki >= blk), 0.0, NEG_INF)

    head0 = lax.broadcasted_iota(jnp.int32, (blk, LANES), 1) < ATTN_HEAD_DIM
    ones_kv = jnp.ones((2 * blk, LANES), BF16)

    conv_rows = 512
    for x_ref, x4, x16 in ((q_ref, q4, q16), (k_ref, k4, k16), (v_ref, v4, v16)):
        def to_f32(c, carry, x_ref=x_ref):
            rows = pl.ds(pl.multiple_of(c * conv_rows, conv_rows), conv_rows)
            stage[rows, :] = x_ref[rows, :].astype(F32)
            return carry
        lax.fori_loop(0, seq // conv_rows, to_f32, 0)

        def by4(c, carry, x4=x4):
            l0 = pl.multiple_of(c * blk, blk)
            for r in range(d4):
                rows = stage[pl.ds(l0 * d4 + r, blk, stride=d4), :]
                s4[r, pl.ds(l0, blk), :] = rows
                x4[pl.ds(l0, blk), r * LANES:(r + 1) * LANES] = rows.astype(BF16)
            return carry
        lax.fori_loop(0, seq // d4 // blk, by4, 0)

        def by16(c, carry, x16=x16):
            l0 = pl.multiple_of(c * 64, 64)
            for r in range(d16):
                x16[pl.ds(l0, 64), r * LANES:(r + 1) * LANES] = (
                    s4[r % d4, pl.ds(l0 * d4 + r // d4, 64, stride=d4), :].astype(BF16))
            return carry
        lax.fori_loop(0, seq // d16 // 64, by16, 0)

    patterns = ((q_ref, k_ref, v_ref, acc1, sm1, sd1, 1),
                (q4, k4, v4, acc4, sm4, sd4, d4),
                (q16, k16, v16, acc16, sm16, sd16, d16))

    def result_rows(ref, row0, r, d):
        if d == 1:
            return ref.at[pl.ds(row0, blk), :]
        if d == d4:
            return ref.at[r, pl.ds(row0, blk), :]
        return ref.at[r % d4, pl.ds(row0 * d4 + r // d4, blk, stride=d4), :]

    def view_lanes(r, d):
        return slice(0, LANES) if d == 1 else pl.ds(pl.multiple_of(r * LANES, LANES), LANES)

    for (qv, kv, vv_ref, acc_ref, sm_ref, sd_ref, d) in patterns:
        nb = seq // d // blk

        def block(i, carry, qv=qv, kv=kv, vv_ref=vv_ref, acc_ref=acc_ref, sm_ref=sm_ref, sd_ref=sd_ref, d=d, nb=nb):
            r = i // nb
            n = i % nb
            row0 = pl.multiple_of(n * blk, blk)
            prow0 = pl.multiple_of(jnp.maximum(n - 1, 0) * blk, blk)
            lanes = view_lanes(r, d)
            q = qv[pl.ds(row0, blk), lanes]
            zero = jnp.zeros_like(q)
            q2 = jnp.concatenate([jnp.where(head0, q, zero), jnp.where(head0, zero, q)], axis=0)
            kk = jnp.concatenate([kv[pl.ds(prow0, blk), lanes], kv[pl.ds(row0, blk), lanes]], axis=0)
            vv = jnp.concatenate([vv_ref[pl.ds(prow0, blk), lanes], vv_ref[pl.ds(row0, blk), lanes]], axis=0)
            s = lax.dot_general(q2, kk, (((1,), (1,)), ((), ())), preferred_element_type=F32)
            s = s + bias_ref[jnp.minimum(n, 1)]
            m = jnp.max(s, axis=1, keepdims=True)
            p = jnp.exp2(s - m).astype(BF16)
            res = jnp.dot(p, jnp.concatenate([vv, ones_kv], axis=1), preferred_element_type=F32)
            result_rows(acc_ref, row0, r, d)[...] = jnp.where(head0, res[:blk, :LANES], res[blk:, :LANES])
            result_rows(sm_ref, row0, r, d)[...] = jnp.where(head0, m[:blk], m[blk:])
            result_rows(sd_ref, row0, r, d)[...] = jnp.where(head0, res[:blk, LANES:], res[blk:, LANES:])
            return carry

        lax.fori_loop(0, d * nb, block, 0, unroll=ATTN_UNROLL)

    rows_m = 256

    def mix(c, carry):
        l0 = pl.multiple_of(c * rows_m, rows_m)
        com = pl.ds(l0, rows_m)
        for r in range(d4):
            tok = pl.ds(l0 * d4 + r, rows_m, stride=d4)
            m1, m4, m16 = sm1[tok, :], sm4[r, com, :], sm16[r, com, :]
            top = jnp.maximum(jnp.maximum(m1, m4), m16)
            e1, e4, e16 = jnp.exp2(m1 - top), jnp.exp2(m4 - top), jnp.exp2(m16 - top)
            den = e1 * sd1[tok, :] + e4 * sd4[r, com, :] + e16 * sd16[r, com, :]
            num = e1 * acc1[tok, :] + e4 * acc4[r, com, :] + e16 * acc16[r, com, :]
            stage[tok, :] = num * (1.0 / den)
        return carry

    lax.fori_loop(0, seq // d4 // rows_m, mix, 0)

    def emit(c, carry):
        rows = pl.ds(pl.multiple_of(c * conv_rows, conv_rows), conv_rows)
        o_ref[rows, :] = stage[rows, :].astype(BF16)
        return carry

    lax.fori_loop(0, seq // conv_rows, emit, 0)


def _dilated_attention(q, k, v):
    B, S, W = q.shape
    d4 = DILATIONS[1]
    spec = pl.BlockSpec((None, S, LANES), lambda b, h: (b, 0, h))
    token = pltpu.VMEM((S, LANES), F32)
    by4 = pltpu.VMEM((d4, S // d4, LANES), F32)
    scratch = ([token, by4]
               + [pltpu.VMEM((S // d, d * LANES), BF16) for d in DILATIONS[1:] for _ in range(3)]
               + [token, by4, by4] * 3
               + [pltpu.VMEM((2, 2 * ATTN_BLOCK, 2 * ATTN_BLOCK), F32)])
    return pl.pallas_call(
        functools.partial(_attn_kernel, seq=S),
        name="dilated_attention",
        grid=(B, W // LANES),
        in_specs=[spec, spec, spec],
        out_specs=spec,
        out_shape=jax.ShapeDtypeStruct((B, S, W), BF16),
        scratch_shapes=scratch,
        compiler_params=pltpu.CompilerParams(
            dimension_semantics=("parallel", "parallel"), vmem_limit_bytes=VMEM_LIMIT),
    )(q, k, v)


def _gate_scans(gt_ref, a_ref, b_ref, col_ref):
    L, H = MLSTM_CHUNK, MLSTM_HEADS
    rows = gt_ref.shape[1]
    gates = gt_ref[...]
    f_pre = gates[H:]
    b = (jnp.minimum(f_pre, 0.0) - jnp.log(1.0 + jnp.exp(-jnp.abs(f_pre)))) * LOG2E
    pos = lax.broadcasted_iota(jnp.int32, (H, rows), 1) % L
    for step in [1 << e for e in range(L.bit_length() - 1)]:
        b = b + jnp.where(pos >= step, pltpu.roll(b, step, axis=1), 0.0)
    a = gates[:H] * LOG2E - b
    amax = a
    for step in [1 << e for e in range(L.bit_length() - 1)]:
        amax = jnp.maximum(amax, jnp.where(pos >= step, pltpu.roll(amax, step, axis=1), NEG_INF))
    a_ref[...] = a
    b_ref[...] = b
    col_ref[...] = jnp.concatenate([b, amax, a, jnp.zeros((LANES - 3 * H, rows), F32)], axis=0).T


def _mlstm_tile(qk_ref, v_ref, mo_ref, ng_ref, out_ref, c_ref, m_ref, a_ref, b_ref, col_ref, *, rows, base, fresh):
    L, H, dh = MLSTM_CHUNK, MLSTM_HEADS, MLSTM_HEAD_DIM
    causal = lax.broadcasted_iota(jnp.int32, (L, L), 1) <= lax.broadcasted_iota(jnp.int32, (L, L), 0)
    ones_blk = jnp.ones((L, dh), BF16)
    wide = lambda x: jnp.concatenate([x, x], axis=1)

    def head_chunk(h, r0):
        cols = slice(h * dh, (h + 1) * dh)
        tok = slice(r0, r0 + L)
        q_b = qk_ref[tok, cols]
        k_b = qk_ref[tok, MLSTM_WIDTH + h * dh:MLSTM_WIDTH + (h + 1) * dh]
        v_b = v_ref[tok, cols]

        pos = pl.ds(pl.multiple_of(base + r0, L), L)
        column = lambda j: jnp.broadcast_to(col_ref[pos, j:j + 1], (L, dh))
        b_rep, amax_rep, a_rep = column(h), column(H + h), column(2 * H + h)
        a_row = a_ref[h:h + 1, pos]
        b_last = b_ref[h:h + 1, pos][:, L - 1:L]
        c_prev = c_ref[h]
        m_prev = m_ref[h, 0:1, :]
        if r0 == 0 and fresh is not None:
            c_prev = jnp.where(fresh, 0.0, c_prev)
            m_prev = jnp.where(fresh, 0.0, m_prev)
        mx = jnp.maximum(m_prev, amax_rep)

        w_intra = jnp.exp2(jnp.where(causal, a_row - wide(mx), NEG_INF))
        qk = lax.dot_general(q_b, k_b, (((1,), (1,)), ((), ())), preferred_element_type=F32) * w_intra
        res = (wide(jnp.exp2(m_prev - mx)) * jnp.dot(q_b, c_prev.astype(BF16), preferred_element_type=F32)
               + jnp.dot(qk.astype(BF16), jnp.concatenate([v_b, ones_blk], axis=1), preferred_element_type=F32))
        hid = res[:, :dh] / jnp.maximum(jnp.abs(res[:, dh:]), jnp.exp2(-(b_rep + mx)))

        m_new = b_last + jnp.maximum(m_prev, jnp.max(a_row, axis=1, keepdims=True))
        decay = jnp.exp2(b_last + m_prev - m_new)
        ws = jnp.exp2(b_last + a_rep - m_new)
        wv = jnp.concatenate([(v_b.astype(F32) * ws).astype(BF16), ws.astype(BF16)], axis=1)
        c_ref[h] = wide(decay) * c_prev + lax.dot_general(k_b, wv, (((0,), (0,)), ((), ())),
                                                          preferred_element_type=F32)
        m_ref[h] = jnp.broadcast_to(m_new, m_ref.shape[1:])

        hn = _rms(hid, ng_ref[:, cols])
        out_ref[tok, cols] = (jax.nn.sigmoid(mo_ref[tok, cols].astype(F32)) * hn).astype(BF16)

    for r0 in range(0, rows, L):
        for h in range(H):
            head_chunk(h, r0)


def _mlstm_tail_kernel(x_ref, at_ref, p_ref, gt_ref, qk0_ref, v0_ref, mo0_ref, qk_ref, v_ref, mo_ref,
                       ng_ref, wo_ref, g_mlp_ref, wu_ref, wd_ref, g_ple_ref, wpg_ref, wple_ref, g_fin_ref, o_ref,
                       ml_ref, c_ref, m_ref, a_ref, b_ref, col_ref, *, tm, ff_chunk, final_norm):
    state = (c_ref, m_ref, a_ref, b_ref, col_ref)
    n = pl.num_programs(1)
    t = pl.program_id(0) * n + pl.program_id(1)
    nxt = jnp.minimum(t + 1, pl.num_programs(0) * n - 1)
    nxt_tile = nxt % n

    @pl.when(t == 0)
    def _():
        c_ref[...] = jnp.zeros_like(c_ref)
        m_ref[...] = jnp.zeros_like(m_ref)
        _gate_scans(gt_ref, a_ref, b_ref, col_ref)
        _mlstm_tile(qk0_ref, v0_ref, mo0_ref, ng_ref, ml_ref, *state, rows=tm, base=0, fresh=None)

    @pl.when((nxt_tile == 0) & (nxt > t))
    def _():
        _gate_scans(gt_ref, a_ref, b_ref, col_ref)

    ml = ml_ref[...]
    _mlstm_tile(qk_ref, v_ref, mo_ref, ng_ref, ml_ref, *state, rows=tm, base=nxt_tile * tm, fresh=nxt_tile == 0)

    mix = jnp.concatenate([at_ref[...], ml], axis=1)
    h = x_ref[...] + jnp.dot(mix, wo_ref[...], preferred_element_type=F32)

    u = _rms(h, g_mlp_ref[...]).astype(BF16)
    for c0 in range(0, D_FF, ff_chunk):
        a = jnp.dot(u, wu_ref[:, c0:c0 + ff_chunk], preferred_element_type=F32)
        a = jnp.square(jnp.maximum(a, 0.0)).astype(BF16)
        h = h + jnp.dot(a, wd_ref[c0:c0 + ff_chunk, :], preferred_element_type=F32)

    gate = jax.nn.sigmoid(jnp.dot(_rms(h, g_ple_ref[...]).astype(BF16), wpg_ref[...], preferred_element_type=F32))
    h = h + gate * jnp.dot(p_ref[...].astype(BF16), wple_ref[...], preferred_element_type=F32)
    o_ref[...] = _rms(h, g_fin_ref[...]) if final_norm else h


def _mlstm_tail(x, attn, p, layer, mqk, mv, mo, gates_t, norm_g, w_out, g_mlp, w_up, w_down, g_ple, w_pg, w_ple,
                g_fin, final_norm, tm=512, ff_chunk=1024):
    B, S, D = x.shape
    H, dh, W = MLSTM_HEADS, MLSTM_HEAD_DIM, MLSTM_WIDTH
    n = S // tm
    tok = lambda width: pl.BlockSpec((None, tm, width), lambda b, i: (b, i, 0))
    first = lambda width: pl.BlockSpec((None, tm, width), lambda b, i: (0, 0, 0))

    def following(b, i):
        t = jnp.minimum(b * n + i + 1, B * n - 1)
        return t // n, t % n

    nxt = lambda width: pl.BlockSpec((None, tm, width), lambda b, i: (*following(b, i), 0))
    p_spec = pl.BlockSpec((None, None, tm, PLE_DIM), lambda b, i: (layer, b, i, 0))
    gt_spec = pl.BlockSpec((None, 2 * H, S), lambda b, i: (following(b, i)[0], 0, 0))
    return pl.pallas_call(
        functools.partial(_mlstm_tail_kernel, tm=tm, ff_chunk=ff_chunk, final_norm=final_norm),
        name="mlstm_tail",
        grid=(B, n),
        in_specs=[tok(D), tok(ATTN_WIDTH), p_spec, gt_spec,
                  first(2 * W), first(W), first(W),
                  nxt(2 * W), nxt(W), nxt(W), _resident(norm_g.shape),
                  _resident(w_out.shape), _resident((1, D)), _resident(w_up.shape), _resident(w_down.shape),
                  _resident((1, D)), _resident(w_pg.shape), _resident(w_ple.shape), _resident((1, D))],
        out_specs=tok(D),
        out_shape=jax.ShapeDtypeStruct((B, S, D), F32),
        scratch_shapes=[pltpu.VMEM((tm, W), BF16),
                        pltpu.VMEM((H, dh, 2 * dh), F32), pltpu.VMEM((H, 8, LANES), F32),
                        pltpu.VMEM((H, S), F32), pltpu.VMEM((H, S), F32), pltpu.VMEM((S, LANES), F32)],
        compiler_params=pltpu.CompilerParams(
            dimension_semantics=("arbitrary", "arbitrary"), vmem_limit_bytes=VMEM_LIMIT),
    )(x, attn, p, gates_t, mqk, mv, mo, mqk, mv, mo, norm_g,
      w_out, g_mlp, w_up, w_down, g_ple, w_pg, w_ple, g_fin)


def _rope_tables(seq):
    half = ROPE_DIM // 2
    inv_freq = jnp.power(ROPE_THETA, -jnp.arange(half, dtype=F32) / half)
    ang = jnp.arange(seq, dtype=jnp.int32).astype(F32)[:, None] * inv_freq[None, :]
    cos, sin = jnp.cos(ang), jnp.sin(ang)
    rest = ATTN_HEAD_DIM - ROPE_DIM
    cos_head = jnp.concatenate([cos, cos, jnp.ones((seq, rest), F32)], axis=1)
    sin_head = jnp.concatenate([-sin, sin, jnp.zeros((seq, rest), F32)], axis=1)
    reps = LANES // ATTN_HEAD_DIM
    return jnp.tile(cos_head, (1, reps)), jnp.tile(sin_head, (1, reps))


def kernel(x, p, norm_mix_g, w_in, conv_w, conv_b, gate_b, mlstm_norm_g, w_out, norm_mlp_g, w_up, w_down,
           norm_ple_g, w_ple_gate, w_ple, final_norm_g):
    B, S, D = x.shape
    depth = w_in.shape[0]
    cos_t, sin_t = _rope_tables(S)
    row = lambda v: v.reshape(1, -1).astype(F32)
    n_gate = 2 * MLSTM_HEADS
    h = x
    for layer in range(depth):
        gb = jnp.pad(gate_b[layer].astype(F32), (0, LANES - n_gate)).reshape(1, LANES)
        q, k, v, mqk, mv, mo, gates_t = _in_proj(h, row(norm_mix_g[layer]), jnp.swapaxes(w_in[layer], 0, 1), gb,
                                                 cos_t, sin_t, conv_w[layer].astype(F32), row(conv_b[layer]))
        attn = _dilated_attention(q, k, v)
        h = _mlstm_tail(h, attn, p, layer, mqk, mv, mo, gates_t, row(mlstm_norm_g[layer]),
                        w_out[layer].astype(BF16), row(norm_mlp_g[layer]),
                        w_up[layer].astype(BF16), w_down[layer].astype(BF16), row(norm_ple_g[layer]),
                        w_ple_gate[layer].astype(BF16), w_ple[layer].astype(BF16), row(final_norm_g),
                        final_norm=(layer == depth - 1))
    return h
```

```python
import functools
import math

import jax
import jax.numpy as jnp
from jax import lax
from jax.experimental import pallas as pl
from jax.experimental.pallas import tpu as pltpu

D_MODEL = 1024
ATTN_HEADS = 8
ATTN_HEAD_DIM = 64
ATTN_WIDTH = ATTN_HEADS * ATTN_HEAD_DIM
MLSTM_HEADS = 4
MLSTM_HEAD_DIM = 128
MLSTM_WIDTH = MLSTM_HEADS * MLSTM_HEAD_DIM
DILATIONS = (1, 4, 16)
ATTN_BLOCK = 128
ATTN_UNROLL = 32
ROPE_THETA = 500000.0
ROPE_DIM = ATTN_HEAD_DIM // 4
CONV_WIDTH = 4
D_FF = 4 * D_MODEL
PLE_DIM = 256
RMS_EPS = 1e-6

LANES = 128
HEAD_PAIRS = ATTN_WIDTH // LANES
MAIN_WIDTH = 3 * ATTN_WIDTH + 4 * MLSTM_WIDTH
MLSTM_CHUNK = 256
CONV_HALO = 8
VMEM_LIMIT = 56 * 1024 * 1024

F32 = jnp.float32
BF16 = jnp.bfloat16
NEG_INF = float("-inf")
LOG2E = 1.4426950408889634


def _rms(x, g):
    return x * lax.rsqrt(jnp.mean(x * x, axis=-1, keepdims=True) + RMS_EPS) * g


def _resident(shape):
    return pl.BlockSpec(shape, lambda *_: (0,) * len(shape), pipeline_mode=pl.Buffered(1))


def _in_proj_kernel(x_ref, g_ref, w32_ref, gb_ref, cos_ref, sin_ref, cw_ref, cb_ref,
                    q_ref, k_ref, v_ref, mqk_ref, mv_ref, mo_ref, gt_ref, tail_ref, w_ref, wg_ref, *, tm, sub):
    @pl.when((pl.program_id(0) == 0) & (pl.program_id(1) == 0))
    def _():
        gain = g_ref[...]
        q_scale = LOG2E / math.sqrt(ATTN_HEAD_DIM)
        for c0 in range(0, MAIN_WIDTH, ATTN_WIDTH):
            w = w32_ref[c0:c0 + ATTN_WIDTH, :] * gain
            w_ref[:, c0:c0 + ATTN_WIDTH] = (w * q_scale if c0 == 0 else w).T.astype(BF16)
        n_gate = 2 * MLSTM_HEADS
        w_gates = w32_ref[MAIN_WIDTH:MAIN_WIDTH + n_gate, :] * gain
        wg_ref[...] = jnp.concatenate([w_gates, jnp.zeros((LANES - n_gate, w_gates.shape[1]), F32)],
                                      axis=0).T.astype(BF16)

    @pl.when(pl.program_id(1) == 0)
    def _():
        tail_ref[:, :CONV_HALO, :] = jnp.zeros((tail_ref.shape[0], CONV_HALO, LANES), F32)

    A, M = ATTN_WIDTH, MLSTM_WIDTH
    lane = lax.broadcasted_iota(jnp.int32, (sub, LANES), 1)
    first_half = (lane % ATTN_HEAD_DIM) < (ROPE_DIM // 2)
    k_scale = 1.0 / math.sqrt(MLSTM_HEAD_DIM)
    rb = 128

    for s0 in range(0, tm, sub):
        rows = slice(s0, s0 + sub)
        u = _rms(x_ref[rows, :], 1.0).astype(BF16)
        pos0 = pl.multiple_of(pl.program_id(1) * tm + s0, sub)
        cos = cos_ref[pl.ds(pos0, sub), :]
        sin = sin_ref[pl.ds(pos0, sub), :]

        def proj(c0, width, u=u):
            return jnp.dot(u, w_ref[:, c0:c0 + width], preferred_element_type=F32)

        def rope_store(y, o_ref, cos=cos, sin=sin, rows=rows):
            for j in range(HEAD_PAIRS):
                yj = y[:, j * LANES:(j + 1) * LANES]
                partner = jnp.where(first_half,
                                    pltpu.roll(yj, LANES - ROPE_DIM // 2, axis=1),
                                    pltpu.roll(yj, ROPE_DIM // 2, axis=1))
                o_ref[rows, j * LANES:(j + 1) * LANES] = (yj * cos + partner * sin).astype(BF16)

        y = proj(3 * A, 2 * M)
        for c0 in range(0, 2 * M, LANES):
            cols = slice(c0, c0 + LANES)
            slab = tail_ref.at[c0 // LANES]
            slab[CONV_HALO + s0:CONV_HALO + s0 + sub, :] = y[:, cols]
            for r0 in range(s0, s0 + sub, rb):
                acc = cb_ref[:, cols]
                for j in range(CONV_WIDTH):
                    start = CONV_HALO + r0 - j
                    acc = acc + cw_ref[CONV_WIDTH - 1 - j:CONV_WIDTH - j, cols] * slab[pl.ds(start, rb, stride=1), :]
                act = acc * jax.nn.sigmoid(acc)
                mqk_ref[r0:r0 + rb, cols] = (act if c0 < M else act * k_scale).astype(BF16)
        rope_store(proj(0, A), q_ref)
        rope_store(proj(A, A), k_ref)
        v_ref[rows, :] = proj(2 * A, A).astype(BF16)
        mv_ref[rows, :] = proj(3 * A + 2 * M, M).astype(BF16)
        mo_ref[rows, :] = proj(3 * A + 3 * M, M).astype(BF16)
        gates = jnp.dot(u, wg_ref[...], preferred_element_type=F32) + gb_ref[...]
        gt_ref[:, rows] = gates.T[:2 * MLSTM_HEADS, :]
    tail_ref[:, :CONV_HALO, :] = tail_ref[:, tm:tm + CONV_HALO, :]


def _in_proj(x, g, w_in_t, gate_b, cos_t, sin_t, conv_w, conv_b, tm=1024, sub=256):
    B, S, D = x.shape
    grid = (B, S // tm)
    tok = lambda width: pl.BlockSpec((None, tm, width), lambda b, i: (b, i, 0))
    attn_shape = jax.ShapeDtypeStruct((B, S, ATTN_WIDTH), BF16)
    return pl.pallas_call(
        functools.partial(_in_proj_kernel, tm=tm, sub=sub),
        name="in_proj",
        grid=grid,
        in_specs=[tok(D), _resident((1, D)), _resident(w_in_t.shape),
                  _resident((1, LANES)), _resident(cos_t.shape), _resident(sin_t.shape),
                  _resident(conv_w.shape), _resident(conv_b.shape)],
        out_specs=[tok(ATTN_WIDTH), tok(ATTN_WIDTH), tok(ATTN_WIDTH),
                   tok(2 * MLSTM_WIDTH), tok(MLSTM_WIDTH), tok(MLSTM_WIDTH),
                   pl.BlockSpec((None, 2 * MLSTM_HEADS, tm), lambda b, i: (b, 0, i))],
        out_shape=[attn_shape, attn_shape, attn_shape,
                   jax.ShapeDtypeStruct((B, S, 2 * MLSTM_WIDTH), BF16),
                   jax.ShapeDtypeStruct((B, S, MLSTM_WIDTH), BF16),
                   jax.ShapeDtypeStruct((B, S, MLSTM_WIDTH), BF16),
                   jax.ShapeDtypeStruct((B, 2 * MLSTM_HEADS, S), F32)],
        scratch_shapes=[pltpu.VMEM((2 * MLSTM_WIDTH // LANES, CONV_HALO + tm, LANES), F32),
                        pltpu.VMEM((D, MAIN_WIDTH), BF16), pltpu.VMEM((D, LANES), BF16)],
        compiler_params=pltpu.CompilerParams(
            dimension_semantics=("arbitrary", "arbitrary"), vmem_limit_bytes=VMEM_LIMIT),
    )(x, g, w_in_t, gate_b, cos_t, sin_t, conv_w, conv_b)


def _attn_kernel(q_ref, k_ref, v_ref, wu32_ref, wd32_ref, wo32_ref, wpg32_ref, o_ref, wu_ref, wd_ref, wo_ref, wpg_ref,
                 stage, s4, q4, k4, v4, q16, k16, v16,
                 acc1, acc4, acc16, sm1, sm4, sm16, sd1, sd4, sd16, bias_ref, *, seq):
    blk = ATTN_BLOCK
    d4, d16 = DILATIONS[1], DILATIONS[2]
    assert DILATIONS[0] == 1 and d16 == d4 * d4
    for src, dst in ((wu32_ref, wu_ref), (wd32_ref, wd_ref), (wo32_ref, wo_ref), (wpg32_ref, wpg_ref)):
        dst[...] = src[...].astype(BF16)
    qi = lax.broadcasted_iota(jnp.int32, (2 * blk, 2 * blk), 0) % blk
    ki = lax.broadcasted_iota(jnp.int32, (2 * blk, 2 * blk), 1)
    dist = qi + blk - ki
    valid = (dist >= 0) & (dist <= blk)
    bias_ref[1] = jnp.where(valid, 0.0, NEG_INF)
    bias_ref[0] = jnp.where(valid & (ki >= blk), 0.0, NEG_INF)

    head0 = lax.broadcasted_iota(jnp.int32, (blk, LANES), 1) < ATTN_HEAD_DIM
    ones_kv = jnp.ones((2 * blk, LANES), BF16)

    conv_rows = 512
    for x_ref, x4, x16 in ((q_ref, q4, q16), (k_ref, k4, k16), (v_ref, v4, v16)):
        def to_f32(c, carry, x_ref=x_ref):
            rows = pl.ds(pl.multiple_of(c * conv_rows, conv_rows), conv_rows)
            stage[rows, :] = x_ref[rows, :].astype(F32)
            return carry
        lax.fori_loop(0, seq // conv_rows, to_f32, 0)

        def by4(c, carry, x4=x4):
            l0 = pl.multiple_of(c * blk, blk)
            for r in range(d4):
                rows = stage[pl.ds(l0 * d4 + r, blk, stride=d4), :]
                s4[r, pl.ds(l0, blk), :] = rows
                x4[pl.ds(l0, blk), r * LANES:(r + 1) * LANES] = rows.astype(BF16)
            return carry
        lax.fori_loop(0, seq // d4 // blk, by4, 0)

        def by16(c, carry, x16=x16):
            l0 = pl.multiple_of(c * 64, 64)
            for r in range(d16):
                x16[pl.ds(l0, 64), r * LANES:(r + 1) * LANES] = (
                    s4[r % d4, pl.ds(l0 * d4 + r // d4, 64, stride=d4), :].astype(BF16))
            return carry
        lax.fori_loop(0, seq // d16 // 64, by16, 0)

    patterns = ((q_ref, k_ref, v_ref, acc1, sm1, sd1, 1),
                (q4, k4, v4, acc4, sm4, sd4, d4),
                (q16, k16, v16, acc16, sm16, sd16, d16))

    def result_rows(ref, row0, r, d):
        if d == 1:
            return ref.at[pl.ds(row0, blk), :]
        if d == d4:
            return ref.at[r, pl.ds(row0, blk), :]
        return ref.at[r % d4, pl.ds(row0 * d4 + r // d4, blk, stride=d4), :]

    def view_lanes(r, d):
        return slice(0, LANES) if d == 1 else pl.ds(pl.multiple_of(r * LANES, LANES), LANES)

    for (qv, kv, vv_ref, acc_ref, sm_ref, sd_ref, d) in patterns:
        nb = seq // d // blk

        def block(i, carry, qv=qv, kv=kv, vv_ref=vv_ref, acc_ref=acc_ref, sm_ref=sm_ref, sd_ref=sd_ref, d=d, nb=nb):
            r = i // nb
            n = i % nb
            row0 = pl.multiple_of(n * blk, blk)
            prow0 = pl.multiple_of(jnp.maximum(n - 1, 0) * blk, blk)
            lanes = view_lanes(r, d)
            q = qv[pl.ds(row0, blk), lanes]
            zero = jnp.zeros_like(q)
            q2 = jnp.concatenate([jnp.where(head0, q, zero), jnp.where(head0, zero, q)], axis=0)
            kk = jnp.concatenate([kv[pl.ds(prow0, blk), lanes], kv[pl.ds(row0, blk), lanes]], axis=0)
            vv = jnp.concatenate([vv_ref[pl.ds(prow0, blk), lanes], vv_ref[pl.ds(row0, blk), lanes]], axis=0)
            s = lax.dot_general(q2, kk, (((1,), (1,)), ((), ())), preferred_element_type=F32)
            s = s + bias_ref[jnp.minimum(n, 1)]
            m = jnp.max(s, axis=1, keepdims=True)
            p = jnp.exp2(s - m).astype(BF16)
            res = jnp.dot(p, jnp.concatenate([vv, ones_kv], axis=1), preferred_element_type=F32)
            result_rows(acc_ref, row0, r, d)[...] = jnp.where(head0, res[:blk, :LANES], res[blk:, :LANES])
            result_rows(sm_ref, row0, r, d)[...] = jnp.where(head0, m[:blk], m[blk:])
            result_rows(sd_ref, row0, r, d)[...] = jnp.where(head0, res[:blk, LANES:], res[blk:, LANES:])
            return carry

        lax.fori_loop(0, d * nb, block, 0, unroll=ATTN_UNROLL)

    rows_m = 256

    def mix(c, carry):
        l0 = pl.multiple_of(c * rows_m, rows_m)
        com = pl.ds(l0, rows_m)
        for r in range(d4):
            tok = pl.ds(l0 * d4 + r, rows_m, stride=d4)
            m1, m4, m16 = sm1[tok, :], sm4[r, com, :], sm16[r, com, :]
            top = jnp.maximum(jnp.maximum(m1, m4), m16)
            e1, e4, e16 = jnp.exp2(m1 - top), jnp.exp2(m4 - top), jnp.exp2(m16 - top)
            den = e1 * sd1[tok, :] + e4 * sd4[r, com, :] + e16 * sd16[r, com, :]
            num = e1 * acc1[tok, :] + e4 * acc4[r, com, :] + e16 * acc16[r, com, :]
            stage[tok, :] = num * (1.0 / den)
        return carry

    lax.fori_loop(0, seq // d4 // rows_m, mix, 0)

    def emit(c, carry):
        rows = pl.ds(pl.multiple_of(c * conv_rows, conv_rows), conv_rows)
        o_ref[rows, :] = stage[rows, :].astype(BF16)
        return carry

    lax.fori_loop(0, seq // conv_rows, emit, 0)


def _dilated_attention(q, k, v, w_up, w_down, w_out, w_pg):
    B, S, W = q.shape
    d4 = DILATIONS[1]
    hp = W // LANES
    steps = B * hp
    d_model, d_ff = w_up.shape
    col_w, row_h, row_s = d_ff // steps, d_ff // steps, d_model // steps
    assert col_w % LANES == 0 and row_s % 16 == 0 and col_w * steps == d_ff and row_s * steps == d_model
    step = lambda b, h: b * hp + h
    w_specs = [pl.BlockSpec((d_model, col_w), lambda b, h: (0, step(b, h))),
               pl.BlockSpec((row_h, d_model), lambda b, h: (step(b, h), 0)),
               pl.BlockSpec((row_s, d_model), lambda b, h: (step(b, h), 0)),
               pl.BlockSpec((row_s, d_model), lambda b, h: (step(b, h), 0))]
    spec = pl.BlockSpec((None, S, LANES), lambda b, h: (b, 0, h))
    token = pltpu.VMEM((S, LANES), F32)
    by4 = pltpu.VMEM((d4, S // d4, LANES), F32)
    scratch = ([token, by4]
               + [pltpu.VMEM((S // d, d * LANES), BF16) for d in DILATIONS[1:] for _ in range(3)]
               + [token, by4, by4] * 3
               + [pltpu.VMEM((2, 2 * ATTN_BLOCK, 2 * ATTN_BLOCK), F32)])
    return pl.pallas_call(
        functools.partial(_attn_kernel, seq=S),
        name="dilated_attention",
        grid=(B, hp),
        in_specs=[spec, spec, spec] + w_specs,
        out_specs=[spec] + w_specs,
        out_shape=[jax.ShapeDtypeStruct((B, S, W), BF16)]
                  + [jax.ShapeDtypeStruct(w.shape, BF16) for w in (w_up, w_down, w_out, w_pg)],
        scratch_shapes=scratch,
        compiler_params=pltpu.CompilerParams(
            dimension_semantics=("parallel", "parallel"), vmem_limit_bytes=VMEM_LIMIT),
    )(q, k, v, w_up, w_down, w_out, w_pg)


def _gate_scans(gt_ref, a_ref, b_ref, col_ref):
    L, H = MLSTM_CHUNK, MLSTM_HEADS
    rows = gt_ref.shape[1]
    gates = gt_ref[...]
    f_pre = gates[H:]
    b = (jnp.minimum(f_pre, 0.0) - jnp.log(1.0 + jnp.exp(-jnp.abs(f_pre)))) * LOG2E
    pos = lax.broadcasted_iota(jnp.int32, (H, rows), 1) % L
    for step in [1 << e for e in range(L.bit_length() - 1)]:
        b = b + jnp.where(pos >= step, pltpu.roll(b, step, axis=1), 0.0)
    a = gates[:H] * LOG2E - b
    amax = a
    for step in [1 << e for e in range(L.bit_length() - 1)]:
        amax = jnp.maximum(amax, jnp.where(pos >= step, pltpu.roll(amax, step, axis=1), NEG_INF))
    a_ref[...] = a
    b_ref[...] = b
    col_ref[...] = jnp.concatenate([b, amax, a, jnp.zeros((LANES - 3 * H, rows), F32)], axis=0).T


def _mlstm_tile(qk_ref, v_ref, mo_ref, ng_ref, out_ref, c_ref, m_ref, a_ref, b_ref, col_ref, *, rows, base, fresh):
    L, H, dh = MLSTM_CHUNK, MLSTM_HEADS, MLSTM_HEAD_DIM
    causal = lax.broadcasted_iota(jnp.int32, (L, L), 1) <= lax.broadcasted_iota(jnp.int32, (L, L), 0)
    ones_blk = jnp.ones((L, dh), BF16)
    wide = lambda x: jnp.concatenate([x, x], axis=1)

    def head_chunk(h, r0):
        cols = slice(h * dh, (h + 1) * dh)
        tok = slice(r0, r0 + L)
        q_b = qk_ref[tok, cols]
        k_b = qk_ref[tok, MLSTM_WIDTH + h * dh:MLSTM_WIDTH + (h + 1) * dh]
        v_b = v_ref[tok, cols]

        pos = pl.ds(pl.multiple_of(base + r0, L), L)
        column = lambda j: jnp.broadcast_to(col_ref[pos, j:j + 1], (L, dh))
        b_rep, amax_rep, a_rep = column(h), column(H + h), column(2 * H + h)
        a_row = a_ref[h:h + 1, pos]
        b_last = b_ref[h:h + 1, pos][:, L - 1:L]
        c_prev = c_ref[h]
        m_prev = m_ref[h, 0:1, :]
        if r0 == 0 and fresh is not None:
            c_prev = jnp.where(fresh, 0.0, c_prev)
            m_prev = jnp.where(fresh, 0.0, m_prev)
        mx = jnp.maximum(m_prev, amax_rep)

        w_intra = jnp.exp2(jnp.where(causal, a_row - wide(mx), NEG_INF))
        qk = lax.dot_general(q_b, k_b, (((1,), (1,)), ((), ())), preferred_element_type=F32) * w_intra
        res = (wide(jnp.exp2(m_prev - mx)) * jnp.dot(q_b, c_prev.astype(BF16), preferred_element_type=F32)
               + jnp.dot(qk.astype(BF16), jnp.concatenate([v_b, ones_blk], axis=1), preferred_element_type=F32))
        hid = res[:, :dh] / jnp.maximum(jnp.abs(res[:, dh:]), jnp.exp2(-(b_rep + mx)))

        m_new = b_last + jnp.maximum(m_prev, jnp.max(a_row, axis=1, keepdims=True))
        decay = jnp.exp2(b_last + m_prev - m_new)
        ws = jnp.exp2(b_last + a_rep - m_new)
        wv = jnp.concatenate([(v_b.astype(F32) * ws).astype(BF16), ws.astype(BF16)], axis=1)
        c_ref[h] = wide(decay) * c_prev + lax.dot_general(k_b, wv, (((0,), (0,)), ((), ())),
                                                          preferred_element_type=F32)
        m_ref[h] = jnp.broadcast_to(m_new, m_ref.shape[1:])

        hn = _rms(hid, ng_ref[:, cols])
        out_ref[tok, cols] = (jax.nn.sigmoid(mo_ref[tok, cols].astype(F32)) * hn).astype(BF16)

    for r0 in range(0, rows, L):
        for h in range(H):
            head_chunk(h, r0)


def _mlstm_tail_kernel(x_ref, at_ref, p_ref, gt_ref, qk0_ref, v0_ref, mo0_ref, qk_ref, v_ref, mo_ref,
                       ng_ref, wo_ref, g_mlp_ref, wu_ref, wd_ref, g_ple_ref, wpg_ref, wple_ref, g_fin_ref, o_ref,
                       ml_ref, c_ref, m_ref, a_ref, b_ref, col_ref, *, tm, ff_chunk, final_norm):
    state = (c_ref, m_ref, a_ref, b_ref, col_ref)
    n = pl.num_programs(1)
    t = pl.program_id(0) * n + pl.program_id(1)
    nxt = jnp.minimum(t + 1, pl.num_programs(0) * n - 1)
    nxt_tile = nxt % n

    @pl.when(t == 0)
    def _():
        c_ref[...] = jnp.zeros_like(c_ref)
        m_ref[...] = jnp.zeros_like(m_ref)
        _gate_scans(gt_ref, a_ref, b_ref, col_ref)
        _mlstm_tile(qk0_ref, v0_ref, mo0_ref, ng_ref, ml_ref, *state, rows=tm, base=0, fresh=None)

    @pl.when((nxt_tile == 0) & (nxt > t))
    def _():
        _gate_scans(gt_ref, a_ref, b_ref, col_ref)

    ml = ml_ref[...]
    _mlstm_tile(qk_ref, v_ref, mo_ref, ng_ref, ml_ref, *state, rows=tm, base=nxt_tile * tm, fresh=nxt_tile == 0)

    mix = jnp.concatenate([at_ref[...], ml], axis=1)
    h = x_ref[...] + jnp.dot(mix, wo_ref[...], preferred_element_type=F32)

    u = _rms(h, g_mlp_ref[...]).astype(BF16)
    for c0 in range(0, D_FF, ff_chunk):
        a = jnp.dot(u, wu_ref[:, c0:c0 + ff_chunk], preferred_element_type=F32)
        a = jnp.square(jnp.maximum(a, 0.0)).astype(BF16)
        h = h + jnp.dot(a, wd_ref[c0:c0 + ff_chunk, :], preferred_element_type=F32)

    gate = jax.nn.sigmoid(jnp.dot(_rms(h, g_ple_ref[...]).astype(BF16), wpg_ref[...], preferred_element_type=F32))
    h = h + gate * jnp.dot(p_ref[...].astype(BF16), wple_ref[...], preferred_element_type=F32)
    o_ref[...] = _rms(h, g_fin_ref[...]) if final_norm else h


def _mlstm_tail(x, attn, p, layer, mqk, mv, mo, gates_t, norm_g, w_out, g_mlp, w_up, w_down, g_ple, w_pg, w_ple,
                g_fin, final_norm, tm=512, ff_chunk=1024):
    B, S, D = x.shape
    H, dh, W = MLSTM_HEADS, MLSTM_HEAD_DIM, MLSTM_WIDTH
    n = S // tm
    tok = lambda width: pl.BlockSpec((None, tm, width), lambda b, i: (b, i, 0))
    first = lambda width: pl.BlockSpec((None, tm, width), lambda b, i: (0, 0, 0))

    def following(b, i):
        t = jnp.minimum(b * n + i + 1, B * n - 1)
        return t // n, t % n

    nxt = lambda width: pl.BlockSpec((None, tm, width), lambda b, i: (*following(b, i), 0))
    p_spec = pl.BlockSpec((None, None, tm, PLE_DIM), lambda b, i: (layer, b, i, 0))
    gt_spec = pl.BlockSpec((None, 2 * H, S), lambda b, i: (following(b, i)[0], 0, 0))
    return pl.pallas_call(
        functools.partial(_mlstm_tail_kernel, tm=tm, ff_chunk=ff_chunk, final_norm=final_norm),
        name="mlstm_tail",
        grid=(B, n),
        in_specs=[tok(D), tok(ATTN_WIDTH), p_spec, gt_spec,
                  first(2 * W), first(W), first(W),
                  nxt(2 * W), nxt(W), nxt(W), _resident(norm_g.shape),
                  _resident(w_out.shape), _resident((1, D)), _resident(w_up.shape), _resident(w_down.shape),
                  _resident((1, D)), _resident(w_pg.shape), _resident(w_ple.shape), _resident((1, D))],
        out_specs=tok(D),
        out_shape=jax.ShapeDtypeStruct((B, S, D), F32),
        scratch_shapes=[pltpu.VMEM((tm, W), BF16),
                        pltpu.VMEM((H, dh, 2 * dh), F32), pltpu.VMEM((H, 8, LANES), F32),
                        pltpu.VMEM((H, S), F32), pltpu.VMEM((H, S), F32), pltpu.VMEM((S, LANES), F32)],
        compiler_params=pltpu.CompilerParams(
            dimension_semantics=("arbitrary", "arbitrary"), vmem_limit_bytes=VMEM_LIMIT),
    )(x, attn, p, gates_t, mqk, mv, mo, mqk, mv, mo, norm_g,
      w_out, g_mlp, w_up, w_down, g_ple, w_pg, w_ple, g_fin)


def _rope_tables(seq):
    half = ROPE_DIM // 2
    inv_freq = jnp.power(ROPE_THETA, -jnp.arange(half, dtype=F32) / half)
    ang = jnp.arange(seq, dtype=jnp.int32).astype(F32)[:, None] * inv_freq[None, :]
    cos, sin = jnp.cos(ang), jnp.sin(ang)
    rest = ATTN_HEAD_DIM - ROPE_DIM
    cos_head = jnp.concatenate([cos, cos, jnp.ones((seq, rest), F32)], axis=1)
    sin_head = jnp.concatenate([-sin, sin, jnp.zeros((seq, rest), F32)], axis=1)
    reps = LANES // ATTN_HEAD_DIM
    return jnp.tile(cos_head, (1, reps)), jnp.tile(sin_head, (1, reps))


def kernel(x, p, norm_mix_g, w_in, conv_w, conv_b, gate_b, mlstm_norm_g, w_out, norm_mlp_g, w_up, w_down,
           norm_ple_g, w_ple_gate, w_ple, final_norm_g):
    B, S, D = x.shape
    depth = w_in.shape[0]
    cos_t, sin_t = _rope_tables(S)
    row = lambda v: v.reshape(1, -1).astype(F32)
    n_gate = 2 * MLSTM_HEADS
    h = x
    for layer in range(depth):
        gb = jnp.pad(gate_b[layer].astype(F32), (0, LANES - n_gate)).reshape(1, LANES)
        q, k, v, mqk, mv, mo, gates_t = _in_proj(h, row(norm_mix_g[layer]), jnp.swapaxes(w_in[layer], 0, 1), gb,
                                                 cos_t, sin_t, conv_w[layer].astype(F32), row(conv_b[layer]))
        attn, w_up_b, w_down_b, w_out_b, w_pg_b = _dilated_attention(q, k, v, w_up[layer], w_down[layer],
                                                                     w_out[layer], w_ple_gate[layer])
        h = _mlstm_tail(h, attn, p, layer, mqk, mv, mo, gates_t, row(mlstm_norm_g[layer]),
                        w_out_b, row(norm_mlp_g[layer]), w_up_b, w_down_b, row(norm_ple_g[layer]),
                        w_pg_b, w_ple[layer].astype(BF16), row(final_norm_g), final_norm=(layer == depth - 1))
    return h
```

```python
import functools
import math

import jax
import jax.numpy as jnp
from jax import lax
from jax.experimental import pallas as pl
from jax.experimental.pallas import tpu as pltpu

D_MODEL = 1024
ATTN_HEADS = 8
ATTN_HEAD_DIM = 64
ATTN_WIDTH = ATTN_HEADS * ATTN_HEAD_DIM
MLSTM_HEADS = 4
MLSTM_HEAD_DIM = 128
MLSTM_WIDTH = MLSTM_HEADS * MLSTM_HEAD_DIM
DILATIONS = (1, 4, 16)
ATTN_BLOCK = 128
ATTN_UNROLL = 32
ROPE_THETA = 500000.0
ROPE_DIM = ATTN_HEAD_DIM // 4
CONV_WIDTH = 4
D_FF = 4 * D_MODEL
PLE_DIM = 256
RMS_EPS = 1e-6

LANES = 128
HEAD_PAIRS = ATTN_WIDTH // LANES
MAIN_WIDTH = 3 * ATTN_WIDTH + 4 * MLSTM_WIDTH
MLSTM_CHUNK = 256
CONV_HALO = 8
VMEM_LIMIT = 56 * 1024 * 1024

F32 = jnp.float32
BF16 = jnp.bfloat16
NEG_INF = float("-inf")
LOG2E = 1.4426950408889634


def _rms(x, g):
    return x * lax.rsqrt(jnp.mean(x * x, axis=-1, keepdims=True) + RMS_EPS) * g


def _resident(shape):
    return pl.BlockSpec(shape, lambda *_: (0,) * len(shape), pipeline_mode=pl.Buffered(1))


def _in_proj_kernel(x_ref, g_ref, w32_ref, gb_ref, cos_ref, sin_ref, cw_ref, cb_ref,
                    q_ref, k_ref, v_ref, mqk_ref, mv_ref, mo_ref, gt_ref, tail_ref, w_ref, wg_ref, *, tm, sub):
    @pl.when((pl.program_id(0) == 0) & (pl.program_id(1) == 0))
    def _():
        gain = g_ref[...]
        q_scale = LOG2E / math.sqrt(ATTN_HEAD_DIM)
        for c0 in range(0, MAIN_WIDTH, ATTN_WIDTH):
            w = w32_ref[c0:c0 + ATTN_WIDTH, :] * gain
            w_ref[:, c0:c0 + ATTN_WIDTH] = (w * q_scale if c0 == 0 else w).T.astype(BF16)
        n_gate = 2 * MLSTM_HEADS
        w_gates = w32_ref[MAIN_WIDTH:MAIN_WIDTH + n_gate, :] * gain
        wg_ref[...] = jnp.concatenate([w_gates, jnp.zeros((LANES - n_gate, w_gates.shape[1]), F32)],
                                      axis=0).T.astype(BF16)

    @pl.when(pl.program_id(1) == 0)
    def _():
        tail_ref[:, :CONV_HALO, :] = jnp.zeros((tail_ref.shape[0], CONV_HALO, LANES), F32)

    A, M = ATTN_WIDTH, MLSTM_WIDTH
    lane = lax.broadcasted_iota(jnp.int32, (sub, LANES), 1)
    first_half = (lane % ATTN_HEAD_DIM) < (ROPE_DIM // 2)
    k_scale = 1.0 / math.sqrt(MLSTM_HEAD_DIM)
    rb = 128

    for s0 in range(0, tm, sub):
        rows = slice(s0, s0 + sub)
        u = _rms(x_ref[rows, :], 1.0).astype(BF16)
        pos0 = pl.multiple_of(pl.program_id(1) * tm + s0, sub)
        cos = cos_ref[pl.ds(pos0, sub), :]
        sin = sin_ref[pl.ds(pos0, sub), :]

        def proj(c0, width, u=u):
            return jnp.dot(u, w_ref[:, c0:c0 + width], preferred_element_type=F32)

        def rope_store(y, o_ref, cos=cos, sin=sin, rows=rows):
            for j in range(HEAD_PAIRS):
                yj = y[:, j * LANES:(j + 1) * LANES]
                partner = jnp.where(first_half,
                                    pltpu.roll(yj, LANES - ROPE_DIM // 2, axis=1),
                                    pltpu.roll(yj, ROPE_DIM // 2, axis=1))
                o_ref[rows, j * LANES:(j + 1) * LANES] = (yj * cos + partner * sin).astype(BF16)

        y = proj(3 * A, 2 * M)
        for c0 in range(0, 2 * M, LANES):
            cols = slice(c0, c0 + LANES)
            slab = tail_ref.at[c0 // LANES]
            slab[CONV_HALO + s0:CONV_HALO + s0 + sub, :] = y[:, cols]
            for r0 in range(s0, s0 + sub, rb):
                acc = cb_ref[:, cols]
                for j in range(CONV_WIDTH):
                    start = CONV_HALO + r0 - j
                    acc = acc + cw_ref[CONV_WIDTH - 1 - j:CONV_WIDTH - j, cols] * slab[pl.ds(start, rb, stride=1), :]
                act = acc * jax.nn.sigmoid(acc)
                mqk_ref[r0:r0 + rb, cols] = (act if c0 < M else act * k_scale).astype(BF16)
        rope_store(proj(0, A), q_ref)
        rope_store(proj(A, A), k_ref)
        v_ref[rows, :] = proj(2 * A, A).astype(BF16)
        mv_ref[rows, :] = proj(3 * A + 2 * M, M).astype(BF16)
        mo_ref[rows, :] = proj(3 * A + 3 * M, M).astype(BF16)
        gates = jnp.dot(u, wg_ref[...], preferred_element_type=F32) + gb_ref[...]
        gt_ref[:, rows] = gates.T[:2 * MLSTM_HEADS, :]
    tail_ref[:, :CONV_HALO, :] = tail_ref[:, tm:tm + CONV_HALO, :]


def _in_proj(x, g, w_in_t, gate_b, cos_t, sin_t, conv_w, conv_b, tm=1024, sub=128):
    B, S, D = x.shape
    grid = (B, S // tm)
    tok = lambda width: pl.BlockSpec((None, tm, width), lambda b, i: (b, i, 0))
    attn_shape = jax.ShapeDtypeStruct((B, S, ATTN_WIDTH), BF16)
    return pl.pallas_call(
        functools.partial(_in_proj_kernel, tm=tm, sub=sub),
        name="in_proj",
        grid=grid,
        in_specs=[tok(D), _resident((1, D)), _resident(w_in_t.shape),
                  _resident((1, LANES)), _resident(cos_t.shape), _resident(sin_t.shape),
                  _resident(conv_w.shape), _resident(conv_b.shape)],
        out_specs=[tok(ATTN_WIDTH), tok(ATTN_WIDTH), tok(ATTN_WIDTH),
                   tok(2 * MLSTM_WIDTH), tok(MLSTM_WIDTH), tok(MLSTM_WIDTH),
                   pl.BlockSpec((None, 2 * MLSTM_HEADS, tm), lambda b, i: (b, 0, i))],
        out_shape=[attn_shape, attn_shape, attn_shape,
                   jax.ShapeDtypeStruct((B, S, 2 * MLSTM_WIDTH), BF16),
                   jax.ShapeDtypeStruct((B, S, MLSTM_WIDTH), BF16),
                   jax.ShapeDtypeStruct((B, S, MLSTM_WIDTH), BF16),
                   jax.ShapeDtypeStruct((B, 2 * MLSTM_HEADS, S), F32)],
        scratch_shapes=[pltpu.VMEM((2 * MLSTM_WIDTH // LANES, CONV_HALO + tm, LANES), F32),
                        pltpu.VMEM((D, MAIN_WIDTH), BF16), pltpu.VMEM((D, LANES), BF16)],
        compiler_params=pltpu.CompilerParams(
            dimension_semantics=("arbitrary", "arbitrary"), vmem_limit_bytes=VMEM_LIMIT),
    )(x, g, w_in_t, gate_b, cos_t, sin_t, conv_w, conv_b)


def _attn_kernel(q_ref, k_ref, v_ref, wu32_ref, wd32_ref, wo32_ref, wpg32_ref, o_ref, wu_ref, wd_ref, wo_ref, wpg_ref,
                 stage, s4, q4, k4, v4, q16, k16, v16,
                 acc1, acc4, acc16, sm1, sm4, sm16, sd1, sd4, sd16, bias_ref, *, seq):
    blk = ATTN_BLOCK
    d4, d16 = DILATIONS[1], DILATIONS[2]
    assert DILATIONS[0] == 1 and d16 == d4 * d4
    for src, dst in ((wu32_ref, wu_ref), (wd32_ref, wd_ref), (wo32_ref, wo_ref), (wpg32_ref, wpg_ref)):
        dst[...] = src[...].astype(BF16)
    qi = lax.broadcasted_iota(jnp.int32, (2 * blk, 2 * blk), 0) % blk
    ki = lax.broadcasted_iota(jnp.int32, (2 * blk, 2 * blk), 1)
    dist = qi + blk - ki
    valid = (dist >= 0) & (dist <= blk)
    bias_ref[1] = jnp.where(valid, 0.0, NEG_INF)
    bias_ref[0] = jnp.where(valid & (ki >= blk), 0.0, NEG_INF)

    head0 = lax.broadcasted_iota(jnp.int32, (blk, LANES), 1) < ATTN_HEAD_DIM
    ones_kv = jnp.ones((2 * blk, LANES), BF16)

    conv_rows = 512
    for x_ref, x4, x16 in ((q_ref, q4, q16), (k_ref, k4, k16), (v_ref, v4, v16)):
        def to_f32(c, carry, x_ref=x_ref):
            rows = pl.ds(pl.multiple_of(c * conv_rows, conv_rows), conv_rows)
            stage[rows, :] = x_ref[rows, :].astype(F32)
            return carry
        lax.fori_loop(0, seq // conv_rows, to_f32, 0)

        def by4(c, carry, x4=x4):
            l0 = pl.multiple_of(c * blk, blk)
            for r in range(d4):
                rows = stage[pl.ds(l0 * d4 + r, blk, stride=d4), :]
                s4[r, pl.ds(l0, blk), :] = rows
                x4[pl.ds(l0, blk), r * LANES:(r + 1) * LANES] = rows.astype(BF16)
            return carry
        lax.fori_loop(0, seq // d4 // blk, by4, 0)

        def by16(c, carry, x16=x16):
            l0 = pl.multiple_of(c * 64, 64)
            for r in range(d16):
                x16[pl.ds(l0, 64), r * LANES:(r + 1) * LANES] = (
                    s4[r % d4, pl.ds(l0 * d4 + r // d4, 64, stride=d4), :].astype(BF16))
            return carry
        lax.fori_loop(0, seq // d16 // 64, by16, 0)

    patterns = ((q_ref, k_ref, v_ref, acc1, sm1, sd1, 1),
                (q4, k4, v4, acc4, sm4, sd4, d4),
                (q16, k16, v16, acc16, sm16, sd16, d16))

    def result_rows(ref, row0, r, d):
        if d == 1:
            return ref.at[pl.ds(row0, blk), :]
        if d == d4:
            return ref.at[r, pl.ds(row0, blk), :]
        return ref.at[r % d4, pl.ds(row0 * d4 + r // d4, blk, stride=d4), :]

    def view_lanes(r, d):
        return slice(0, LANES) if d == 1 else pl.ds(pl.multiple_of(r * LANES, LANES), LANES)

    for (qv, kv, vv_ref, acc_ref, sm_ref, sd_ref, d) in patterns:
        nb = seq // d // blk

        def block(i, carry, qv=qv, kv=kv, vv_ref=vv_ref, acc_ref=acc_ref, sm_ref=sm_ref, sd_ref=sd_ref, d=d, nb=nb):
            r = i // nb
            n = i % nb
            row0 = pl.multiple_of(n * blk, blk)
            prow0 = pl.multiple_of(jnp.maximum(n - 1, 0) * blk, blk)
            lanes = view_lanes(r, d)
            q = qv[pl.ds(row0, blk), lanes]
            zero = jnp.zeros_like(q)
            q2 = jnp.concatenate([jnp.where(head0, q, zero), jnp.where(head0, zero, q)], axis=0)
            kk = jnp.concatenate([kv[pl.ds(prow0, blk), lanes], kv[pl.ds(row0, blk), lanes]], axis=0)
            vv = jnp.concatenate([vv_ref[pl.ds(prow0, blk), lanes], vv_ref[pl.ds(row0, blk), lanes]], axis=0)
            s = lax.dot_general(q2, kk, (((1,), (1,)), ((), ())), preferred_element_type=F32)
            s = s + bias_ref[jnp.minimum(n, 1)]
            m = jnp.max(s, axis=1, keepdims=True)
            p = jnp.exp2(s - m).astype(BF16)
            res = jnp.dot(p, jnp.concatenate([vv, ones_kv], axis=1), preferred_element_type=F32)
            result_rows(acc_ref, row0, r, d)[...] = jnp.where(head0, res[:blk, :LANES], res[blk:, :LANES])
            result_rows(sm_ref, row0, r, d)[...] = jnp.where(head0, m[:blk], m[blk:])
            result_rows(sd_ref, row0, r, d)[...] = jnp.where(head0, res[:blk, LANES:], res[blk:, LANES:])
            return carry

        lax.fori_loop(0, d * nb, block, 0, unroll=ATTN_UNROLL)

    rows_m = 256

    def mix(c, carry):
        l0 = pl.multiple_of(c * rows_m, rows_m)
        com = pl.ds(l0, rows_m)
        for r in range(d4):
            tok = pl.ds(l0 * d4 + r, rows_m, stride=d4)
            m1, m4, m16 = sm1[tok, :], sm4[r, com, :], sm16[r, com, :]
            top = jnp.maximum(jnp.maximum(m1, m4), m16)
            e1, e4, e16 = jnp.exp2(m1 - top), jnp.exp2(m4 - top), jnp.exp2(m16 - top)
            den = e1 * sd1[tok, :] + e4 * sd4[r, com, :] + e16 * sd16[r, com, :]
            num = e1 * acc1[tok, :] + e4 * acc4[r, com, :] + e16 * acc16[r, com, :]
            stage[tok, :] = num * (1.0 / den)
        return carry

    lax.fori_loop(0, seq // d4 // rows_m, mix, 0)

    def emit(c, carry):
        rows = pl.ds(pl.multiple_of(c * conv_rows, conv_rows), conv_rows)
        o_ref[rows, :] = stage[rows, :].astype(BF16)
        return carry

    lax.fori_loop(0, seq // conv_rows, emit, 0)


def _dilated_attention(q, k, v, w_up, w_down, w_out, w_pg):
    B, S, W = q.shape
    d4 = DILATIONS[1]
    hp = W // LANES
    steps = B * hp
    d_model, d_ff = w_up.shape
    col_w, row_h, row_s = d_ff // steps, d_ff // steps, d_model // steps
    assert col_w % LANES == 0 and row_s % 16 == 0 and col_w * steps == d_ff and row_s * steps == d_model
    step = lambda b, h: b * hp + h
    w_specs = [pl.BlockSpec((d_model, col_w), lambda b, h: (0, step(b, h))),
               pl.BlockSpec((row_h, d_model), lambda b, h: (step(b, h), 0)),
               pl.BlockSpec((row_s, d_model), lambda b, h: (step(b, h), 0)),
               pl.BlockSpec((row_s, d_model), lambda b, h: (step(b, h), 0))]
    spec = pl.BlockSpec((None, S, LANES), lambda b, h: (b, 0, h))
    token = pltpu.VMEM((S, LANES), F32)
    by4 = pltpu.VMEM((d4, S // d4, LANES), F32)
    scratch = ([token, by4]
               + [pltpu.VMEM((S // d, d * LANES), BF16) for d in DILATIONS[1:] for _ in range(3)]
               + [token, by4, by4] * 3
               + [pltpu.VMEM((2, 2 * ATTN_BLOCK, 2 * ATTN_BLOCK), F32)])
    return pl.pallas_call(
        functools.partial(_attn_kernel, seq=S),
        name="dilated_attention",
        grid=(B, hp),
        in_specs=[spec, spec, spec] + w_specs,
        out_specs=[spec] + w_specs,
        out_shape=[jax.ShapeDtypeStruct((B, S, W), BF16)]
                  + [jax.ShapeDtypeStruct(w.shape, BF16) for w in (w_up, w_down, w_out, w_pg)],
        scratch_shapes=scratch,
        compiler_params=pltpu.CompilerParams(
            dimension_semantics=("parallel", "parallel"), vmem_limit_bytes=VMEM_LIMIT),
    )(q, k, v, w_up, w_down, w_out, w_pg)


def _gate_scans(gt_ref, a_ref, b_ref, col_ref):
    L, H = MLSTM_CHUNK, MLSTM_HEADS
    rows = gt_ref.shape[1]
    gates = gt_ref[...]
    f_pre = gates[H:]
    b = (jnp.minimum(f_pre, 0.0) - jnp.log(1.0 + jnp.exp(-jnp.abs(f_pre)))) * LOG2E
    pos = lax.broadcasted_iota(jnp.int32, (H, rows), 1) % L
    for step in [1 << e for e in range(L.bit_length() - 1)]:
        b = b + jnp.where(pos >= step, pltpu.roll(b, step, axis=1), 0.0)
    a = gates[:H] * LOG2E - b
    amax = a
    for step in [1 << e for e in range(L.bit_length() - 1)]:
        amax = jnp.maximum(amax, jnp.where(pos >= step, pltpu.roll(amax, step, axis=1), NEG_INF))
    a_ref[...] = a
    b_ref[...] = b
    col_ref[...] = jnp.concatenate([b, amax, a, jnp.zeros((LANES - 3 * H, rows), F32)], axis=0).T


def _mlstm_tile(qk_ref, v_ref, mo_ref, ng_ref, out_ref, c_ref, m_ref, a_ref, b_ref, col_ref, *, rows, base, fresh):
    L, H, dh = MLSTM_CHUNK, MLSTM_HEADS, MLSTM_HEAD_DIM
    causal = lax.broadcasted_iota(jnp.int32, (L, L), 1) <= lax.broadcasted_iota(jnp.int32, (L, L), 0)
    ones_blk = jnp.ones((L, dh), BF16)
    wide = lambda x: jnp.concatenate([x, x], axis=1)

    def head_chunk(h, r0):
        cols = slice(h * dh, (h + 1) * dh)
        tok = slice(r0, r0 + L)
        q_b = qk_ref[tok, cols]
        k_b = qk_ref[tok, MLSTM_WIDTH + h * dh:MLSTM_WIDTH + (h + 1) * dh]
        v_b = v_ref[tok, cols]

        pos = pl.ds(pl.multiple_of(base + r0, L), L)
        column = lambda j: jnp.broadcast_to(col_ref[pos, j:j + 1], (L, dh))
        b_rep, amax_rep, a_rep = column(h), column(H + h), column(2 * H + h)
        a_row = a_ref[h:h + 1, pos]
        b_last = b_ref[h:h + 1, pos][:, L - 1:L]
        c_prev = c_ref[h]
        m_prev = m_ref[h, 0:1, :]
        if r0 == 0 and fresh is not None:
            c_prev = jnp.where(fresh, 0.0, c_prev)
            m_prev = jnp.where(fresh, 0.0, m_prev)
        mx = jnp.maximum(m_prev, amax_rep)

        w_intra = jnp.exp2(jnp.where(causal, a_row - wide(mx), NEG_INF))
        qk = lax.dot_general(q_b, k_b, (((1,), (1,)), ((), ())), preferred_element_type=F32) * w_intra
        res = (wide(jnp.exp2(m_prev - mx)) * jnp.dot(q_b, c_prev.astype(BF16), preferred_element_type=F32)
               + jnp.dot(qk.astype(BF16), jnp.concatenate([v_b, ones_blk], axis=1), preferred_element_type=F32))
        hid = res[:, :dh] / jnp.maximum(jnp.abs(res[:, dh:]), jnp.exp2(-(b_rep + mx)))

        m_new = b_last + jnp.maximum(m_prev, jnp.max(a_row, axis=1, keepdims=True))
        decay = jnp.exp2(b_last + m_prev - m_new)
        ws = jnp.exp2(b_last + a_rep - m_new)
        wv = jnp.concatenate([(v_b.astype(F32) * ws).astype(BF16), ws.astype(BF16)], axis=1)
        c_ref[h] = wide(decay) * c_prev + lax.dot_general(k_b, wv, (((0,), (0,)), ((), ())),
                                                          preferred_element_type=F32)
        m_ref[h] = jnp.broadcast_to(m_new, m_ref.shape[1:])

        hn = _rms(hid, ng_ref[:, cols])
        out_ref[tok, cols] = (jax.nn.sigmoid(mo_ref[tok, cols].astype(F32)) * hn).astype(BF16)

    for r0 in range(0, rows, L):
        for h in range(H):
            head_chunk(h, r0)


def _mlstm_tail_kernel(x_ref, at_ref, p_ref, gt_ref, qk0_ref, v0_ref, mo0_ref, qk_ref, v_ref, mo_ref,
                       ng_ref, wo_ref, g_mlp_ref, wu_ref, wd_ref, g_ple_ref, wpg_ref, wple_ref, g_fin_ref, o_ref,
                       ml_ref, c_ref, m_ref, a_ref, b_ref, col_ref, *, tm, ff_chunk, final_norm):
    state = (c_ref, m_ref, a_ref, b_ref, col_ref)
    n = pl.num_programs(1)
    t = pl.program_id(0) * n + pl.program_id(1)
    nxt = jnp.minimum(t + 1, pl.num_programs(0) * n - 1)
    nxt_tile = nxt % n

    @pl.when(t == 0)
    def _():
        c_ref[...] = jnp.zeros_like(c_ref)
        m_ref[...] = jnp.zeros_like(m_ref)
        _gate_scans(gt_ref, a_ref, b_ref, col_ref)
        _mlstm_tile(qk0_ref, v0_ref, mo0_ref, ng_ref, ml_ref, *state, rows=tm, base=0, fresh=None)

    @pl.when((nxt_tile == 0) & (nxt > t))
    def _():
        _gate_scans(gt_ref, a_ref, b_ref, col_ref)

    ml = ml_ref[...]
    _mlstm_tile(qk_ref, v_ref, mo_ref, ng_ref, ml_ref, *state, rows=tm, base=nxt_tile * tm, fresh=nxt_tile == 0)

    mix = jnp.concatenate([at_ref[...], ml], axis=1)
    h = x_ref[...] + jnp.dot(mix, wo_ref[...], preferred_element_type=F32)

    u = _rms(h, g_mlp_ref[...]).astype(BF16)
    for c0 in range(0, D_FF, ff_chunk):
        a = jnp.dot(u, wu_ref[:, c0:c0 + ff_chunk], preferred_element_type=F32)
        a = jnp.square(jnp.maximum(a, 0.0)).astype(BF16)
        h = h + jnp.dot(a, wd_ref[c0:c0 + ff_chunk, :], preferred_element_type=F32)

    gate = jax.nn.sigmoid(jnp.dot(_rms(h, g_ple_ref[...]).astype(BF16), wpg_ref[...], preferred_element_type=F32))
    h = h + gate * jnp.dot(p_ref[...].astype(BF16), wple_ref[...], preferred_element_type=F32)
    o_ref[...] = _rms(h, g_fin_ref[...]) if final_norm else h


def _mlstm_tail(x, attn, p, layer, mqk, mv, mo, gates_t, norm_g, w_out, g_mlp, w_up, w_down, g_ple, w_pg, w_ple,
                g_fin, final_norm, tm=512, ff_chunk=1024):
    B, S, D = x.shape
    H, dh, W = MLSTM_HEADS, MLSTM_HEAD_DIM, MLSTM_WIDTH
    n = S // tm
    tok = lambda width: pl.BlockSpec((None, tm, width), lambda b, i: (b, i, 0))
    first = lambda width: pl.BlockSpec((None, tm, width), lambda b, i: (0, 0, 0))

    def following(b, i):
        t = jnp.minimum(b * n + i + 1, B * n - 1)
        return t // n, t % n

    nxt = lambda width: pl.BlockSpec((None, tm, width), lambda b, i: (*following(b, i), 0))
    p_spec = pl.BlockSpec((None, None, tm, PLE_DIM), lambda b, i: (layer, b, i, 0))
    gt_spec = pl.BlockSpec((None, 2 * H, S), lambda b, i: (following(b, i)[0], 0, 0))
    return pl.pallas_call(
        functools.partial(_mlstm_tail_kernel, tm=tm, ff_chunk=ff_chunk, final_norm=final_norm),
        name="mlstm_tail",
        grid=(B, n),
        in_specs=[tok(D), tok(ATTN_WIDTH), p_spec, gt_spec,
                  first(2 * W), first(W), first(W),
                  nxt(2 * W), nxt(W), nxt(W), _resident(norm_g.shape),
                  _resident(w_out.shape), _resident((1, D)), _resident(w_up.shape), _resident(w_down.shape),
                  _resident((1, D)), _resident(w_pg.shape), _resident(w_ple.shape), _resident((1, D))],
        out_specs=tok(D),
        out_shape=jax.ShapeDtypeStruct((B, S, D), F32),
        scratch_shapes=[pltpu.VMEM((tm, W), BF16),
                        pltpu.VMEM((H, dh, 2 * dh), F32), pltpu.VMEM((H, 8, LANES), F32),
                        pltpu.VMEM((H, S), F32), pltpu.VMEM((H, S), F32), pltpu.VMEM((S, LANES), F32)],
        compiler_params=pltpu.CompilerParams(
            dimension_semantics=("arbitrary", "arbitrary"), vmem_limit_bytes=VMEM_LIMIT),
    )(x, attn, p, gates_t, mqk, mv, mo, mqk, mv, mo, norm_g,
      w_out, g_mlp, w_up, w_down, g_ple, w_pg, w_ple, g_fin)


def _rope_tables(seq):
    half = ROPE_DIM // 2
    inv_freq = jnp.power(ROPE_THETA, -jnp.arange(half, dtype=F32) / half)
    ang = jnp.arange(seq, dtype=jnp.int32).astype(F32)[:, None] * inv_freq[None, :]
    cos, sin = jnp.cos(ang), jnp.sin(ang)
    rest = ATTN_HEAD_DIM - ROPE_DIM
    cos_head = jnp.concatenate([cos, cos, jnp.ones((seq, rest), F32)], axis=1)
    sin_head = jnp.concatenate([-sin, sin, jnp.zeros((seq, rest), F32)], axis=1)
    reps = LANES // ATTN_HEAD_DIM
    return jnp.tile(cos_head, (1, reps)), jnp.tile(sin_head, (1, reps))


def kernel(x, p, norm_mix_g, w_in, conv_w, conv_b, gate_b, mlstm_norm_g, w_out, norm_mlp_g, w_up, w_down,
           norm_ple_g, w_ple_gate, w_ple, final_norm_g):
    B, S, D = x.shape
    depth = w_in.shape[0]
    cos_t, sin_t = _rope_tables(S)
    row = lambda v: v.reshape(1, -1).astype(F32)
    n_gate = 2 * MLSTM_HEADS
    h = x
    for layer in range(depth):
        gb = jnp.pad(gate_b[layer].astype(F32), (0, LANES - n_gate)).reshape(1, LANES)
        q, k, v, mqk, mv, mo, gates_t = _in_proj(h, row(norm_mix_g[layer]), jnp.swapaxes(w_in[layer], 0, 1), gb,
                                                 cos_t, sin_t, conv_w[layer].astype(F32), row(conv_b[layer]))
        attn, w_up_b, w_down_b, w_out_b, w_pg_b = _dilated_attention(q, k, v, w_up[layer], w_down[layer],
                                                                     w_out[layer], w_ple_gate[layer])
        h = _mlstm_tail(h, attn, p, layer, mqk, mv, mo, gates_t, row(mlstm_norm_g[layer]),
                        w_out_b, row(norm_mlp_g[layer]), w_up_b, w_down_b, row(norm_ple_g[layer]),
                        w_pg_b, w_ple[layer].astype(BF16), row(final_norm_g), final_norm=(layer == depth - 1))
    return h
```

```python
import functools
import math

import jax
import jax.numpy as jnp
from jax import lax
from jax.experimental import pallas as pl
from jax.experimental.pallas import tpu as pltpu

D_MODEL = 1024
ATTN_HEADS = 8
ATTN_HEAD_DIM = 64
ATTN_WIDTH = ATTN_HEADS * ATTN_HEAD_DIM
MLSTM_HEADS = 4
MLSTM_HEAD_DIM = 128
MLSTM_WIDTH = MLSTM_HEADS * MLSTM_HEAD_DIM
DILATIONS = (1, 4, 16)
ATTN_BLOCK = 128
ATTN_UNROLL = 32
ROPE_THETA = 500000.0
ROPE_DIM = ATTN_HEAD_DIM // 4
CONV_WIDTH = 4
D_FF = 4 * D_MODEL
PLE_DIM = 256
RMS_EPS = 1e-6

LANES = 128
HEAD_PAIRS = ATTN_WIDTH // LANES
MAIN_WIDTH = 3 * ATTN_WIDTH + 4 * MLSTM_WIDTH
MLSTM_CHUNK = 256
CONV_HALO = 8
VMEM_LIMIT = 56 * 1024 * 1024

F32 = jnp.float32
BF16 = jnp.bfloat16
NEG_INF = float("-inf")
LOG2E = 1.4426950408889634


def _rms(x, g):
    return x * lax.rsqrt(jnp.mean(x * x, axis=-1, keepdims=True) + RMS_EPS) * g


def _resident(shape):
    return pl.BlockSpec(shape, lambda *_: (0,) * len(shape), pipeline_mode=pl.Buffered(1))


def _in_proj_kernel(x_ref, g_ref, w32_ref, gb_ref, cos_ref, sin_ref, cw_ref, cb_ref,
                    q_ref, k_ref, v_ref, q4_ref, k4_ref, v4_ref, q16_ref, k16_ref, v16_ref,
                    mqk_ref, mv_ref, mo_ref, gt_ref, tail_ref, w_ref, wg_ref, tok_ref, by4_ref, *, tm, sub):
    @pl.when((pl.program_id(0) == 0) & (pl.program_id(1) == 0))
    def _():
        gain = g_ref[...]
        q_scale = LOG2E / math.sqrt(ATTN_HEAD_DIM)
        for c0 in range(0, MAIN_WIDTH, ATTN_WIDTH):
            w = w32_ref[c0:c0 + ATTN_WIDTH, :] * gain
            w_ref[:, c0:c0 + ATTN_WIDTH] = (w * q_scale if c0 == 0 else w).T.astype(BF16)
        n_gate = 2 * MLSTM_HEADS
        w_gates = w32_ref[MAIN_WIDTH:MAIN_WIDTH + n_gate, :] * gain
        wg_ref[...] = jnp.concatenate([w_gates, jnp.zeros((LANES - n_gate, w_gates.shape[1]), F32)],
                                      axis=0).T.astype(BF16)

    @pl.when(pl.program_id(1) == 0)
    def _():
        tail_ref[:, :CONV_HALO, :] = jnp.zeros((tail_ref.shape[0], CONV_HALO, LANES), F32)

    A, M = ATTN_WIDTH, MLSTM_WIDTH
    lane = lax.broadcasted_iota(jnp.int32, (sub, LANES), 1)
    first_half = (lane % ATTN_HEAD_DIM) < (ROPE_DIM // 2)
    k_scale = 1.0 / math.sqrt(MLSTM_HEAD_DIM)
    rb = 128
    d4, d16 = DILATIONS[1], DILATIONS[2]
    grp = 16 * d16

    for s0 in range(0, tm, sub):
        rows = slice(s0, s0 + sub)
        u = _rms(x_ref[rows, :], 1.0).astype(BF16)
        pos0 = pl.multiple_of(pl.program_id(1) * tm + s0, sub)
        cos = cos_ref[pl.ds(pos0, sub), :]
        sin = sin_ref[pl.ds(pos0, sub), :]

        def proj(c0, width, u=u):
            return jnp.dot(u, w_ref[:, c0:c0 + width], preferred_element_type=F32)

        def rope_store(y, o_ref, a, cos=cos, sin=sin, rows=rows):
            for j in range(HEAD_PAIRS):
                yj = y[:, j * LANES:(j + 1) * LANES]
                partner = jnp.where(first_half,
                                    pltpu.roll(yj, LANES - ROPE_DIM // 2, axis=1),
                                    pltpu.roll(yj, ROPE_DIM // 2, axis=1))
                roped = yj * cos + partner * sin
                tok_ref[a, j, rows, :] = roped
                o_ref[rows, j * LANES:(j + 1) * LANES] = roped.astype(BF16)

        y = proj(3 * A, 2 * M)
        for c0 in range(0, 2 * M, LANES):
            cols = slice(c0, c0 + LANES)
            slab = tail_ref.at[c0 // LANES]
            slab[CONV_HALO + s0:CONV_HALO + s0 + sub, :] = y[:, cols]
            for r0 in range(s0, s0 + sub, rb):
                acc = cb_ref[:, cols]
                for j in range(CONV_WIDTH):
                    start = CONV_HALO + r0 - j
                    acc = acc + cw_ref[CONV_WIDTH - 1 - j:CONV_WIDTH - j, cols] * slab[pl.ds(start, rb, stride=1), :]
                act = acc * jax.nn.sigmoid(acc)
                mqk_ref[r0:r0 + rb, cols] = (act if c0 < M else act * k_scale).astype(BF16)
        rope_store(proj(0, A), q_ref, 0)
        rope_store(proj(A, A), k_ref, 1)
        y = proj(2 * A, A)
        for j in range(HEAD_PAIRS):
            tok_ref[2, j, rows, :] = y[:, j * LANES:(j + 1) * LANES]
        v_ref[rows, :] = y.astype(BF16)
        mv_ref[rows, :] = proj(3 * A + 2 * M, M).astype(BF16)
        mo_ref[rows, :] = proj(3 * A + 3 * M, M).astype(BF16)
        gates = jnp.dot(u, wg_ref[...], preferred_element_type=F32) + gb_ref[...]
        gt_ref[:, rows] = gates.T[:2 * MLSTM_HEADS, :]
        if (s0 + sub) % grp == 0:
            g0 = s0 + sub - grp
            for a, (x4_ref, x16_ref) in enumerate(((q4_ref, q16_ref), (k4_ref, k16_ref), (v4_ref, v16_ref))):
                for j in range(HEAD_PAIRS):
                    for r in range(d4):
                        part = tok_ref[a, j, pl.ds(g0 + r, grp // d4, stride=d4), :]
                        by4_ref[a, j, r, g0 // d4:(g0 + grp) // d4, :] = part
                        x4_ref[g0 // d4:(g0 + grp) // d4, (j * d4 + r) * LANES:(j * d4 + r + 1) * LANES] = (
                            part.astype(BF16))
                    for r in range(d16):
                        x16_ref[g0 // d16:(g0 + grp) // d16, (j * d16 + r) * LANES:(j * d16 + r + 1) * LANES] = (
                            by4_ref[a, j, r % d4, pl.ds(g0 // d4 + r // d4, grp // d16, stride=d4), :].astype(BF16))
    tail_ref[:, :CONV_HALO, :] = tail_ref[:, tm:tm + CONV_HALO, :]


def _in_proj(x, g, w_in_t, gate_b, cos_t, sin_t, conv_w, conv_b, tm=512, sub=128):
    B, S, D = x.shape
    grid = (B, S // tm)
    d4, d16 = DILATIONS[1], DILATIONS[2]
    tok = lambda width: pl.BlockSpec((None, tm, width), lambda b, i: (b, i, 0))
    view = lambda d: pl.BlockSpec((None, tm // d, d * ATTN_WIDTH), lambda b, i: (b, i, 0))
    attn_shape = jax.ShapeDtypeStruct((B, S, ATTN_WIDTH), BF16)
    view_shape = lambda d: jax.ShapeDtypeStruct((B, S // d, d * ATTN_WIDTH), BF16)
    return pl.pallas_call(
        functools.partial(_in_proj_kernel, tm=tm, sub=sub),
        name="in_proj",
        grid=grid,
        in_specs=[tok(D), _resident((1, D)), _resident(w_in_t.shape),
                  _resident((1, LANES)), _resident(cos_t.shape), _resident(sin_t.shape),
                  _resident(conv_w.shape), _resident(conv_b.shape)],
        out_specs=[tok(ATTN_WIDTH), tok(ATTN_WIDTH), tok(ATTN_WIDTH),
                   view(d4), view(d4), view(d4), view(d16), view(d16), view(d16),
                   tok(2 * MLSTM_WIDTH), tok(MLSTM_WIDTH), tok(MLSTM_WIDTH),
                   pl.BlockSpec((None, 2 * MLSTM_HEADS, tm), lambda b, i: (b, 0, i))],
        out_shape=[attn_shape, attn_shape, attn_shape,
                   view_shape(d4), view_shape(d4), view_shape(d4),
                   view_shape(d16), view_shape(d16), view_shape(d16),
                   jax.ShapeDtypeStruct((B, S, 2 * MLSTM_WIDTH), BF16),
                   jax.ShapeDtypeStruct((B, S, MLSTM_WIDTH), BF16),
                   jax.ShapeDtypeStruct((B, S, MLSTM_WIDTH), BF16),
                   jax.ShapeDtypeStruct((B, 2 * MLSTM_HEADS, S), F32)],
        scratch_shapes=[pltpu.VMEM((2 * MLSTM_WIDTH // LANES, CONV_HALO + tm, LANES), F32),
                        pltpu.VMEM((D, MAIN_WIDTH), BF16), pltpu.VMEM((D, LANES), BF16),
                        pltpu.VMEM((3, HEAD_PAIRS, tm, LANES), F32),
                        pltpu.VMEM((3, HEAD_PAIRS, d4, tm // d4, LANES), F32)],
        compiler_params=pltpu.CompilerParams(
            dimension_semantics=("arbitrary", "arbitrary"), vmem_limit_bytes=VMEM_LIMIT),
    )(x, g, w_in_t, gate_b, cos_t, sin_t, conv_w, conv_b)


def _attn_kernel(q_ref, k_ref, v_ref, q4, k4, v4, q16, k16, v16, wu32_ref, wd32_ref, wo32_ref, wpg32_ref,
                 o_ref, wu_ref, wd_ref, wo_ref, wpg_ref, stage,
                 acc1, acc4, acc16, sm1, sm4, sm16, sd1, sd4, sd16, bias_ref, *, seq):
    blk = ATTN_BLOCK
    d4, d16 = DILATIONS[1], DILATIONS[2]
    assert DILATIONS[0] == 1 and d16 == d4 * d4
    for src, dst in ((wu32_ref, wu_ref), (wd32_ref, wd_ref), (wo32_ref, wo_ref), (wpg32_ref, wpg_ref)):
        dst[...] = src[...].astype(BF16)
    qi = lax.broadcasted_iota(jnp.int32, (2 * blk, 2 * blk), 0) % blk
    ki = lax.broadcasted_iota(jnp.int32, (2 * blk, 2 * blk), 1)
    dist = qi + blk - ki
    valid = (dist >= 0) & (dist <= blk)
    bias_ref[1] = jnp.where(valid, 0.0, NEG_INF)
    bias_ref[0] = jnp.where(valid & (ki >= blk), 0.0, NEG_INF)

    head0 = lax.broadcasted_iota(jnp.int32, (blk, LANES), 1) < ATTN_HEAD_DIM
    ones_kv = jnp.ones((2 * blk, LANES), BF16)

    conv_rows = 512
    patterns = ((q_ref, k_ref, v_ref, acc1, sm1, sd1, 1),
                (q4, k4, v4, acc4, sm4, sd4, d4),
                (q16, k16, v16, acc16, sm16, sd16, d16))

    def result_rows(ref, row0, r, d):
        if d == 1:
            return ref.at[pl.ds(row0, blk), :]
        if d == d4:
            return ref.at[r, pl.ds(row0, blk), :]
        return ref.at[r % d4, pl.ds(row0 * d4 + r // d4, blk, stride=d4), :]

    def view_lanes(r, d):
        return slice(0, LANES) if d == 1 else pl.ds(pl.multiple_of(r * LANES, LANES), LANES)

    for (qv, kv, vv_ref, acc_ref, sm_ref, sd_ref, d) in patterns:
        nb = seq // d // blk

        def block(i, carry, qv=qv, kv=kv, vv_ref=vv_ref, acc_ref=acc_ref, sm_ref=sm_ref, sd_ref=sd_ref, d=d, nb=nb):
            r = i // nb
            n = i % nb
            row0 = pl.multiple_of(n * blk, blk)
            prow0 = pl.multiple_of(jnp.maximum(n - 1, 0) * blk, blk)
            lanes = view_lanes(r, d)
            q = qv[pl.ds(row0, blk), lanes]
            zero = jnp.zeros_like(q)
            q2 = jnp.concatenate([jnp.where(head0, q, zero), jnp.where(head0, zero, q)], axis=0)
            kk = jnp.concatenate([kv[pl.ds(prow0, blk), lanes], kv[pl.ds(row0, blk), lanes]], axis=0)
            vv = jnp.concatenate([vv_ref[pl.ds(prow0, blk), lanes], vv_ref[pl.ds(row0, blk), lanes]], axis=0)
            s = lax.dot_general(q2, kk, (((1,), (1,)), ((), ())), preferred_element_type=F32)
            s = s + bias_ref[jnp.minimum(n, 1)]
            m = jnp.max(s, axis=1, keepdims=True)
            p = jnp.exp2(s - m).astype(BF16)
            res = jnp.dot(p, jnp.concatenate([vv, ones_kv], axis=1), preferred_element_type=F32)
            result_rows(acc_ref, row0, r, d)[...] = jnp.where(head0, res[:blk, :LANES], res[blk:, :LANES])
            result_rows(sm_ref, row0, r, d)[...] = jnp.where(head0, m[:blk], m[blk:])
            result_rows(sd_ref, row0, r, d)[...] = jnp.where(head0, res[:blk, LANES:], res[blk:, LANES:])
            return carry

        lax.fori_loop(0, d * nb, block, 0, unroll=ATTN_UNROLL)

    rows_m = 256

    def mix(c, carry):
        l0 = pl.multiple_of(c * rows_m, rows_m)
        com = pl.ds(l0, rows_m)
        for r in range(d4):
            tok = pl.ds(l0 * d4 + r, rows_m, stride=d4)
            m1, m4, m16 = sm1[tok, :], sm4[r, com, :], sm16[r, com, :]
            top = jnp.maximum(jnp.maximum(m1, m4), m16)
            e1, e4, e16 = jnp.exp2(m1 - top), jnp.exp2(m4 - top), jnp.exp2(m16 - top)
            den = e1 * sd1[tok, :] + e4 * sd4[r, com, :] + e16 * sd16[r, com, :]
            num = e1 * acc1[tok, :] + e4 * acc4[r, com, :] + e16 * acc16[r, com, :]
            stage[tok, :] = num * (1.0 / den)
        return carry

    lax.fori_loop(0, seq // d4 // rows_m, mix, 0)

    def emit(c, carry):
        rows = pl.ds(pl.multiple_of(c * conv_rows, conv_rows), conv_rows)
        o_ref[rows, :] = stage[rows, :].astype(BF16)
        return carry

    lax.fori_loop(0, seq // conv_rows, emit, 0)


def _dilated_attention(q, k, v, views, w_up, w_down, w_out, w_pg):
    B, S, W = q.shape
    d4 = DILATIONS[1]
    hp = W // LANES
    steps = B * hp
    d_model, d_ff = w_up.shape
    col_w, row_h, row_s = d_ff // steps, d_ff // steps, d_model // steps
    assert col_w % LANES == 0 and row_s % 16 == 0 and col_w * steps == d_ff and row_s * steps == d_model
    step = lambda b, h: b * hp + h
    w_specs = [pl.BlockSpec((d_model, col_w), lambda b, h: (0, step(b, h))),
               pl.BlockSpec((row_h, d_model), lambda b, h: (step(b, h), 0)),
               pl.BlockSpec((row_s, d_model), lambda b, h: (step(b, h), 0)),
               pl.BlockSpec((row_s, d_model), lambda b, h: (step(b, h), 0))]
    spec = pl.BlockSpec((None, S, LANES), lambda b, h: (b, 0, h))
    view_specs = [pl.BlockSpec((None, S // d, d * LANES), lambda b, h: (b, 0, h))
                  for d in DILATIONS[1:] for _ in range(3)]
    token = pltpu.VMEM((S, LANES), F32)
    by4 = pltpu.VMEM((d4, S // d4, LANES), F32)
    scratch = ([token]
               + [token, by4, by4] * 3
               + [pltpu.VMEM((2, 2 * ATTN_BLOCK, 2 * ATTN_BLOCK), F32)])
    return pl.pallas_call(
        functools.partial(_attn_kernel, seq=S),
        name="dilated_attention",
        grid=(B, hp),
        in_specs=[spec, spec, spec] + view_specs + w_specs,
        out_specs=[spec] + w_specs,
        out_shape=[jax.ShapeDtypeStruct((B, S, W), BF16)]
                  + [jax.ShapeDtypeStruct(w.shape, BF16) for w in (w_up, w_down, w_out, w_pg)],
        scratch_shapes=scratch,
        compiler_params=pltpu.CompilerParams(
            dimension_semantics=("parallel", "parallel"), vmem_limit_bytes=VMEM_LIMIT),
    )(q, k, v, *views, w_up, w_down, w_out, w_pg)


def _gate_scans(gt_ref, a_ref, b_ref, col_ref):
    L, H = MLSTM_CHUNK, MLSTM_HEADS
    rows = gt_ref.shape[1]
    gates = gt_ref[...]
    f_pre = gates[H:]
    b = (jnp.minimum(f_pre, 0.0) - jnp.log(1.0 + jnp.exp(-jnp.abs(f_pre)))) * LOG2E
    pos = lax.broadcasted_iota(jnp.int32, (H, rows), 1) % L
    for step in [1 << e for e in range(L.bit_length() - 1)]:
        b = b + jnp.where(pos >= step, pltpu.roll(b, step, axis=1), 0.0)
    a = gates[:H] * LOG2E - b
    amax = a
    for step in [1 << e for e in range(L.bit_length() - 1)]:
        amax = jnp.maximum(amax, jnp.where(pos >= step, pltpu.roll(amax, step, axis=1), NEG_INF))
    a_ref[...] = a
    b_ref[...] = b
    col_ref[...] = jnp.concatenate([b, amax, a, jnp.zeros((LANES - 3 * H, rows), F32)], axis=0).T


def _mlstm_tile(qk_ref, v_ref, mo_ref, ng_ref, out_ref, c_ref, m_ref, a_ref, b_ref, col_ref, *, rows, base, fresh):
    L, H, dh = MLSTM_CHUNK, MLSTM_HEADS, MLSTM_HEAD_DIM
    causal = lax.broadcasted_iota(jnp.int32, (L, L), 1) <= lax.broadcasted_iota(jnp.int32, (L, L), 0)
    ones_blk = jnp.ones((L, dh), BF16)
    wide = lambda x: jnp.concatenate([x, x], axis=1)

    def head_chunk(h, r0):
        cols = slice(h * dh, (h + 1) * dh)
        tok = slice(r0, r0 + L)
        q_b = qk_ref[tok, cols]
        k_b = qk_ref[tok, MLSTM_WIDTH + h * dh:MLSTM_WIDTH + (h + 1) * dh]
        v_b = v_ref[tok, cols]

        pos = pl.ds(pl.multiple_of(base + r0, L), L)
        column = lambda j: jnp.broadcast_to(col_ref[pos, j:j + 1], (L, dh))
        b_rep, amax_rep, a_rep = column(h), column(H + h), column(2 * H + h)
        a_row = a_ref[h:h + 1, pos]
        b_last = b_ref[h:h + 1, pos][:, L - 1:L]
        c_prev = c_ref[h]
        m_prev = m_ref[h, 0:1, :]
        if r0 == 0 and fresh is not None:
            c_prev = jnp.where(fresh, 0.0, c_prev)
            m_prev = jnp.where(fresh, 0.0, m_prev)
        mx = jnp.maximum(m_prev, amax_rep)

        w_intra = jnp.exp2(jnp.where(causal, a_row - wide(mx), NEG_INF))
        qk = lax.dot_general(q_b, k_b, (((1,), (1,)), ((), ())), preferred_element_type=F32) * w_intra
        res = (wide(jnp.exp2(m_prev - mx)) * jnp.dot(q_b, c_prev.astype(BF16), preferred_element_type=F32)
               + jnp.dot(qk.astype(BF16), jnp.concatenate([v_b, ones_blk], axis=1), preferred_element_type=F32))
        hid = res[:, :dh] / jnp.maximum(jnp.abs(res[:, dh:]), jnp.exp2(-(b_rep + mx)))

        m_new = b_last + jnp.maximum(m_prev, jnp.max(a_row, axis=1, keepdims=True))
        decay = jnp.exp2(b_last + m_prev - m_new)
        ws = jnp.exp2(b_last + a_rep - m_new)
        wv = jnp.concatenate([(v_b.astype(F32) * ws).astype(BF16), ws.astype(BF16)], axis=1)
        c_ref[h] = wide(decay) * c_prev + lax.dot_general(k_b, wv, (((0,), (0,)), ((), ())),
                                                          preferred_element_type=F32)
        m_ref[h] = jnp.broadcast_to(m_new, m_ref.shape[1:])

        hn = _rms(hid, ng_ref[:, cols])
        out_ref[tok, cols] = (jax.nn.sigmoid(mo_ref[tok, cols].astype(F32)) * hn).astype(BF16)

    for r0 in range(0, rows, L):
        for h in range(H):
            head_chunk(h, r0)


def _mlstm_tail_kernel(x_ref, at_ref, p_ref, gt_ref, qk0_ref, v0_ref, mo0_ref, qk_ref, v_ref, mo_ref,
                       ng_ref, wo_ref, g_mlp_ref, wu_ref, wd_ref, g_ple_ref, wpg_ref, wple_ref, g_fin_ref, o_ref,
                       ml_ref, c_ref, m_ref, a_ref, b_ref, col_ref, *, tm, ff_chunk, final_norm):
    state = (c_ref, m_ref, a_ref, b_ref, col_ref)
    n = pl.num_programs(1)
    t = pl.program_id(0) * n + pl.program_id(1)
    nxt = jnp.minimum(t + 1, pl.num_programs(0) * n - 1)
    nxt_tile = nxt % n

    @pl.when(t == 0)
    def _():
        c_ref[...] = jnp.zeros_like(c_ref)
        m_ref[...] = jnp.zeros_like(m_ref)
        _gate_scans(gt_ref, a_ref, b_ref, col_ref)
        _mlstm_tile(qk0_ref, v0_ref, mo0_ref, ng_ref, ml_ref, *state, rows=tm, base=0, fresh=None)

    @pl.when((nxt_tile == 0) & (nxt > t))
    def _():
        _gate_scans(gt_ref, a_ref, b_ref, col_ref)

    ml = ml_ref[...]
    _mlstm_tile(qk_ref, v_ref, mo_ref, ng_ref, ml_ref, *state, rows=tm, base=nxt_tile * tm, fresh=nxt_tile == 0)

    mix = jnp.concatenate([at_ref[...], ml], axis=1)
    h = x_ref[...] + jnp.dot(mix, wo_ref[...], preferred_element_type=F32)

    u = _rms(h, g_mlp_ref[...]).astype(BF16)
    for c0 in range(0, D_FF, ff_chunk):
        a = jnp.dot(u, wu_ref[:, c0:c0 + ff_chunk], preferred_element_type=F32)
        a = jnp.square(jnp.maximum(a, 0.0)).astype(BF16)
        h = h + jnp.dot(a, wd_ref[c0:c0 + ff_chunk, :], preferred_element_type=F32)

    gate = jax.nn.sigmoid(jnp.dot(_rms(h, g_ple_ref[...]).astype(BF16), wpg_ref[...], preferred_element_type=F32))
    h = h + gate * jnp.dot(p_ref[...].astype(BF16), wple_ref[...], preferred_element_type=F32)
    o_ref[...] = _rms(h, g_fin_ref[...]) if final_norm else h


def _mlstm_tail(x, attn, p, layer, mqk, mv, mo, gates_t, norm_g, w_out, g_mlp, w_up, w_down, g_ple, w_pg, w_ple,
                g_fin, final_norm, tm=512, ff_chunk=1024):
    B, S, D = x.shape
    H, dh, W = MLSTM_HEADS, MLSTM_HEAD_DIM, MLSTM_WIDTH
    n = S // tm
    tok = lambda width: pl.BlockSpec((None, tm, width), lambda b, i: (b, i, 0))
    first = lambda width: pl.BlockSpec((None, tm, width), lambda b, i: (0, 0, 0))

    def following(b, i):
        t = jnp.minimum(b * n + i + 1, B * n - 1)
        return t // n, t % n

    nxt = lambda width: pl.BlockSpec((None, tm, width), lambda b, i: (*following(b, i), 0))
    p_spec = pl.BlockSpec((None, None, tm, PLE_DIM), lambda b, i: (layer, b, i, 0))
    gt_spec = pl.BlockSpec((None, 2 * H, S), lambda b, i: (following(b, i)[0], 0, 0))
    return pl.pallas_call(
        functools.partial(_mlstm_tail_kernel, tm=tm, ff_chunk=ff_chunk, final_norm=final_norm),
        name="mlstm_tail",
        grid=(B, n),
        in_specs=[tok(D), tok(ATTN_WIDTH), p_spec, gt_spec,
                  first(2 * W), first(W), first(W),
                  nxt(2 * W), nxt(W), nxt(W), _resident(norm_g.shape),
                  _resident(w_out.shape), _resident((1, D)), _resident(w_up.shape), _resident(w_down.shape),
                  _resident((1, D)), _resident(w_pg.shape), _resident(w_ple.shape), _resident((1, D))],
        out_specs=tok(D),
        out_shape=jax.ShapeDtypeStruct((B, S, D), F32),
        scratch_shapes=[pltpu.VMEM((tm, W), BF16),
                        pltpu.VMEM((H, dh, 2 * dh), F32), pltpu.VMEM((H, 8, LANES), F32),
                        pltpu.VMEM((H, S), F32), pltpu.VMEM((H, S), F32), pltpu.VMEM((S, LANES), F32)],
        compiler_params=pltpu.CompilerParams(
            dimension_semantics=("arbitrary", "arbitrary"), vmem_limit_bytes=VMEM_LIMIT),
    )(x, attn, p, gates_t, mqk, mv, mo, mqk, mv, mo, norm_g,
      w_out, g_mlp, w_up, w_down, g_ple, w_pg, w_ple, g_fin)


def _rope_tables(seq):
    half = ROPE_DIM // 2
    inv_freq = jnp.power(ROPE_THETA, -jnp.arange(half, dtype=F32) / half)
    ang = jnp.arange(seq, dtype=jnp.int32).astype(F32)[:, None] * inv_freq[None, :]
    cos, sin = jnp.cos(ang), jnp.sin(ang)
    rest = ATTN_HEAD_DIM - ROPE_DIM
    cos_head = jnp.concatenate([cos, cos, jnp.ones((seq, rest), F32)], axis=1)
    sin_head = jnp.concatenate([-sin, sin, jnp.zeros((seq, rest), F32)], axis=1)
    reps = LANES // ATTN_HEAD_DIM
    return jnp.tile(cos_head, (1, reps)), jnp.tile(sin_head, (1, reps))


def kernel(x, p, norm_mix_g, w_in, conv_w, conv_b, gate_b, mlstm_norm_g, w_out, norm_mlp_g, w_up, w_down,
           norm_ple_g, w_ple_gate, w_ple, final_norm_g):
    B, S, D = x.shape
    depth = w_in.shape[0]
    cos_t, sin_t = _rope_tables(S)
    row = lambda v: v.reshape(1, -1).astype(F32)
    n_gate = 2 * MLSTM_HEADS
    h = x
    for layer in range(depth):
        gb = jnp.pad(gate_b[layer].astype(F32), (0, LANES - n_gate)).reshape(1, LANES)
        q, k, v, *views, mqk, mv, mo, gates_t = _in_proj(h, row(norm_mix_g[layer]), jnp.swapaxes(w_in[layer], 0, 1), gb,
                                                         cos_t, sin_t, conv_w[layer].astype(F32), row(conv_b[layer]))
        attn, w_up_b, w_down_b, w_out_b, w_pg_b = _dilated_attention(q, k, v, views, w_up[layer], w_down[layer],
                                                                     w_out[layer], w_ple_gate[layer])
        h = _mlstm_tail(h, attn, p, layer, mqk, mv, mo, gates_t, row(mlstm_norm_g[layer]),
                        w_out_b, row(norm_mlp_g[layer]), w_up_b, w_down_b, row(norm_ple_g[layer]),
                        w_pg_b, w_ple[layer].astype(BF16), row(final_norm_g), final_norm=(layer == depth - 1))
    return h
```

```python
import functools
import math

import jax
import jax.numpy as jnp
from jax import lax
from jax.experimental import pallas as pl
from jax.experimental.pallas import tpu as pltpu

D_MODEL = 1024
ATTN_HEADS = 8
ATTN_HEAD_DIM = 64
ATTN_WIDTH = ATTN_HEADS * ATTN_HEAD_DIM
MLSTM_HEADS = 4
MLSTM_HEAD_DIM = 128
MLSTM_WIDTH = MLSTM_HEADS * MLSTM_HEAD_DIM
DILATIONS = (1, 4, 16)
ATTN_BLOCK = 128
ATTN_UNROLL = 32
ROPE_THETA = 500000.0
ROPE_DIM = ATTN_HEAD_DIM // 4
CONV_WIDTH = 4
D_FF = 4 * D_MODEL
PLE_DIM = 256
RMS_EPS = 1e-6

LANES = 128
HEAD_PAIRS = ATTN_WIDTH // LANES
MAIN_WIDTH = 3 * ATTN_WIDTH + 4 * MLSTM_WIDTH
MLSTM_CHUNK = 256
CONV_HALO = 8
VMEM_LIMIT = 56 * 1024 * 1024

F32 = jnp.float32
BF16 = jnp.bfloat16
NEG_INF = float("-inf")
LOG2E = 1.4426950408889634


def _rms(x, g):
    return x * lax.rsqrt(jnp.mean(x * x, axis=-1, keepdims=True) + RMS_EPS) * g


def _resident(shape):
    return pl.BlockSpec(shape, lambda *_: (0,) * len(shape), pipeline_mode=pl.Buffered(1))


def _in_proj_kernel(x_ref, g_ref, w32_ref, gb_ref, cos_ref, sin_ref, cw_ref, cb_ref,
                    q4_ref, k4_ref, v4_ref, q16_ref, k16_ref, v16_ref,
                    mqk_ref, mv_ref, mo_ref, gt_ref, tail_ref, w_ref, wg_ref, tok_ref, by4_ref, *, tm, sub):
    @pl.when((pl.program_id(0) == 0) & (pl.program_id(1) == 0))
    def _():
        gain = g_ref[...]
        q_scale = LOG2E / math.sqrt(ATTN_HEAD_DIM)
        for c0 in range(0, MAIN_WIDTH, ATTN_WIDTH):
            w = w32_ref[c0:c0 + ATTN_WIDTH, :] * gain
            w_ref[:, c0:c0 + ATTN_WIDTH] = (w * q_scale if c0 == 0 else w).T.astype(BF16)
        n_gate = 2 * MLSTM_HEADS
        w_gates = w32_ref[MAIN_WIDTH:MAIN_WIDTH + n_gate, :] * gain
        wg_ref[...] = jnp.concatenate([w_gates, jnp.zeros((LANES - n_gate, w_gates.shape[1]), F32)],
                                      axis=0).T.astype(BF16)

    @pl.when(pl.program_id(1) == 0)
    def _():
        tail_ref[:, :CONV_HALO, :] = jnp.zeros((tail_ref.shape[0], CONV_HALO, LANES), F32)

    A, M = ATTN_WIDTH, MLSTM_WIDTH
    lane = lax.broadcasted_iota(jnp.int32, (sub, LANES), 1)
    first_half = (lane % ATTN_HEAD_DIM) < (ROPE_DIM // 2)
    k_scale = 1.0 / math.sqrt(MLSTM_HEAD_DIM)
    rb = 128
    d4, d16 = DILATIONS[1], DILATIONS[2]
    grp = 16 * d16

    for s0 in range(0, tm, sub):
        rows = slice(s0, s0 + sub)
        u = _rms(x_ref[rows, :], 1.0).astype(BF16)
        pos0 = pl.multiple_of(pl.program_id(1) * tm + s0, sub)
        cos = cos_ref[pl.ds(pos0, sub), :]
        sin = sin_ref[pl.ds(pos0, sub), :]

        def proj(c0, width, u=u):
            return jnp.dot(u, w_ref[:, c0:c0 + width], preferred_element_type=F32)

        def rope_store(y, a, cos=cos, sin=sin, rows=rows):
            for j in range(HEAD_PAIRS):
                yj = y[:, j * LANES:(j + 1) * LANES]
                partner = jnp.where(first_half,
                                    pltpu.roll(yj, LANES - ROPE_DIM // 2, axis=1),
                                    pltpu.roll(yj, ROPE_DIM // 2, axis=1))
                tok_ref[a, j, rows, :] = yj * cos + partner * sin

        y = proj(3 * A, 2 * M)
        for c0 in range(0, 2 * M, LANES):
            cols = slice(c0, c0 + LANES)
            slab = tail_ref.at[c0 // LANES]
            slab[CONV_HALO + s0:CONV_HALO + s0 + sub, :] = y[:, cols]
            for r0 in range(s0, s0 + sub, rb):
                acc = cb_ref[:, cols]
                for j in range(CONV_WIDTH):
                    start = CONV_HALO + r0 - j
                    acc = acc + cw_ref[CONV_WIDTH - 1 - j:CONV_WIDTH - j, cols] * slab[pl.ds(start, rb, stride=1), :]
                act = acc * jax.nn.sigmoid(acc)
                mqk_ref[r0:r0 + rb, cols] = (act if c0 < M else act * k_scale).astype(BF16)
        rope_store(proj(0, A), 0)
        rope_store(proj(A, A), 1)
        y = proj(2 * A, A)
        for j in range(HEAD_PAIRS):
            tok_ref[2, j, rows, :] = y[:, j * LANES:(j + 1) * LANES]
        mv_ref[rows, :] = proj(3 * A + 2 * M, M).astype(BF16)
        mo_ref[rows, :] = proj(3 * A + 3 * M, M).astype(BF16)
        gates = jnp.dot(u, wg_ref[...], preferred_element_type=F32) + gb_ref[...]
        gt_ref[:, rows] = gates.T[:2 * MLSTM_HEADS, :]
        if (s0 + sub) % grp == 0:
            g0 = s0 + sub - grp
            for a, (x4_ref, x16_ref) in enumerate(((q4_ref, q16_ref), (k4_ref, k16_ref), (v4_ref, v16_ref))):
                for j in range(HEAD_PAIRS):
                    for r in range(d4):
                        part = tok_ref[a, j, pl.ds(g0 + r, grp // d4, stride=d4), :]
                        by4_ref[a, j, r, g0 // d4:(g0 + grp) // d4, :] = part
                        x4_ref[g0 // d4:(g0 + grp) // d4, (j * d4 + r) * LANES:(j * d4 + r + 1) * LANES] = (
                            part.astype(BF16))
                    for r in range(d16):
                        x16_ref[g0 // d16:(g0 + grp) // d16, (j * d16 + r) * LANES:(j * d16 + r + 1) * LANES] = (
                            by4_ref[a, j, r % d4, pl.ds(g0 // d4 + r // d4, grp // d16, stride=d4), :].astype(BF16))
    tail_ref[:, :CONV_HALO, :] = tail_ref[:, tm:tm + CONV_HALO, :]


def _in_proj(x, g, w_in_t, gate_b, cos_t, sin_t, conv_w, conv_b, tm=512, sub=128):
    B, S, D = x.shape
    grid = (B, S // tm)
    d4, d16 = DILATIONS[1], DILATIONS[2]
    tok = lambda width: pl.BlockSpec((None, tm, width), lambda b, i: (b, i, 0))
    view = lambda d: pl.BlockSpec((None, tm // d, d * ATTN_WIDTH), lambda b, i: (b, i, 0))
    view_shape = lambda d: jax.ShapeDtypeStruct((B, S // d, d * ATTN_WIDTH), BF16)
    return pl.pallas_call(
        functools.partial(_in_proj_kernel, tm=tm, sub=sub),
        name="in_proj",
        grid=grid,
        in_specs=[tok(D), _resident((1, D)), _resident(w_in_t.shape),
                  _resident((1, LANES)), _resident(cos_t.shape), _resident(sin_t.shape),
                  _resident(conv_w.shape), _resident(conv_b.shape)],
        out_specs=[view(d4), view(d4), view(d4), view(d16), view(d16), view(d16),
                   tok(2 * MLSTM_WIDTH), tok(MLSTM_WIDTH), tok(MLSTM_WIDTH),
                   pl.BlockSpec((None, 2 * MLSTM_HEADS, tm), lambda b, i: (b, 0, i))],
        out_shape=[view_shape(d4), view_shape(d4), view_shape(d4),
                   view_shape(d16), view_shape(d16), view_shape(d16),
                   jax.ShapeDtypeStruct((B, S, 2 * MLSTM_WIDTH), BF16),
                   jax.ShapeDtypeStruct((B, S, MLSTM_WIDTH), BF16),
                   jax.ShapeDtypeStruct((B, S, MLSTM_WIDTH), BF16),
                   jax.ShapeDtypeStruct((B, 2 * MLSTM_HEADS, S), F32)],
        scratch_shapes=[pltpu.VMEM((2 * MLSTM_WIDTH // LANES, CONV_HALO + tm, LANES), F32),
                        pltpu.VMEM((D, MAIN_WIDTH), BF16), pltpu.VMEM((D, LANES), BF16),
                        pltpu.VMEM((3, HEAD_PAIRS, tm, LANES), F32),
                        pltpu.VMEM((3, HEAD_PAIRS, d4, tm // d4, LANES), F32)],
        compiler_params=pltpu.CompilerParams(
            dimension_semantics=("arbitrary", "arbitrary"), vmem_limit_bytes=VMEM_LIMIT),
    )(x, g, w_in_t, gate_b, cos_t, sin_t, conv_w, conv_b)


def _attn_kernel(q4, k4, v4, q16, k16, v16, wu32_ref, wd32_ref, wo32_ref, wpg32_ref,
                 o_ref, wu_ref, wd_ref, wo_ref, wpg_ref, stage,
                 acc1, acc4, acc16, sm1, sm4, sm16, sd1, sd4, sd16, bias_ref, *, seq):
    blk = ATTN_BLOCK
    d4, d16 = DILATIONS[1], DILATIONS[2]
    assert DILATIONS[0] == 1 and d16 == d4 * d4
    for src, dst in ((wu32_ref, wu_ref), (wd32_ref, wd_ref), (wo32_ref, wo_ref), (wpg32_ref, wpg_ref)):
        dst[...] = src[...].astype(BF16)
    qi = lax.broadcasted_iota(jnp.int32, (2 * blk, 2 * blk), 0) % blk
    ki = lax.broadcasted_iota(jnp.int32, (2 * blk, 2 * blk), 1)
    dist = qi + blk - ki
    valid = (dist >= 0) & (dist <= blk)
    bias_ref[1] = jnp.where(valid, 0.0, NEG_INF)
    bias_ref[0] = jnp.where(valid & (ki >= blk), 0.0, NEG_INF)
    sub = blk // d4
    kj = ki % blk
    dist = (d4 * (qi % sub) + qi // sub) + blk - (d4 * (kj % sub) + kj // sub + blk * (ki // blk))
    valid = (dist >= 0) & (dist <= blk)
    bias_ref[3] = jnp.where(valid, 0.0, NEG_INF)
    bias_ref[2] = jnp.where(valid & (ki >= blk), 0.0, NEG_INF)

    head0 = lax.broadcasted_iota(jnp.int32, (blk, LANES), 1) < ATTN_HEAD_DIM
    ones_kv = jnp.ones((2 * blk, LANES), BF16)

    conv_rows = 512
    patterns = ((q4, k4, v4, acc1, sm1, sd1, 1),
                (q4, k4, v4, acc4, sm4, sd4, d4),
                (q16, k16, v16, acc16, sm16, sd16, d16))

    def store(ref, n, r, d, val):
        row0 = pl.multiple_of(n * blk, blk)
        if d == 1:
            for rr in range(d4):
                ref[rr, pl.ds(pl.multiple_of(n * sub, sub), sub), :] = val[rr * sub:(rr + 1) * sub]
        elif d == d4:
            ref[r, pl.ds(row0, blk), :] = val
        else:
            ref[r % d4, pl.ds(row0 * d4 + r // d4, blk, stride=d4), :] = val

    def load(ref, n, r, d):
        if d == 1:
            l0 = pl.multiple_of(n * sub, sub)
            return jnp.concatenate([ref[pl.ds(l0, sub), rr * LANES:(rr + 1) * LANES] for rr in range(d4)], axis=0)
        return ref[pl.ds(pl.multiple_of(n * blk, blk), blk), pl.ds(pl.multiple_of(r * LANES, LANES), LANES)]

    for (qv, kv, vv_ref, acc_ref, sm_ref, sd_ref, d) in patterns:
        nb = seq // d // blk

        def block(i, carry, qv=qv, kv=kv, vv_ref=vv_ref, acc_ref=acc_ref, sm_ref=sm_ref, sd_ref=sd_ref, d=d, nb=nb):
            r = i // nb
            n = i % nb
            prev = jnp.maximum(n - 1, 0)
            q = load(qv, n, r, d)
            zero = jnp.zeros_like(q)
            q2 = jnp.concatenate([jnp.where(head0, q, zero), jnp.where(head0, zero, q)], axis=0)
            kk = jnp.concatenate([load(kv, prev, r, d), load(kv, n, r, d)], axis=0)
            vv = jnp.concatenate([load(vv_ref, prev, r, d), load(vv_ref, n, r, d)], axis=0)
            s = lax.dot_general(q2, kk, (((1,), (1,)), ((), ())), preferred_element_type=F32)
            s = s + bias_ref[jnp.minimum(n, 1) + (2 if d == 1 else 0)]
            m = jnp.max(s, axis=1, keepdims=True)
            p = jnp.exp2(s - m).astype(BF16)
            res = jnp.dot(p, jnp.concatenate([vv, ones_kv], axis=1), preferred_element_type=F32)
            store(acc_ref, n, r, d, jnp.where(head0, res[:blk, :LANES], res[blk:, :LANES]))
            store(sm_ref, n, r, d, jnp.where(head0, m[:blk], m[blk:]))
            store(sd_ref, n, r, d, jnp.where(head0, res[:blk, LANES:], res[blk:, LANES:]))
            return carry

        lax.fori_loop(0, d * nb, block, 0, unroll=ATTN_UNROLL)

    rows_m = 256

    def mix(c, carry):
        l0 = pl.multiple_of(c * rows_m, rows_m)
        com = pl.ds(l0, rows_m)
        for r in range(d4):
            tok = pl.ds(l0 * d4 + r, rows_m, stride=d4)
            m1, m4, m16 = sm1[r, com, :], sm4[r, com, :], sm16[r, com, :]
            top = jnp.maximum(jnp.maximum(m1, m4), m16)
            e1, e4, e16 = jnp.exp2(m1 - top), jnp.exp2(m4 - top), jnp.exp2(m16 - top)
            den = e1 * sd1[r, com, :] + e4 * sd4[r, com, :] + e16 * sd16[r, com, :]
            num = e1 * acc1[r, com, :] + e4 * acc4[r, com, :] + e16 * acc16[r, com, :]
            stage[tok, :] = num * (1.0 / den)
        return carry

    lax.fori_loop(0, seq // d4 // rows_m, mix, 0)

    def emit(c, carry):
        rows = pl.ds(pl.multiple_of(c * conv_rows, conv_rows), conv_rows)
        o_ref[rows, :] = stage[rows, :].astype(BF16)
        return carry

    lax.fori_loop(0, seq // conv_rows, emit, 0)


def _dilated_attention(views, w_up, w_down, w_out, w_pg):
    d4 = DILATIONS[1]
    B, S, W = views[0].shape[0], views[0].shape[1] * d4, views[0].shape[2] // d4
    hp = W // LANES
    steps = B * hp
    d_model, d_ff = w_up.shape
    col_w, row_h, row_s = d_ff // steps, d_ff // steps, d_model // steps
    assert col_w % LANES == 0 and row_s % 16 == 0 and col_w * steps == d_ff and row_s * steps == d_model
    step = lambda b, h: b * hp + h
    w_specs = [pl.BlockSpec((d_model, col_w), lambda b, h: (0, step(b, h))),
               pl.BlockSpec((row_h, d_model), lambda b, h: (step(b, h), 0)),
               pl.BlockSpec((row_s, d_model), lambda b, h: (step(b, h), 0)),
               pl.BlockSpec((row_s, d_model), lambda b, h: (step(b, h), 0))]
    spec = pl.BlockSpec((None, S, LANES), lambda b, h: (b, 0, h))
    view_specs = [pl.BlockSpec((None, S // d, d * LANES), lambda b, h: (b, 0, h))
                  for d in DILATIONS[1:] for _ in range(3)]
    token = pltpu.VMEM((S, LANES), F32)
    by4 = pltpu.VMEM((d4, S // d4, LANES), F32)
    scratch = ([token]
               + [by4, by4, by4] * 3
               + [pltpu.VMEM((4, 2 * ATTN_BLOCK, 2 * ATTN_BLOCK), F32)])
    return pl.pallas_call(
        functools.partial(_attn_kernel, seq=S),
        name="dilated_attention",
        grid=(B, hp),
        in_specs=view_specs + w_specs,
        out_specs=[spec] + w_specs,
        out_shape=[jax.ShapeDtypeStruct((B, S, W), BF16)]
                  + [jax.ShapeDtypeStruct(w.shape, BF16) for w in (w_up, w_down, w_out, w_pg)],
        scratch_shapes=scratch,
        compiler_params=pltpu.CompilerParams(
            dimension_semantics=("parallel", "parallel"), vmem_limit_bytes=VMEM_LIMIT),
    )(*views, w_up, w_down, w_out, w_pg)


def _gate_scans(gt_ref, a_ref, b_ref, col_ref):
    L, H = MLSTM_CHUNK, MLSTM_HEADS
    rows = gt_ref.shape[1]
    gates = gt_ref[...]
    f_pre = gates[H:]
    b = (jnp.minimum(f_pre, 0.0) - jnp.log(1.0 + jnp.exp(-jnp.abs(f_pre)))) * LOG2E
    pos = lax.broadcasted_iota(jnp.int32, (H, rows), 1) % L
    for step in [1 << e for e in range(L.bit_length() - 1)]:
        b = b + jnp.where(pos >= step, pltpu.roll(b, step, axis=1), 0.0)
    a = gates[:H] * LOG2E - b
    amax = a
    for step in [1 << e for e in range(L.bit_length() - 1)]:
        amax = jnp.maximum(amax, jnp.where(pos >= step, pltpu.roll(amax, step, axis=1), NEG_INF))
    a_ref[...] = a
    b_ref[...] = b
    col_ref[...] = jnp.concatenate([b, amax, a, jnp.zeros((LANES - 3 * H, rows), F32)], axis=0).T


def _mlstm_tile(qk_ref, v_ref, mo_ref, ng_ref, out_ref, c_ref, m_ref, a_ref, b_ref, col_ref, *, rows, base, fresh):
    L, H, dh = MLSTM_CHUNK, MLSTM_HEADS, MLSTM_HEAD_DIM
    causal = lax.broadcasted_iota(jnp.int32, (L, L), 1) <= lax.broadcasted_iota(jnp.int32, (L, L), 0)
    ones_blk = jnp.ones((L, dh), BF16)
    wide = lambda x: jnp.concatenate([x, x], axis=1)

    def head_chunk(h, r0):
        cols = slice(h * dh, (h + 1) * dh)
        tok = slice(r0, r0 + L)
        q_b = qk_ref[tok, cols]
        k_b = qk_ref[tok, MLSTM_WIDTH + h * dh:MLSTM_WIDTH + (h + 1) * dh]
        v_b = v_ref[tok, cols]

        pos = pl.ds(pl.multiple_of(base + r0, L), L)
        column = lambda j: jnp.broadcast_to(col_ref[pos, j:j + 1], (L, dh))
        b_rep, amax_rep, a_rep = column(h), column(H + h), column(2 * H + h)
        a_row = a_ref[h:h + 1, pos]
        b_last = b_ref[h:h + 1, pos][:, L - 1:L]
        c_prev = c_ref[h]
        m_prev = m_ref[h, 0:1, :]
        if r0 == 0 and fresh is not None:
            c_prev = jnp.where(fresh, 0.0, c_prev)
            m_prev = jnp.where(fresh, 0.0, m_prev)
        mx = jnp.maximum(m_prev, amax_rep)

        w_intra = jnp.exp2(jnp.where(causal, a_row - wide(mx), NEG_INF))
        qk = lax.dot_general(q_b, k_b, (((1,), (1,)), ((), ())), preferred_element_type=F32) * w_intra
        res = (wide(jnp.exp2(m_prev - mx)) * jnp.dot(q_b, c_prev.astype(BF16), preferred_element_type=F32)
               + jnp.dot(qk.astype(BF16), jnp.concatenate([v_b, ones_blk], axis=1), preferred_element_type=F32))
        hid = res[:, :dh] / jnp.maximum(jnp.abs(res[:, dh:]), jnp.exp2(-(b_rep + mx)))

        m_new = b_last + jnp.maximum(m_prev, jnp.max(a_row, axis=1, keepdims=True))
        decay = jnp.exp2(b_last + m_prev - m_new)
        ws = jnp.exp2(b_last + a_rep - m_new)
        wv = jnp.concatenate([(v_b.astype(F32) * ws).astype(BF16), ws.astype(BF16)], axis=1)
        c_ref[h] = wide(decay) * c_prev + lax.dot_general(k_b, wv, (((0,), (0,)), ((), ())),
                                                          preferred_element_type=F32)
        m_ref[h] = jnp.broadcast_to(m_new, m_ref.shape[1:])

        hn = _rms(hid, ng_ref[:, cols])
        out_ref[tok, cols] = (jax.nn.sigmoid(mo_ref[tok, cols].astype(F32)) * hn).astype(BF16)

    for r0 in range(0, rows, L):
        for h in range(H):
            head_chunk(h, r0)


def _mlstm_tail_kernel(x_ref, at_ref, p_ref, gt_ref, qk0_ref, v0_ref, mo0_ref, qk_ref, v_ref, mo_ref,
                       ng_ref, wo_ref, g_mlp_ref, wu_ref, wd_ref, g_ple_ref, wpg_ref, wple_ref, g_fin_ref, o_ref,
                       ml_ref, c_ref, m_ref, a_ref, b_ref, col_ref, *, tm, ff_chunk, final_norm):
    state = (c_ref, m_ref, a_ref, b_ref, col_ref)
    n = pl.num_programs(1)
    t = pl.program_id(0) * n + pl.program_id(1)
    nxt = jnp.minimum(t + 1, pl.num_programs(0) * n - 1)
    nxt_tile = nxt % n

    @pl.when(t == 0)
    def _():
        c_ref[...] = jnp.zeros_like(c_ref)
        m_ref[...] = jnp.zeros_like(m_ref)
        _gate_scans(gt_ref, a_ref, b_ref, col_ref)
        _mlstm_tile(qk0_ref, v0_ref, mo0_ref, ng_ref, ml_ref, *state, rows=tm, base=0, fresh=None)

    @pl.when((nxt_tile == 0) & (nxt > t))
    def _():
        _gate_scans(gt_ref, a_ref, b_ref, col_ref)

    ml = ml_ref[...]
    _mlstm_tile(qk_ref, v_ref, mo_ref, ng_ref, ml_ref, *state, rows=tm, base=nxt_tile * tm, fresh=nxt_tile == 0)

    mix = jnp.concatenate([at_ref[...], ml], axis=1)
    h = x_ref[...] + jnp.dot(mix, wo_ref[...], preferred_element_type=F32)

    u = _rms(h, g_mlp_ref[...]).astype(BF16)
    for c0 in range(0, D_FF, ff_chunk):
        a = jnp.dot(u, wu_ref[:, c0:c0 + ff_chunk], preferred_element_type=F32)
        a = jnp.square(jnp.maximum(a, 0.0)).astype(BF16)
        h = h + jnp.dot(a, wd_ref[c0:c0 + ff_chunk, :], preferred_element_type=F32)

    gate = jax.nn.sigmoid(jnp.dot(_rms(h, g_ple_ref[...]).astype(BF16), wpg_ref[...], preferred_element_type=F32))
    h = h + gate * jnp.dot(p_ref[...].astype(BF16), wple_ref[...], preferred_element_type=F32)
    o_ref[...] = _rms(h, g_fin_ref[...]) if final_norm else h


def _mlstm_tail(x, attn, p, layer, mqk, mv, mo, gates_t, norm_g, w_out, g_mlp, w_up, w_down, g_ple, w_pg, w_ple,
                g_fin, final_norm, tm=512, ff_chunk=1024):
    B, S, D = x.shape
    H, dh, W = MLSTM_HEADS, MLSTM_HEAD_DIM, MLSTM_WIDTH
    n = S // tm
    tok = lambda width: pl.BlockSpec((None, tm, width), lambda b, i: (b, i, 0))
    first = lambda width: pl.BlockSpec((None, tm, width), lambda b, i: (0, 0, 0))

    def following(b, i):
        t = jnp.minimum(b * n + i + 1, B * n - 1)
        return t // n, t % n

    nxt = lambda width: pl.BlockSpec((None, tm, width), lambda b, i: (*following(b, i), 0))
    p_spec = pl.BlockSpec((None, None, tm, PLE_DIM), lambda b, i: (layer, b, i, 0))
    gt_spec = pl.BlockSpec((None, 2 * H, S), lambda b, i: (following(b, i)[0], 0, 0))
    return pl.pallas_call(
        functools.partial(_mlstm_tail_kernel, tm=tm, ff_chunk=ff_chunk, final_norm=final_norm),
        name="mlstm_tail",
        grid=(B, n),
        in_specs=[tok(D), tok(ATTN_WIDTH), p_spec, gt_spec,
                  first(2 * W), first(W), first(W),
                  nxt(2 * W), nxt(W), nxt(W), _resident(norm_g.shape),
                  _resident(w_out.shape), _resident((1, D)), _resident(w_up.shape), _resident(w_down.shape),
                  _resident((1, D)), _resident(w_pg.shape), _resident(w_ple.shape), _resident((1, D))],
        out_specs=tok(D),
        out_shape=jax.ShapeDtypeStruct((B, S, D), F32),
        scratch_shapes=[pltpu.VMEM((tm, W), BF16),
                        pltpu.VMEM((H, dh, 2 * dh), F32), pltpu.VMEM((H, 8, LANES), F32),
                        pltpu.VMEM((H, S), F32), pltpu.VMEM((H, S), F32), pltpu.VMEM((S, LANES), F32)],
        compiler_params=pltpu.CompilerParams(
            dimension_semantics=("arbitrary", "arbitrary"), vmem_limit_bytes=VMEM_LIMIT),
    )(x, attn, p, gates_t, mqk, mv, mo, mqk, mv, mo, norm_g,
      w_out, g_mlp, w_up, w_down, g_ple, w_pg, w_ple, g_fin)


def _rope_tables(seq):
    half = ROPE_DIM // 2
    inv_freq = jnp.power(ROPE_THETA, -jnp.arange(half, dtype=F32) / half)
    ang = jnp.arange(seq, dtype=jnp.int32).astype(F32)[:, None] * inv_freq[None, :]
    cos, sin = jnp.cos(ang), jnp.sin(ang)
    rest = ATTN_HEAD_DIM - ROPE_DIM
    cos_head = jnp.concatenate([cos, cos, jnp.ones((seq, rest), F32)], axis=1)
    sin_head = jnp.concatenate([-sin, sin, jnp.zeros((seq, rest), F32)], axis=1)
    reps = LANES // ATTN_HEAD_DIM
    return jnp.tile(cos_head, (1, reps)), jnp.tile(sin_head, (1, reps))


def kernel(x, p, norm_mix_g, w_in, conv_w, conv_b, gate_b, mlstm_norm_g, w_out, norm_mlp_g, w_up, w_down,
           norm_ple_g, w_ple_gate, w_ple, final_norm_g):
    B, S, D = x.shape
    depth = w_in.shape[0]
    cos_t, sin_t = _rope_tables(S)
    row = lambda v: v.reshape(1, -1).astype(F32)
    n_gate = 2 * MLSTM_HEADS
    h = x
    for layer in range(depth):
        gb = jnp.pad(gate_b[layer].astype(F32), (0, LANES - n_gate)).reshape(1, LANES)
        *views, mqk, mv, mo, gates_t = _in_proj(h, row(norm_mix_g[layer]), jnp.swapaxes(w_in[layer], 0, 1), gb,
                                                cos_t, sin_t, conv_w[layer].astype(F32), row(conv_b[layer]))
        attn, w_up_b, w_down_b, w_out_b, w_pg_b = _dilated_attention(views, w_up[layer], w_down[layer],
                                                                     w_out[layer], w_ple_gate[layer])
        h = _mlstm_tail(h, attn, p, layer, mqk, mv, mo, gates_t, row(mlstm_norm_g[layer]),
                        w_out_b, row(norm_mlp_g[layer]), w_up_b, w_down_b, row(norm_ple_g[layer]),
                        w_pg_b, w_ple[layer].astype(BF16), row(final_norm_g), final_norm=(layer == depth - 1))
    return h
```

```python
import functools
import math

import jax
import jax.numpy as jnp
from jax import lax
from jax.experimental import pallas as pl
from jax.experimental.pallas import tpu as pltpu

D_MODEL = 1024
ATTN_HEADS = 8
ATTN_HEAD_DIM = 64
ATTN_WIDTH = ATTN_HEADS * ATTN_HEAD_DIM
MLSTM_HEADS = 4
MLSTM_HEAD_DIM = 128
MLSTM_WIDTH = MLSTM_HEADS * MLSTM_HEAD_DIM
DILATIONS = (1, 4, 16)
ATTN_BLOCK = 128
ATTN_UNROLL = 32
ROPE_THETA = 500000.0
ROPE_DIM = ATTN_HEAD_DIM // 4
CONV_WIDTH = 4
D_FF = 4 * D_MODEL
PLE_DIM = 256
RMS_EPS = 1e-6

LANES = 128
HEAD_PAIRS = ATTN_WIDTH // LANES
MAIN_WIDTH = 3 * ATTN_WIDTH + 4 * MLSTM_WIDTH
MLSTM_CHUNK = 256
CONV_HALO = 8
VMEM_LIMIT = 56 * 1024 * 1024

F32 = jnp.float32
BF16 = jnp.bfloat16
NEG_INF = float("-inf")
LOG2E = 1.4426950408889634


def _rms(x, g):
    return x * lax.rsqrt(jnp.mean(x * x, axis=-1, keepdims=True) + RMS_EPS) * g


def _resident(shape):
    return pl.BlockSpec(shape, lambda *_: (0,) * len(shape), pipeline_mode=pl.Buffered(1))


def _w_in_prep_kernel(g_ref, w32_ref, wgate32_ref, w_ref, wg_ref):
    gain = g_ref[...]
    q_scale = jnp.where(pl.program_id(0) == 0, LOG2E / math.sqrt(ATTN_HEAD_DIM), 1.0)
    w_ref[...] = (w32_ref[...] * gain * q_scale).T.astype(BF16)
    w_gates = wgate32_ref[...] * gain
    wg_ref[...] = jnp.concatenate([w_gates, jnp.zeros((LANES - w_gates.shape[0], w_gates.shape[1]), F32)],
                                  axis=0).T.astype(BF16)


def _w_in_prep(g, w_in_t):
    D = w_in_t.shape[1]
    n_gate = 2 * MLSTM_HEADS
    assert MAIN_WIDTH % n_gate == 0
    return pl.pallas_call(
        _w_in_prep_kernel,
        name="w_in_prep",
        grid=(MAIN_WIDTH // ATTN_WIDTH,),
        in_specs=[pl.BlockSpec((1, D), lambda i: (0, 0)),
                  pl.BlockSpec((ATTN_WIDTH, D), lambda i: (i, 0)),
                  pl.BlockSpec((n_gate, D), lambda i: (MAIN_WIDTH // n_gate, 0))],
        out_specs=[pl.BlockSpec((D, ATTN_WIDTH), lambda i: (0, i)),
                   pl.BlockSpec((D, LANES), lambda i: (0, 0))],
        out_shape=[jax.ShapeDtypeStruct((D, MAIN_WIDTH), BF16), jax.ShapeDtypeStruct((D, LANES), BF16)],
        compiler_params=pltpu.CompilerParams(dimension_semantics=("arbitrary",), vmem_limit_bytes=VMEM_LIMIT),
    )(g, w_in_t, w_in_t)


def _in_proj_kernel(x_ref, w_ref, wg_ref, gb_ref, cos_ref, sin_ref, cw_ref, cb_ref,
                    q4_ref, k4_ref, v4_ref, q16_ref, k16_ref, v16_ref,
                    mqk_ref, mv_ref, mo_ref, gt_ref, tail_ref, tok_ref, by4_ref, *, tm, sub):
    @pl.when(pl.program_id(1) == 0)
    def _():
        tail_ref[:, :CONV_HALO, :] = jnp.zeros((tail_ref.shape[0], CONV_HALO, LANES), F32)

    A, M = ATTN_WIDTH, MLSTM_WIDTH
    lane = lax.broadcasted_iota(jnp.int32, (sub, LANES), 1)
    first_half = (lane % ATTN_HEAD_DIM) < (ROPE_DIM // 2)
    k_scale = 1.0 / math.sqrt(MLSTM_HEAD_DIM)
    rb = 128
    d4, d16 = DILATIONS[1], DILATIONS[2]
    grp = 16 * d16

    for s0 in range(0, tm, sub):
        rows = slice(s0, s0 + sub)
        srows = slice(s0 % (2 * grp), s0 % (2 * grp) + sub)
        u = _rms(x_ref[rows, :], 1.0).astype(BF16)
        pos0 = pl.multiple_of(pl.program_id(1) * tm + s0, sub)
        cos = cos_ref[pl.ds(pos0, sub), :]
        sin = sin_ref[pl.ds(pos0, sub), :]

        def proj(c0, width, u=u):
            return jnp.dot(u, w_ref[:, c0:c0 + width], preferred_element_type=F32)

        def rope_store(y, a, cos=cos, sin=sin, rows=srows):
            for j in range(HEAD_PAIRS):
                yj = y[:, j * LANES:(j + 1) * LANES]
                partner = jnp.where(first_half,
                                    pltpu.roll(yj, LANES - ROPE_DIM // 2, axis=1),
                                    pltpu.roll(yj, ROPE_DIM // 2, axis=1))
                tok_ref[a, j, rows, :] = yj * cos + partner * sin

        y = proj(3 * A, 2 * M)
        for c0 in range(0, 2 * M, LANES):
            cols = slice(c0, c0 + LANES)
            slab = tail_ref.at[c0 // LANES]
            slab[CONV_HALO + s0:CONV_HALO + s0 + sub, :] = y[:, cols]
            for r0 in range(s0, s0 + sub, rb):
                acc = cb_ref[:, cols]
                for j in range(CONV_WIDTH):
                    start = CONV_HALO + r0 - j
                    acc = acc + cw_ref[CONV_WIDTH - 1 - j:CONV_WIDTH - j, cols] * slab[pl.ds(start, rb, stride=1), :]
                act = acc * jax.nn.sigmoid(acc)
                mqk_ref[r0:r0 + rb, cols] = (act if c0 < M else act * k_scale).astype(BF16)
        rope_store(proj(0, A), 0)
        rope_store(proj(A, A), 1)
        y = proj(2 * A, A)
        for j in range(HEAD_PAIRS):
            tok_ref[2, j, srows, :] = y[:, j * LANES:(j + 1) * LANES]
        mv_ref[rows, :] = proj(3 * A + 2 * M, M).astype(BF16)
        mo_ref[rows, :] = proj(3 * A + 3 * M, M).astype(BF16)
        gates = jnp.dot(u, wg_ref[...], preferred_element_type=F32) + gb_ref[...]
        gt_ref[:, rows] = gates.T[:2 * MLSTM_HEADS, :]
        if (s0 + sub) % grp == 0:
            g0 = s0 + sub - grp
            sg0 = g0 % (2 * grp)
            for a, (x4_ref, x16_ref) in enumerate(((q4_ref, q16_ref), (k4_ref, k16_ref), (v4_ref, v16_ref))):
                for j in range(HEAD_PAIRS):
                    for r in range(d4):
                        part = tok_ref[a, j, pl.ds(sg0 + r, grp // d4, stride=d4), :]
                        by4_ref[a, j, r, sg0 // d4:(sg0 + grp) // d4, :] = part
                        x4_ref[g0 // d4:(g0 + grp) // d4, (j * d4 + r) * LANES:(j * d4 + r + 1) * LANES] = (
                            part.astype(BF16))
                    for r in range(d16):
                        x16_ref[g0 // d16:(g0 + grp) // d16, (j * d16 + r) * LANES:(j * d16 + r + 1) * LANES] = (
                            by4_ref[a, j, r % d4, pl.ds(sg0 // d4 + r // d4, grp // d16, stride=d4), :].astype(BF16))
    tail_ref[:, :CONV_HALO, :] = tail_ref[:, tm:tm + CONV_HALO, :]


def _in_proj(x, w_b, wg_b, gate_b, cos_t, sin_t, conv_w, conv_b, tm=1024, sub=128):
    B, S, D = x.shape
    grid = (B, S // tm)
    d4, d16 = DILATIONS[1], DILATIONS[2]
    stage_rows = 2 * 16 * d16
    tok = lambda width: pl.BlockSpec((None, tm, width), lambda b, i: (b, i, 0))
    view = lambda d: pl.BlockSpec((None, tm // d, d * ATTN_WIDTH), lambda b, i: (b, i, 0))
    view_shape = lambda d: jax.ShapeDtypeStruct((B, S // d, d * ATTN_WIDTH), BF16)
    return pl.pallas_call(
        functools.partial(_in_proj_kernel, tm=tm, sub=sub),
        name="in_proj",
        grid=grid,
        in_specs=[tok(D), _resident(w_b.shape), _resident(wg_b.shape),
                  _resident((1, LANES)), _resident(cos_t.shape), _resident(sin_t.shape),
                  _resident(conv_w.shape), _resident(conv_b.shape)],
        out_specs=[view(d4), view(d4), view(d4), view(d16), view(d16), view(d16),
                   tok(2 * MLSTM_WIDTH), tok(MLSTM_WIDTH), tok(MLSTM_WIDTH),
                   pl.BlockSpec((None, 2 * MLSTM_HEADS, tm), lambda b, i: (b, 0, i))],
        out_shape=[view_shape(d4), view_shape(d4), view_shape(d4),
                   view_shape(d16), view_shape(d16), view_shape(d16),
                   jax.ShapeDtypeStruct((B, S, 2 * MLSTM_WIDTH), BF16),
                   jax.ShapeDtypeStruct((B, S, MLSTM_WIDTH), BF16),
                   jax.ShapeDtypeStruct((B, S, MLSTM_WIDTH), BF16),
                   jax.ShapeDtypeStruct((B, 2 * MLSTM_HEADS, S), F32)],
        scratch_shapes=[pltpu.VMEM((2 * MLSTM_WIDTH // LANES, CONV_HALO + tm, LANES), F32),
                        pltpu.VMEM((3, HEAD_PAIRS, stage_rows, LANES), F32),
                        pltpu.VMEM((3, HEAD_PAIRS, d4, stage_rows // d4, LANES), F32)],
        compiler_params=pltpu.CompilerParams(
            dimension_semantics=("arbitrary", "arbitrary"), vmem_limit_bytes=VMEM_LIMIT),
    )(x, w_b, wg_b, gate_b, cos_t, sin_t, conv_w, conv_b)


def _attn_kernel(q4, k4, v4, q16, k16, v16, wu32_ref, wd32_ref, wo32_ref, wpg32_ref,
                 o_ref, wu_ref, wd_ref, wo_ref, wpg_ref, stage,
                 acc1, acc4, acc16, sm1, sm4, sm16, sd1, sd4, sd16, bias_ref, *, seq):
    blk = ATTN_BLOCK
    d4, d16 = DILATIONS[1], DILATIONS[2]
    assert DILATIONS[0] == 1 and d16 == d4 * d4
    for src, dst in ((wu32_ref, wu_ref), (wd32_ref, wd_ref), (wo32_ref, wo_ref), (wpg32_ref, wpg_ref)):
        dst[...] = src[...].astype(BF16)
    qi = lax.broadcasted_iota(jnp.int32, (2 * blk, 2 * blk), 0) % blk
    ki = lax.broadcasted_iota(jnp.int32, (2 * blk, 2 * blk), 1)
    dist = qi + blk - ki
    valid = (dist >= 0) & (dist <= blk)
    bias_ref[1] = jnp.where(valid, 0.0, NEG_INF)
    bias_ref[0] = jnp.where(valid & (ki >= blk), 0.0, NEG_INF)
    sub = blk // d4
    kj = ki % blk
    dist = (d4 * (qi % sub) + qi // sub) + blk - (d4 * (kj % sub) + kj // sub + blk * (ki // blk))
    valid = (dist >= 0) & (dist <= blk)
    bias_ref[3] = jnp.where(valid, 0.0, NEG_INF)
    bias_ref[2] = jnp.where(valid & (ki >= blk), 0.0, NEG_INF)

    head0 = lax.broadcasted_iota(jnp.int32, (blk, LANES), 1) < ATTN_HEAD_DIM
    ones_kv = jnp.ones((2 * blk, LANES), BF16)

    conv_rows = 512
    patterns = ((q4, k4, v4, acc1, sm1, sd1, 1),
                (q4, k4, v4, acc4, sm4, sd4, d4),
                (q16, k16, v16, acc16, sm16, sd16, d16))

    def store(ref, n, r, d, val):
        row0 = pl.multiple_of(n * blk, blk)
        if d == 1:
            for rr in range(d4):
                ref[rr, pl.ds(pl.multiple_of(n * sub, sub), sub), :] = val[rr * sub:(rr + 1) * sub]
        elif d == d4:
            ref[r, pl.ds(row0, blk), :] = val
        else:
            ref[r % d4, pl.ds(row0 * d4 + r // d4, blk, stride=d4), :] = val

    def load(ref, n, r, d):
        if d == 1:
            l0 = pl.multiple_of(n * sub, sub)
            return jnp.concatenate([ref[pl.ds(l0, sub), rr * LANES:(rr + 1) * LANES] for rr in range(d4)], axis=0)
        return ref[pl.ds(pl.multiple_of(n * blk, blk), blk), pl.ds(pl.multiple_of(r * LANES, LANES), LANES)]

    for (qv, kv, vv_ref, acc_ref, sm_ref, sd_ref, d) in patterns:
        nb = seq // d // blk

        def block(i, carry, qv=qv, kv=kv, vv_ref=vv_ref, acc_ref=acc_ref, sm_ref=sm_ref, sd_ref=sd_ref, d=d, nb=nb):
            r = i // nb
            n = i % nb
            prev = jnp.maximum(n - 1, 0)
            q = load(qv, n, r, d)
            zero = jnp.zeros_like(q)
            q2 = jnp.concatenate([jnp.where(head0, q, zero), jnp.where(head0, zero, q)], axis=0)
            kk = jnp.concatenate([load(kv, prev, r, d), load(kv, n, r, d)], axis=0)
            vv = jnp.concatenate([load(vv_ref, prev, r, d), load(vv_ref, n, r, d)], axis=0)
            s = lax.dot_general(q2, kk, (((1,), (1,)), ((), ())), preferred_element_type=F32)
            s = s + bias_ref[jnp.minimum(n, 1) + (2 if d == 1 else 0)]
            m = jnp.max(s, axis=1, keepdims=True)
            p = jnp.exp2(s - m).astype(BF16)
            res = jnp.dot(p, jnp.concatenate([vv, ones_kv], axis=1), preferred_element_type=F32)
            store(acc_ref, n, r, d, jnp.where(head0, res[:blk, :LANES], res[blk:, :LANES]))
            store(sm_ref, n, r, d, jnp.where(head0, m[:blk], m[blk:]))
            store(sd_ref, n, r, d, jnp.where(head0, res[:blk, LANES:], res[blk:, LANES:]))
            return carry

        lax.fori_loop(0, d * nb, block, 0, unroll=ATTN_UNROLL)

    rows_m = 256

    def mix(c, carry):
        l0 = pl.multiple_of(c * rows_m, rows_m)
        com = pl.ds(l0, rows_m)
        for r in range(d4):
            tok = pl.ds(l0 * d4 + r, rows_m, stride=d4)
            m1, m4, m16 = sm1[r, com, :], sm4[r, com, :], sm16[r, com, :]
            top = jnp.maximum(jnp.maximum(m1, m4), m16)
            e1, e4, e16 = jnp.exp2(m1 - top), jnp.exp2(m4 - top), jnp.exp2(m16 - top)
            den = e1 * sd1[r, com, :] + e4 * sd4[r, com, :] + e16 * sd16[r, com, :]
            num = e1 * acc1[r, com, :] + e4 * acc4[r, com, :] + e16 * acc16[r, com, :]
            stage[tok, :] = num * (1.0 / den)
        return carry

    lax.fori_loop(0, seq // d4 // rows_m, mix, 0)

    def emit(c, carry):
        rows = pl.ds(pl.multiple_of(c * conv_rows, conv_rows), conv_rows)
        o_ref[rows, :] = stage[rows, :].astype(BF16)
        return carry

    lax.fori_loop(0, seq // conv_rows, emit, 0)


def _dilated_attention(views, w_up, w_down, w_out, w_pg):
    d4 = DILATIONS[1]
    B, S, W = views[0].shape[0], views[0].shape[1] * d4, views[0].shape[2] // d4
    hp = W // LANES
    steps = B * hp
    d_model, d_ff = w_up.shape
    col_w, row_h, row_s = d_ff // steps, d_ff // steps, d_model // steps
    assert col_w % LANES == 0 and row_s % 16 == 0 and col_w * steps == d_ff and row_s * steps == d_model
    step = lambda b, h: b * hp + h
    w_specs = [pl.BlockSpec((d_model, col_w), lambda b, h: (0, step(b, h))),
               pl.BlockSpec((row_h, d_model), lambda b, h: (step(b, h), 0)),
               pl.BlockSpec((row_s, d_model), lambda b, h: (step(b, h), 0)),
               pl.BlockSpec((row_s, d_model), lambda b, h: (step(b, h), 0))]
    spec = pl.BlockSpec((None, S, LANES), lambda b, h: (b, 0, h))
    view_specs = [pl.BlockSpec((None, S // d, d * LANES), lambda b, h: (b, 0, h))
                  for d in DILATIONS[1:] for _ in range(3)]
    token = pltpu.VMEM((S, LANES), F32)
    by4 = pltpu.VMEM((d4, S // d4, LANES), F32)
    scratch = ([token]
               + [by4, by4, by4] * 3
               + [pltpu.VMEM((4, 2 * ATTN_BLOCK, 2 * ATTN_BLOCK), F32)])
    return pl.pallas_call(
        functools.partial(_attn_kernel, seq=S),
        name="dilated_attention",
        grid=(B, hp),
        in_specs=view_specs + w_specs,
        out_specs=[spec] + w_specs,
        out_shape=[jax.ShapeDtypeStruct((B, S, W), BF16)]
                  + [jax.ShapeDtypeStruct(w.shape, BF16) for w in (w_up, w_down, w_out, w_pg)],
        scratch_shapes=scratch,
        compiler_params=pltpu.CompilerParams(
            dimension_semantics=("parallel", "parallel"), vmem_limit_bytes=VMEM_LIMIT),
    )(*views, w_up, w_down, w_out, w_pg)


def _gate_scans(gt_ref, a_ref, b_ref, col_ref):
    L, H = MLSTM_CHUNK, MLSTM_HEADS
    rows = gt_ref.shape[1]
    gates = gt_ref[...]
    f_pre = gates[H:]
    b = (jnp.minimum(f_pre, 0.0) - jnp.log(1.0 + jnp.exp(-jnp.abs(f_pre)))) * LOG2E
    pos = lax.broadcasted_iota(jnp.int32, (H, rows), 1) % L
    for step in [1 << e for e in range(L.bit_length() - 1)]:
        b = b + jnp.where(pos >= step, pltpu.roll(b, step, axis=1), 0.0)
    a = gates[:H] * LOG2E - b
    amax = a
    for step in [1 << e for e in range(L.bit_length() - 1)]:
        amax = jnp.maximum(amax, jnp.where(pos >= step, pltpu.roll(amax, step, axis=1), NEG_INF))
    a_ref[...] = a
    b_ref[...] = b
    col_ref[...] = jnp.concatenate([b, amax, a, jnp.zeros((LANES - 3 * H, rows), F32)], axis=0).T


def _mlstm_tile(qk_ref, v_ref, mo_ref, ng_ref, out_ref, c_ref, m_ref, a_ref, b_ref, col_ref, *, rows, base, fresh):
    L, H, dh = MLSTM_CHUNK, MLSTM_HEADS, MLSTM_HEAD_DIM
    causal = lax.broadcasted_iota(jnp.int32, (L, L), 1) <= lax.broadcasted_iota(jnp.int32, (L, L), 0)
    ones_blk = jnp.ones((L, dh), BF16)
    wide = lambda x: jnp.concatenate([x, x], axis=1)

    def head_chunk(h, r0):
        cols = slice(h * dh, (h + 1) * dh)
        tok = slice(r0, r0 + L)
        q_b = qk_ref[tok, cols]
        k_b = qk_ref[tok, MLSTM_WIDTH + h * dh:MLSTM_WIDTH + (h + 1) * dh]
        v_b = v_ref[tok, cols]

        pos = pl.ds(pl.multiple_of(base + r0, L), L)
        column = lambda j: jnp.broadcast_to(col_ref[pos, j:j + 1], (L, dh))
        b_rep, amax_rep, a_rep = column(h), column(H + h), column(2 * H + h)
        a_row = a_ref[h:h + 1, pos]
        b_last = b_ref[h:h + 1, pos][:, L - 1:L]
        c_prev = c_ref[h]
        m_prev = m_ref[h, 0:1, :]
        if r0 == 0 and fresh is not None:
            c_prev = jnp.where(fresh, 0.0, c_prev)
            m_prev = jnp.where(fresh, 0.0, m_prev)
        mx = jnp.maximum(m_prev, amax_rep)

        w_intra = jnp.exp2(jnp.where(causal, a_row - wide(mx), NEG_INF))
        qk = lax.dot_general(q_b, k_b, (((1,), (1,)), ((), ())), preferred_element_type=F32) * w_intra
        res = (wide(jnp.exp2(m_prev - mx)) * jnp.dot(q_b, c_prev.astype(BF16), preferred_element_type=F32)
               + jnp.dot(qk.astype(BF16), jnp.concatenate([v_b, ones_blk], axis=1), preferred_element_type=F32))
        hid = res[:, :dh] / jnp.maximum(jnp.abs(res[:, dh:]), jnp.exp2(-(b_rep + mx)))

        m_new = b_last + jnp.maximum(m_prev, jnp.max(a_row, axis=1, keepdims=True))
        decay = jnp.exp2(b_last + m_prev - m_new)
        ws = jnp.exp2(b_last + a_rep - m_new)
        wv = jnp.concatenate([(v_b.astype(F32) * ws).astype(BF16), ws.astype(BF16)], axis=1)
        c_ref[h] = wide(decay) * c_prev + lax.dot_general(k_b, wv, (((0,), (0,)), ((), ())),
                                                          preferred_element_type=F32)
        m_ref[h] = jnp.broadcast_to(m_new, m_ref.shape[1:])

        hn = _rms(hid, ng_ref[:, cols])
        out_ref[tok, cols] = (jax.nn.sigmoid(mo_ref[tok, cols].astype(F32)) * hn).astype(BF16)

    for r0 in range(0, rows, L):
        for h in range(H):
            head_chunk(h, r0)


def _mlstm_tail_kernel(x_ref, at_ref, p_ref, gt_ref, qk0_ref, v0_ref, mo0_ref, qk_ref, v_ref, mo_ref,
                       ng_ref, wo_ref, g_mlp_ref, wu_ref, wd_ref, g_ple_ref, wpg_ref, wple_ref, g_fin_ref, o_ref,
                       ml_ref, c_ref, m_ref, a_ref, b_ref, col_ref, *, tm, ff_chunk, final_norm):
    state = (c_ref, m_ref, a_ref, b_ref, col_ref)
    n = pl.num_programs(1)
    t = pl.program_id(0) * n + pl.program_id(1)
    nxt = jnp.minimum(t + 1, pl.num_programs(0) * n - 1)
    nxt_tile = nxt % n

    @pl.when(t == 0)
    def _():
        c_ref[...] = jnp.zeros_like(c_ref)
        m_ref[...] = jnp.zeros_like(m_ref)
        _gate_scans(gt_ref, a_ref, b_ref, col_ref)
        _mlstm_tile(qk0_ref, v0_ref, mo0_ref, ng_ref, ml_ref, *state, rows=tm, base=0, fresh=None)

    @pl.when((nxt_tile == 0) & (nxt > t))
    def _():
        _gate_scans(gt_ref, a_ref, b_ref, col_ref)

    ml = ml_ref[...]
    _mlstm_tile(qk_ref, v_ref, mo_ref, ng_ref, ml_ref, *state, rows=tm, base=nxt_tile * tm, fresh=nxt_tile == 0)

    mix = jnp.concatenate([at_ref[...], ml], axis=1)
    h = x_ref[...] + jnp.dot(mix, wo_ref[...], preferred_element_type=F32)

    u = _rms(h, g_mlp_ref[...]).astype(BF16)
    for c0 in range(0, D_FF, ff_chunk):
        a = jnp.dot(u, wu_ref[:, c0:c0 + ff_chunk], preferred_element_type=F32)
        a = jnp.square(jnp.maximum(a, 0.0)).astype(BF16)
        h = h + jnp.dot(a, wd_ref[c0:c0 + ff_chunk, :], preferred_element_type=F32)

    gate = jax.nn.sigmoid(jnp.dot(_rms(h, g_ple_ref[...]).astype(BF16), wpg_ref[...], preferred_element_type=F32))
    h = h + gate * jnp.dot(p_ref[...].astype(BF16), wple_ref[...], preferred_element_type=F32)
    o_ref[...] = _rms(h, g_fin_ref[...]) if final_norm else h


def _mlstm_tail(x, attn, p, layer, mqk, mv, mo, gates_t, norm_g, w_out, g_mlp, w_up, w_down, g_ple, w_pg, w_ple,
                g_fin, final_norm, tm=512, ff_chunk=1024):
    B, S, D = x.shape
    H, dh, W = MLSTM_HEADS, MLSTM_HEAD_DIM, MLSTM_WIDTH
    n = S // tm
    tok = lambda width: pl.BlockSpec((None, tm, width), lambda b, i: (b, i, 0))
    first = lambda width: pl.BlockSpec((None, tm, width), lambda b, i: (0, 0, 0))

    def following(b, i):
        t = jnp.minimum(b * n + i + 1, B * n - 1)
        return t // n, t % n

    nxt = lambda width: pl.BlockSpec((None, tm, width), lambda b, i: (*following(b, i), 0))
    p_spec = pl.BlockSpec((None, None, tm, PLE_DIM), lambda b, i: (layer, b, i, 0))
    gt_spec = pl.BlockSpec((None, 2 * H, S), lambda b, i: (following(b, i)[0], 0, 0))
    return pl.pallas_call(
        functools.partial(_mlstm_tail_kernel, tm=tm, ff_chunk=ff_chunk, final_norm=final_norm),
        name="mlstm_tail",
        grid=(B, n),
        in_specs=[tok(D), tok(ATTN_WIDTH), p_spec, gt_spec,
                  first(2 * W), first(W), first(W),
                  nxt(2 * W), nxt(W), nxt(W), _resident(norm_g.shape),
                  _resident(w_out.shape), _resident((1, D)), _resident(w_up.shape), _resident(w_down.shape),
                  _resident((1, D)), _resident(w_pg.shape), _resident(w_ple.shape), _resident((1, D))],
        out_specs=tok(D),
        out_shape=jax.ShapeDtypeStruct((B, S, D), F32),
        scratch_shapes=[pltpu.VMEM((tm, W), BF16),
                        pltpu.VMEM((H, dh, 2 * dh), F32), pltpu.VMEM((H, 8, LANES), F32),
                        pltpu.VMEM((H, S), F32), pltpu.VMEM((H, S), F32), pltpu.VMEM((S, LANES), F32)],
        compiler_params=pltpu.CompilerParams(
            dimension_semantics=("arbitrary", "arbitrary"), vmem_limit_bytes=VMEM_LIMIT),
    )(x, attn, p, gates_t, mqk, mv, mo, mqk, mv, mo, norm_g,
      w_out, g_mlp, w_up, w_down, g_ple, w_pg, w_ple, g_fin)


def _rope_tables(seq):
    half = ROPE_DIM // 2
    inv_freq = jnp.power(ROPE_THETA, -jnp.arange(half, dtype=F32) / half)
    ang = jnp.arange(seq, dtype=jnp.int32).astype(F32)[:, None] * inv_freq[None, :]
    cos, sin = jnp.cos(ang), jnp.sin(ang)
    rest = ATTN_HEAD_DIM - ROPE_DIM
    cos_head = jnp.concatenate([cos, cos, jnp.ones((seq, rest), F32)], axis=1)
    sin_head = jnp.concatenate([-sin, sin, jnp.zeros((seq, rest), F32)], axis=1)
    reps = LANES // ATTN_HEAD_DIM
    return jnp.tile(cos_head, (1, reps)), jnp.tile(sin_head, (1, reps))


def kernel(x, p, norm_mix_g, w_in, conv_w, conv_b, gate_b, mlstm_norm_g, w_out, norm_mlp_g, w_up, w_down,
           norm_ple_g, w_ple_gate, w_ple, final_norm_g):
    B, S, D = x.shape
    depth = w_in.shape[0]
    cos_t, sin_t = _rope_tables(S)
    row = lambda v: v.reshape(1, -1).astype(F32)
    n_gate = 2 * MLSTM_HEADS
    h = x
    for layer in range(depth):
        gb = jnp.pad(gate_b[layer].astype(F32), (0, LANES - n_gate)).reshape(1, LANES)
        w_b, wg_b = _w_in_prep(row(norm_mix_g[layer]), jnp.swapaxes(w_in[layer], 0, 1))
        *views, mqk, mv, mo, gates_t = _in_proj(h, w_b, wg_b, gb,
                                                cos_t, sin_t, conv_w[layer].astype(F32), row(conv_b[layer]))
        attn, w_up_b, w_down_b, w_out_b, w_pg_b = _dilated_attention(views, w_up[layer], w_down[layer],
                                                                     w_out[layer], w_ple_gate[layer])
        h = _mlstm_tail(h, attn, p, layer, mqk, mv, mo, gates_t, row(mlstm_norm_g[layer]),
                        w_out_b, row(norm_mlp_g[layer]), w_up_b, w_down_b, row(norm_ple_g[layer]),
                        w_pg_b, w_ple[layer].astype(BF16), row(final_norm_g), final_norm=(layer == depth - 1))
    return h
```
